```python
import math
import jax, jax.numpy as jnp
from jax import lax
import numpy as np

D_MODEL = 1024
BATCH = 8
SEQ = 2048
DEPTH = 2

RET_HEADS = 4
RET_DK = 64
RET_DV = 128
RET_CHUNK = 128
DSA_HEADS = 8
DSA_DH = 64
DSA_Q_RANK = 256
DSA_KV_RANK = 128
IDX_HEADS = 8
IDX_DIM = 64
DSA_TOPK_MAX = 256
Q_BLOCK = 128
HGRN_HEADS = 4
HGRN_EXPAND = 128
HGRN_DV = 128
HGRN_CHUNK = 32
F_FLOOR = 1e-6
REL_BUCKETS = 32
REL_MAX_DIST = 128
D_FF = 2816
CONV_WIDTH = 3
EPS = 1e-6
NEG_BIG = -1e30

RET_W = RET_HEADS * RET_DV
DSA_W = DSA_HEADS * DSA_DH
HGRN_KW = HGRN_HEADS * HGRN_EXPAND
HGRN_W = HGRN_HEADS * HGRN_DV
N_BRANCH = 3
IN_WIDTHS = (RET_HEADS * RET_DK, RET_HEADS * RET_DK, RET_W, RET_W,
             DSA_Q_RANK, DSA_KV_RANK, IDX_DIM, IDX_HEADS,
             HGRN_KW, HGRN_KW, HGRN_W, HGRN_W,
             N_BRANCH * D_MODEL)
D_IN = sum(IN_WIDTHS)
DSA_UQ_OUT = DSA_HEADS * DSA_DH + IDX_HEADS * IDX_DIM

kernel_name = "hybrid_ret_dsa_hgrn2_gated"

F32 = jnp.float32


def rms_norm(x, g):
    xf = x.astype(F32)
    y = xf * lax.rsqrt(jnp.mean(xf * xf, axis=-1, keepdims=True) + EPS)
    return (y * g.astype(F32)).astype(x.dtype)


def split_columns(z):
    offsets, acc = [], 0
    for w in IN_WIDTHS[:-1]:
        acc += w
        offsets.append(acc)
    return jnp.split(z, offsets, axis=-1)


def rotary(x, pos):
    half = x.shape[-1] // 2
    freq = 1.0 / (10000.0 ** jnp.linspace(0.0, 1.0, half, dtype=F32))
    ang = pos[:, None] * freq[None, :]
    cos = jnp.cos(ang)[None, :, None, :]
    sin = jnp.sin(ang)[None, :, None, :]
    x1, x2 = x[..., :half], x[..., half:]
    return jnp.concatenate([x1 * cos - x2 * sin, x1 * sin + x2 * cos], axis=-1)


def masked_softmax(logits, mask):
    z = jnp.where(mask, logits, NEG_BIG)
    e = jnp.exp(z - jnp.max(z, axis=-1, keepdims=True))
    e = jnp.where(mask, e, 0.0)
    return e / jnp.sum(e, axis=-1, keepdims=True)


def retention_branch(q, k, v, g):
    B, T, _ = q.shape
    H, dk, dv = RET_HEADS, RET_DK, RET_DV
    pos = jnp.arange(T, dtype=F32)
    q = rotary(q.reshape(B, T, H, dk).astype(F32), pos)
    k = rotary(k.reshape(B, T, H, dk).astype(F32), pos) * dk ** -0.5
    v = v.reshape(B, T, H, dv).astype(F32)
    log_gamma = jnp.log1p(-(2.0 ** (-5.0 - jnp.arange(H, dtype=F32))))
    C = min(RET_CHUNK, T)
    n = T // C
    i = jnp.arange(C, dtype=F32)
    rel = i[:, None] - i[None, :]
    dmask = jnp.where(rel >= 0, jnp.exp(jnp.maximum(rel, 0.0)[None] * log_gamma[:, None, None]), 0.0)
    xi = jnp.exp((i + 1.0)[None, :] * log_gamma[:, None])
    zeta = jnp.exp((C - 1.0 - i)[None, :] * log_gamma[:, None])
    gamma_c = jnp.exp(C * log_gamma)

    def chunks(a):
        return a.reshape(B, n, C, H, a.shape[-1]).transpose(1, 0, 3, 2, 4)

    def step(R, inp):
        qi, ki, vi = inp
        inner = jnp.einsum('bhts,bhse->bhte', jnp.einsum('bhtd,bhsd->bhts', qi, ki) * dmask, vi)
        cross = jnp.einsum('bhtd,bhde->bhte', qi, R) * xi[None, :, :, None]
        R = gamma_c[None, :, None, None] * R + jnp.einsum('bhsd,bhse->bhde', ki * zeta[None, :, :, None], vi)
        return R, inner + cross

    _, o = lax.scan(step, jnp.zeros((B, H, dk, dv), F32), (chunks(q), chunks(k), chunks(v)))
    o = o.transpose(1, 0, 3, 2, 4).reshape(B, T, H, dv)
    mu = jnp.mean(o, axis=-1, keepdims=True)
    var = jnp.mean((o - mu) ** 2, axis=-1, keepdims=True)
    o = ((o - mu) * lax.rsqrt(var + EPS)).reshape(B, T, H * dv)
    return (jax.nn.silu(g.astype(F32)) * o).astype(g.dtype)


def rel_bucket(n):
    max_exact = REL_BUCKETS // 2
    nf = jnp.maximum(n, max_exact).astype(F32)
    large = max_exact + (jnp.log(nf / max_exact) / math.log(REL_MAX_DIST / max_exact)
                         * (REL_BUCKETS - max_exact)).astype(jnp.int32)
    large = jnp.minimum(large, REL_BUCKETS - 1)
    return jnp.where(n < max_exact, n, large)


def dsa_branch(c_q, c_kv, idx_k, idx_w, q_norm, kv_norm, w_uq, w_uk, w_uv, rel_bias):
    B, T, _ = c_q.shape
    H, dh, HI, dI = DSA_HEADS, DSA_DH, IDX_HEADS, IDX_DIM
    c_q = rms_norm(c_q, q_norm)
    c_kv = rms_norm(c_kv, kv_norm)
    qq = c_q @ w_uq
    q = qq[..., :H * dh].reshape(B, T, H, dh)
    q_idx = qq[..., H * dh:].reshape(B, T, HI, dI)
    q_lat = jnp.einsum('bthd,hdc->bthc', q, w_uk) * dh ** -0.5
    w_idx = idx_w * (HI * dI) ** -0.5
    Qb = min(Q_BLOCK, T)
    nb = T // Qb
    K = min(DSA_TOPK_MAX, T // 4)
    key_pos = jnp.arange(T, dtype=jnp.int32)
    starts = jnp.arange(nb, dtype=jnp.int32) * Qb

    def blocks(a):
        return a.reshape((B, nb, Qb) + a.shape[2:]).swapaxes(0, 1)

    def attend_block(inp):
        ql, qi, wi, t0 = inp
        t = t0 + jnp.arange(Qb, dtype=jnp.int32)
        score = jnp.einsum('bqsh,bqh->bqs', jax.nn.relu(jnp.einsum('bqhd,bsd->bqsh', qi, idx_k)), wi)
        score = jnp.where(key_pos[None, None, :] <= t[None, :, None], score.astype(F32), NEG_BIG)
        _, sel = lax.top_k(score, K)
        valid = sel <= t[None, :, None]
        c_sel = jax.vmap(lambda c, s: c[s])(c_kv, sel)
        bias = jnp.take(rel_bias, rel_bucket(jnp.maximum(t[None, :, None] - sel, 0)), axis=0)
        logits = jnp.einsum('bqhc,bqkc->bqhk', ql, c_sel).astype(F32) + jnp.moveaxis(bias, -1, 2).astype(F32)
        probs = masked_softmax(logits, valid[:, :, None, :]).astype(c_sel.dtype)
        o_lat = jnp.einsum('bqhk,bqkc->bqhc', probs, c_sel)
        return jnp.einsum('bqhc,hcd->bqhd', o_lat, w_uv).reshape(B, Qb, H * dh)

    out = lax.map(attend_block, (blocks(q_lat), blocks(q_idx), blocks(w_idx), starts))
    return out.swapaxes(0, 1).reshape(B, T, H * dh)


def hgrn2_branch(q, f_logit, v, g, lb, norm_gain):
    B, T, _ = q.shape
    H, dk, dv = HGRN_HEADS, HGRN_EXPAND, HGRN_DV
    z = f_logit.astype(F32)
    lb = jnp.clip(lb.astype(F32), 0.0, 1.0)
    f = lb + (1.0 - lb) * jax.nn.sigmoid(z)
    log_f = jnp.log(jnp.maximum(f, F_FLOOR))
    k = (1.0 - lb) * jax.nn.sigmoid(-z)
    C = min(HGRN_CHUNK, T)
    n = T // C

    def chunks(a, d):
        return a.astype(F32).reshape(B, n, C, H, d).transpose(1, 0, 3, 2, 4)

    causal = jnp.tril(jnp.ones((C, C), dtype=bool))[None, None, :, :, None]

    def step(S, inp):
        qi, ki, vi, fi = inp
        b = jnp.cumsum(fi, axis=2)
        diff = b[:, :, :, None, :] - b[:, :, None, :, :]
        decay = jnp.where(causal, jnp.exp(jnp.where(causal, diff, 0.0)), 0.0)
        attn = jnp.einsum('bhtc,bhsc,bhtsc->bhts', qi, ki, decay)
        o = jnp.einsum('bhts,bhse->bhte', attn, vi) + jnp.einsum('bhtc,bhce->bhte', qi * jnp.exp(b), S)
        b_last = b[:, :, -1:, :]
        S = jnp.exp(b_last[:, :, 0, :])[..., None] * S + jnp.einsum('bhsc,bhse->bhce', ki * jnp.exp(b_last - b), vi)
        return S, o

    _, o = lax.scan(step, jnp.zeros((B, H, dk, dv), F32),
                    (chunks(q, dk), chunks(k, dk), chunks(v, dv), chunks(log_f, dk)))
    o = o.transpose(1, 0, 3, 2, 4).reshape(B, T, H * dv)
    o = rms_norm(o, norm_gain)
    return (jax.nn.silu(g.astype(F32)) * o).astype(g.dtype)


def causal_depthwise_conv(h, w, b):
    C = h.shape[-1]
    y = lax.conv_general_dilated(h, w[:, None, :], window_strides=(1,), padding=[(CONV_WIDTH - 1, 0)],
                                 dimension_numbers=('NWC', 'WIO', 'NWC'), feature_group_count=C)
    return y + b


def conv_ffn(x, w_up, conv_w, conv_b, w_down):
    h = causal_depthwise_conv(x @ w_up, conv_w, conv_b)
    a, u = jnp.split(h, 2, axis=-1)
    return (jax.nn.gelu(a) * u) @ w_down


def setup_inputs(seed: int = 0) -> dict:
    key = jax.random.key(seed)
    ks = jax.random.split(key, 24)
    L = DEPTH

    def nrm(k, shape, fan_in):
        return jax.random.normal(k, shape, F32) * fan_in ** -0.5

    def gain(k, shape):
        return 1.0 + 0.02 * jax.random.normal(k, shape, F32)

    return {
        "x": jax.random.normal(ks[0], (BATCH, SEQ, D_MODEL), F32),
        "rel_bias": 0.5 * jax.random.normal(ks[1], (REL_BUCKETS, DSA_HEADS), F32),
        "hgrn_lb": jax.random.normal(ks[2], (L, HGRN_KW), F32),
        "ln_mix_pre": gain(ks[3], (L, D_MODEL)),
        "ln_mix_post": gain(ks[4], (L, D_MODEL)),
        "ln_ffn_pre": gain(ks[5], (L, D_MODEL)),
        "ln_ffn_post": gain(ks[6], (L, D_MODEL)),
        "w_in": nrm(ks[7], (L, D_MODEL, D_IN), D_MODEL),
        "dsa_q_norm": gain(ks[8], (L, DSA_Q_RANK)),
        "dsa_kv_norm": gain(ks[9], (L, DSA_KV_RANK)),
        "dsa_w_uq": nrm(ks[10], (L, DSA_Q_RANK, DSA_UQ_OUT), DSA_Q_RANK),
        "dsa_w_uk": nrm(ks[11], (L, DSA_HEADS, DSA_DH, DSA_KV_RANK), DSA_DH),
        "dsa_w_uv": nrm(ks[12], (L, DSA_HEADS, DSA_KV_RANK, DSA_DH), DSA_KV_RANK),
        "hgrn_norm": gain(ks[13], (L, HGRN_W)),
        "w_br_ret": nrm(ks[14], (L, RET_W, D_MODEL), RET_W),
        "w_br_dsa": nrm(ks[15], (L, DSA_W, D_MODEL), DSA_W),
        "w_br_hgrn": nrm(ks[16], (L, HGRN_W, D_MODEL), HGRN_W),
        "w_out": nrm(ks[17], (L, D_MODEL, D_MODEL), D_MODEL),
        "ffn_w_up": nrm(ks[18], (L, D_MODEL, 2 * D_FF), D_MODEL),
        "ffn_conv_w": nrm(ks[19], (L, CONV_WIDTH, 2 * D_FF), CONV_WIDTH),
        "ffn_conv_b": 0.01 * jax.random.normal(ks[20], (L, 2 * D_FF), F32),
        "ffn_w_down": nrm(ks[21], (L, D_FF, D_MODEL), D_FF),
    }


def reference(x, rel_bias, hgrn_lb, ln_mix_pre, ln_mix_post, ln_ffn_pre, ln_ffn_post, w_in,
              dsa_q_norm, dsa_kv_norm, dsa_w_uq, dsa_w_uk, dsa_w_uv, hgrn_norm,
              w_br_ret, w_br_dsa, w_br_hgrn, w_out, ffn_w_up, ffn_conv_w, ffn_conv_b, ffn_w_down):
    B, T, D = x.shape
    lb_soft = jax.nn.softmax(hgrn_lb.astype(F32), axis=0)
    lower_bounds = jnp.cumsum(lb_soft, axis=0) - lb_soft[0:1]
    for l in range(DEPTH):
        h = rms_norm(x, ln_mix_pre[l])
        (rq, rk, rv, rg, cq, ckv, ik, iw, hq, hf, hv, hg, gt) = split_columns(h @ w_in[l])
        y_ret = retention_branch(rq, rk, rv, rg)
        y_dsa = dsa_branch(cq, ckv, ik, iw, dsa_q_norm[l], dsa_kv_norm[l], dsa_w_uq[l], dsa_w_uk[l],
                           dsa_w_uv[l], rel_bias)
        y_hg = hgrn2_branch(hq, hf, hv, hg, lower_bounds[l], hgrn_norm[l])
        gates = jax.nn.sigmoid(gt).reshape(B, T, N_BRANCH, D)
        merged = (gates[:, :, 0] * (y_ret @ w_br_ret[l])
                  + gates[:, :, 1] * (y_dsa @ w_br_dsa[l])
                  + gates[:, :, 2] * (y_hg @ w_br_hgrn[l]))
        x = x + rms_norm(merged @ w_out[l], ln_mix_post[l])
        h = rms_norm(x, ln_ffn_pre[l])
        x = x + rms_norm(conv_ffn(h, ffn_w_up[l], ffn_conv_w[l], ffn_conv_b[l], ffn_w_down[l]), ln_ffn_post[l])
    return x
```

```python
import functools
import math

import jax
import jax.numpy as jnp
import numpy as np
from jax import lax
from jax.experimental import pallas as pl
from jax.experimental.pallas import tpu as pltpu

F32 = jnp.float32
BF16 = jnp.bfloat16
I32 = jnp.int32

D_MODEL = 1024
RET_HEADS, RET_DK, RET_DV, RET_CHUNK = 4, 64, 128, 128
DSA_HEADS, DSA_DH, DSA_Q_RANK, DSA_KV_RANK = 8, 64, 256, 128
IDX_HEADS, IDX_DIM, DSA_TOPK_MAX, Q_BLOCK = 8, 64, 256, 128
HGRN_HEADS, HGRN_EXPAND, HGRN_DV, HGRN_CHUNK = 4, 128, 128, 32
F_FLOOR = 1e-6
REL_BUCKETS, REL_MAX_DIST = 32, 128
D_FF = 2816
CONV_WIDTH = 3
EPS = 1e-6
NEG_BIG = -1e30

RET_W = RET_HEADS * RET_DV
DSA_W = DSA_HEADS * DSA_DH
HGRN_KW = HGRN_HEADS * HGRN_EXPAND
HGRN_W = HGRN_HEADS * HGRN_DV

Z_RQ, Z_RK, Z_RV, Z_RG = 0, 256, 512, 1024
Z_CQ, Z_CKV = 1536, 1792
Z_HQ, Z_HF, Z_HV, Z_HG = 2048, 2560, 3072, 3584
Z_GT = 4096
Z_W = 7168

VMEM_LIMIT_BYTES = 56 * 1024 * 1024
SUBLANES = 8
LANES = 128

INT_MIN = -(2 ** 31)


def _params(*sem):
    return pltpu.CompilerParams(dimension_semantics=sem, vmem_limit_bytes=VMEM_LIMIT_BYTES)


def _dot(a, b):
    return jnp.dot(a, b, preferred_element_type=F32)


def _dot_nt(a, b):
    return lax.dot_general(a, b, (((1,), (1,)), ((), ())), preferred_element_type=F32)


def _dot_tn(a, b):
    return lax.dot_general(a, b, (((0,), (0,)), ((), ())), preferred_element_type=F32)


def _sigmoid(x):
    return 1.0 / (1.0 + jnp.exp(-x))


def _rms(x, g):
    return x * lax.rsqrt(jnp.mean(x * x, axis=-1, keepdims=True) + EPS) * g


def _inproj_kernel(x_ref, g_ref, w_ref, o_ref, h_ref):
    @pl.when(pl.program_id(1) == 0)
    def _():
        h_ref[...] = _rms(x_ref[...], g_ref[...]).astype(BF16)

    o_ref[...] = _dot(h_ref[...], w_ref[...])


def _inproj(x2, g, w):
    M = x2.shape[0]
    tm = min(1024, M)
    tn = 1024
    return pl.pallas_call(
        _inproj_kernel,
        grid=(M // tm, Z_W // tn),
        in_specs=[pl.BlockSpec((tm, D_MODEL), lambda i, j: (i, 0)),
                  pl.BlockSpec((1, D_MODEL), lambda i, j: (0, 0)),
                  pl.BlockSpec((D_MODEL, tn), lambda i, j: (0, j))],
        out_specs=pl.BlockSpec((tm, tn), lambda i, j: (i, j)),
        out_shape=jax.ShapeDtypeStruct((M, Z_W), F32),
        scratch_shapes=[pltpu.VMEM((tm, D_MODEL), BF16)],
        compiler_params=_params("parallel", "arbitrary"),
        name="inproj",
    )(x2, g, w)


def _ret_kernel(gam_ref, q_ref, k_ref, v_ref, g_ref, cos_ref, sin_ref, dm_ref, xi_ref, zeta_ref,
                o_ref, r_ref):
    C = q_ref.shape[0]
    H, dk, dv = RET_HEADS, RET_DK, RET_DV

    @pl.when(pl.program_id(1) == 0)
    def _():
        r_ref[...] = jnp.zeros_like(r_ref)

    cos = cos_ref[...]
    sin = sin_ref[...]
    lane = lax.broadcasted_iota(I32, (C, H * dk), 1)
    first_half = (lane % dk) < (dk // 2)

    def rot(x):
        swapped = jnp.where(first_half, pltpu.roll(x, H * dk - dk // 2, 1), pltpu.roll(x, dk // 2, 1))
        return x * cos + swapped * sin

    q = rot(q_ref[...])
    k = rot(k_ref[...]) * dk ** -0.5
    for h in range(H):
        qh = q[:, h * dk:(h + 1) * dk].astype(BF16)
        kh = k[:, h * dk:(h + 1) * dk]
        vh = v_ref[:, h * dv:(h + 1) * dv].astype(BF16)
        s = _dot_nt(qh, kh.astype(BF16)) * dm_ref[h]
        inner = _dot(s.astype(BF16), vh)
        rh = r_ref[h]
        cross = _dot(qh, rh.astype(BF16)) * xi_ref[h]
        r_ref[h] = gam_ref[h] * rh + _dot_tn((kh * zeta_ref[h]).astype(BF16), vh)
        o = inner + cross
        mu = jnp.mean(o, axis=-1, keepdims=True)
        oc = o - mu
        var = jnp.mean(oc * oc, axis=-1, keepdims=True)
        gh = g_ref[:, h * dv:(h + 1) * dv]
        o_ref[:, h * dv:(h + 1) * dv] = (gh * _sigmoid(gh) * (oc * lax.rsqrt(var + EPS))).astype(BF16)


def _retention(z, B, T):
    H, dk, dv = RET_HEADS, RET_DK, RET_DV
    C = min(RET_CHUNK, T)
    n = T // C
    pos = jnp.arange(T, dtype=F32)
    half = dk // 2
    freq = 1.0 / (10000.0 ** jnp.linspace(0.0, 1.0, half, dtype=F32))
    ang = pos[:, None] * freq[None, :]
    cos = jnp.tile(jnp.cos(ang), (1, 2 * H))
    sin = jnp.tile(jnp.concatenate([-jnp.sin(ang), jnp.sin(ang)], axis=1), (1, H))
    log_gamma = jnp.log1p(-(2.0 ** (-5.0 - jnp.arange(H, dtype=F32))))
    i = jnp.arange(C, dtype=F32)
    rel = i[:, None] - i[None, :]
    dmask = jnp.where(rel >= 0, jnp.exp(jnp.maximum(rel, 0.0)[None] * log_gamma[:, None, None]), 0.0)
    xi = jnp.exp((i + 1.0)[None, :] * log_gamma[:, None])[:, :, None]
    zeta = jnp.exp((C - 1.0 - i)[None, :] * log_gamma[:, None])[:, :, None]
    gamma_c = jnp.exp(C * log_gamma)

    wq = H * dk
    wv = H * dv
    full = lambda shape: pl.BlockSpec(shape, lambda b, c: (0,) * len(shape))
    return pl.pallas_call(
        _ret_kernel,
        grid=(B, n),
        in_specs=[pl.BlockSpec(memory_space=pltpu.SMEM),
                  pl.BlockSpec((C, wq), lambda b, c: (b * n + c, Z_RQ // wq)),
                  pl.BlockSpec((C, wq), lambda b, c: (b * n + c, Z_RK // wq)),
                  pl.BlockSpec((C, wv), lambda b, c: (b * n + c, Z_RV // wv)),
                  pl.BlockSpec((C, wv), lambda b, c: (b * n + c, Z_RG // wv)),
                  pl.BlockSpec((C, wq), lambda b, c: (c, 0)),
                  pl.BlockSpec((C, wq), lambda b, c: (c, 0)),
                  full((H, C, C)), full((H, C, 1)), full((H, C, 1))],
        out_specs=pl.BlockSpec((C, wv), lambda b, c: (b * n + c, 0)),
        out_shape=jax.ShapeDtypeStruct((B * T, wv), BF16),
        scratch_shapes=[pltpu.VMEM((H, dk, dv), F32)],
        compiler_params=_params("parallel", "arbitrary"),
        name="retention",
    )(gamma_c, z, z, z, z, cos, sin, dmask, xi, zeta)


def _hgrn_kernel(layer, q_ref, f_ref, v_ref, g_ref, lbraw_ref, gain_ref, tri_ref, o_ref, st_ref, oi_ref):
    TR = q_ref.shape[0]
    H, dk, dv, C = HGRN_HEADS, HGRN_EXPAND, HGRN_DV, HGRN_CHUNK
    C = min(C, TR)
    SB = SUBLANES

    @pl.when(pl.program_id(1) == 0)
    def _():
        st_ref[...] = jnp.zeros_like(st_ref)

    raw = lbraw_ref[...]
    e = jnp.exp(raw - jnp.max(raw, axis=0, keepdims=True))
    soft = e / jnp.sum(e, axis=0, keepdims=True)
    cs = soft[0:1]
    for l in range(1, layer + 1):
        cs = cs + soft[l:l + 1]
    lb = jnp.clip(cs - soft[0:1], 0.0, 1.0)
    tri = tri_ref[...]
    row_in_blk = lax.broadcasted_iota(I32, (SB, dk), 0)

    def chunk(c, carry):
        r0 = pl.multiple_of(c * C, C)
        zf = f_ref[pl.ds(r0, C), :]
        f = lb + (1.0 - lb) * _sigmoid(zf)
        log_f = jnp.log(jnp.maximum(f, F_FLOOR))
        kk = (1.0 - lb) * _sigmoid(-zf)
        b = jnp.dot(tri, log_f, preferred_element_type=F32, precision=lax.Precision.HIGHEST)
        qq = q_ref[pl.ds(r0, C), :]
        vv = v_ref[pl.ds(r0, C), :]
        eb = jnp.exp(b)
        b_last = b[C - 1:C, :]
        eb_last = eb[C - 1:C, :]
        q_dec = (qq * eb).astype(BF16)
        k_dec = (kk * jnp.exp(b_last - b)).astype(BF16)
        vb = vv.astype(BF16)
        for h in range(H):
            sl = slice(h * dk, (h + 1) * dk)
            bh, qh, kh, vh = b[:, sl], qq[:, sl], kk[:, sl], vv[:, h * dv:(h + 1) * dv]
            for t in range(C):
                ti = t // SB
                bt = bh[t:t + 1, :]
                qt = qh[t:t + 1, :]
                acc = None
                for j in range(ti + 1):
                    diff = bt - bh[j * SB:(j + 1) * SB, :]
                    if j == ti:
                        diff = jnp.where(row_in_blk <= (t - ti * SB), diff, NEG_BIG)
                    p = (qt * jnp.exp(diff)) * kh[j * SB:(j + 1) * SB, :]
                    a = jnp.sum(p, axis=-1, keepdims=True)
                    term = a * vh[j * SB:(j + 1) * SB, :]
                    acc = term if acc is None else acc + term
                oi_ref[t:t + 1, h * dv:(h + 1) * dv] = jnp.sum(acc, axis=0, keepdims=True)
            st = st_ref[h]
            cross = _dot_nt(q_dec[:, sl], st.astype(BF16))
            oi_ref[:, h * dv:(h + 1) * dv] = oi_ref[:, h * dv:(h + 1) * dv] + cross
            st_ref[h] = st * eb_last[:, sl] + _dot_tn(vb[:, h * dv:(h + 1) * dv], k_dec[:, sl])
        o = _rms(oi_ref[...], gain_ref[...])
        gg = g_ref[pl.ds(r0, C), :]
        o_ref[pl.ds(r0, C), :] = (gg * _sigmoid(gg) * o).astype(BF16)
        return carry

    lax.fori_loop(0, TR // C, chunk, 0)


def _hgrn(z, hgrn_lb, gain, layer, B, T):
    H, dk, dv = HGRN_HEADS, HGRN_EXPAND, HGRN_DV
    C = min(HGRN_CHUNK, T)
    TR = min(256, T)
    n = T // TR
    w = H * dk
    L = hgrn_lb.shape[0]
    tri = jnp.tril(jnp.ones((C, C), F32))
    return pl.pallas_call(
        functools.partial(_hgrn_kernel, layer),
        grid=(B, n),
        in_specs=[pl.BlockSpec((TR, w), lambda b, c: (b * n + c, Z_HQ // w)),
                  pl.BlockSpec((TR, w), lambda b, c: (b * n + c, Z_HF // w)),
                  pl.BlockSpec((TR, w), lambda b, c: (b * n + c, Z_HV // w)),
                  pl.BlockSpec((TR, w), lambda b, c: (b * n + c, Z_HG // w)),
                  pl.BlockSpec((L, w), lambda b, c: (0, 0)),
                  pl.BlockSpec((1, w), lambda b, c: (0, 0)),
                  pl.BlockSpec((C, C), lambda b, c: (0, 0))],
        out_specs=pl.BlockSpec((TR, w), lambda b, c: (b * n + c, 0)),
        out_shape=jax.ShapeDtypeStruct((B * T, w), BF16),
        scratch_shapes=[pltpu.VMEM((H, dv, dk), F32), pltpu.VMEM((C, H * dv), F32)],
        compiler_params=_params("parallel", "arbitrary"),
        name="hgrn2",
    )(z, z, z, z, hgrn_lb, gain, tri)


def _dsa_prep_kernel(cq_ref, kv_ref, qn_ref, kn_ref, wq_ref, wqi_ref, wuk_ref,
                     qlat_ref, qidx_ref, w_ref, ckv_ref, ik_ref):
    tm = cq_ref.shape[0]
    H, dh, HI, dI, Dc = DSA_HEADS, DSA_DH, IDX_HEADS, IDX_DIM, DSA_KV_RANK
    QB = qlat_ref.shape[1] // H
    nb = tm // QB
    cq = _rms(cq_ref[...], qn_ref[...]).astype(BF16)
    kv = kv_ref[...]
    ckv_ref[...] = _rms(kv[:, :Dc], kn_ref[...]).astype(BF16)
    ik_ref[...] = kv[:, Dc:Dc + dI].astype(BF16)
    wi = kv[:, Dc + dI:Dc + dI + HI] * (HI * dI) ** -0.5
    for h in range(H):
        qh = _dot(cq, wq_ref[h])
        ql = (_dot(qh.astype(BF16), wuk_ref[h]) * dh ** -0.5).astype(BF16)
        qi = _dot(cq, wqi_ref[h]).astype(BF16)
        for j in range(nb):
            qlat_ref[j, h * QB:(h + 1) * QB, :] = ql[j * QB:(j + 1) * QB]
            qidx_ref[j, h * QB:(h + 1) * QB, :] = qi[j * QB:(j + 1) * QB]
            w_ref[j, h * QB:(h + 1) * QB, :] = wi[j * QB:(j + 1) * QB, h:h + 1]


def _dsa_prep(z, qn, kn, wq, wqi, wuk, B, T):
    M = B * T
    H, dh, HI, dI, Dc, Rq = DSA_HEADS, DSA_DH, IDX_HEADS, IDX_DIM, DSA_KV_RANK, DSA_Q_RANK
    QB = min(Q_BLOCK, T)
    tm = min(256, T)
    nb = tm // QB
    c3 = lambda shape: pl.BlockSpec(shape, lambda i: (0, 0, 0))
    return pl.pallas_call(
        _dsa_prep_kernel,
        grid=(M // tm,),
        in_specs=[pl.BlockSpec((tm, Rq), lambda i: (i, Z_CQ // Rq)),
                  pl.BlockSpec((tm, 256), lambda i: (i, Z_CKV // 256)),
                  pl.BlockSpec((1, Rq), lambda i: (0, 0)),
                  pl.BlockSpec((1, Dc), lambda i: (0, 0)),
                  c3((H, Rq, dh)), c3((HI, Rq, dI)), c3((H, dh, Dc))],
        out_specs=[pl.BlockSpec((nb, H * QB, Dc), lambda i: (i, 0, 0)),
                   pl.BlockSpec((nb, HI * QB, dI), lambda i: (i, 0, 0)),
                   pl.BlockSpec((nb, HI * QB, 1), lambda i: (i, 0, 0)),
                   pl.BlockSpec((tm, Dc), lambda i: (i, 0)),
                   pl.BlockSpec((tm, dI), lambda i: (i, 0))],
        out_shape=[jax.ShapeDtypeStruct((M // QB, H * QB, Dc), BF16),
                   jax.ShapeDtypeStruct((M // QB, HI * QB, dI), BF16),
                   jax.ShapeDtypeStruct((M // QB, HI * QB, 1), F32),
                   jax.ShapeDtypeStruct((M, Dc), BF16),
                   jax.ShapeDtypeStruct((M, dI), BF16)],
        compiler_params=_params("parallel"),
        name="dsa_prep",
    )(z, z, qn, kn, wq, wqi, wuk)


def _sortable_key(s):
    s = jnp.where(s == 0.0, 0.0, s)
    bits = pltpu.bitcast(s, I32)
    return bits ^ ((bits >> 31) & 0x7FFFFFFF)


def _dsa_attn_kernel(topk, n_keys, neg_key, qlat_ref, qidx_ref, w_ref, ckv_ref, ik_ref, bias_ref, wuv_ref,
                     o_ref, key_ref, m_ref, l_ref, acc_ref, tie_ref):
    H, HI, Dc = DSA_HEADS, IDX_HEADS, DSA_KV_RANK
    QB = o_ref.shape[0]
    i = pl.program_id(1)
    nkb = i + 1
    n_skip = n_keys - nkb * QB

    row = lax.broadcasted_iota(I32, (QB, QB), 0)
    col = lax.broadcasted_iota(I32, (QB, QB), 1)
    t_glob = i * QB + row

    qidx = qidx_ref[0]
    wb = jnp.broadcast_to(w_ref[0], (HI * QB, QB))

    def score_blk(kb, carry):
        k0 = pl.multiple_of(kb * QB, QB)
        p = _dot_nt(qidx, ik_ref[pl.ds(k0, QB), :])
        p = jnp.maximum(p, 0.0) * wb
        s = p[0:QB]
        for h in range(1, HI):
            s = s + p[h * QB:(h + 1) * QB]
        s = jnp.where(kb * QB + col <= t_glob, s, NEG_BIG)
        key_ref[kb] = _sortable_key(s)
        return carry

    lax.fori_loop(0, nkb, score_blk, 0)

    def count_ge(cand):
        cand_b = jnp.broadcast_to(cand, (QB, QB))

        def body(kb, acc):
            return acc + jnp.where(key_ref[kb] >= cand_b, 1, 0)

        acc = lax.fori_loop(0, nkb, body, jnp.zeros((QB, QB), I32))
        cnt = jnp.sum(acc, axis=1, keepdims=True)
        return cnt + jnp.where(cand <= neg_key, n_skip, 0)

    zero = jnp.zeros((QB, 1), I32)
    thr = jnp.where(count_ge(zero) >= topk, zero, jnp.full((QB, 1), INT_MIN, I32))

    def bisect(it, thr):
        cand = thr | jnp.left_shift(jnp.int32(1), 30 - it)
        return jnp.where(count_ge(cand) >= topk, cand, thr)

    thr = lax.fori_loop(0, 31, bisect, thr)
    thr_b = jnp.broadcast_to(thr, (QB, QB))

    def count_gt_eq(kb, carry):
        g, e = carry
        kk = key_ref[kb]
        return g + jnp.where(kk > thr_b, 1, 0), e + jnp.where(kk == thr_b, 1, 0)

    g, e = lax.fori_loop(0, nkb, count_gt_eq, (jnp.zeros((QB, QB), I32), jnp.zeros((QB, QB), I32)))
    n_gt = jnp.sum(g, axis=1, keepdims=True) + jnp.where(thr < neg_key, n_skip, 0)
    n_eq = jnp.sum(e, axis=1, keepdims=True)
    need = topk - n_gt
    nbits = max(1, (n_keys - 1).bit_length())
    tie_ref[...] = jnp.full((QB, QB), (1 << nbits) - 1, I32)

    @pl.when(jnp.max(jnp.where(need < n_eq, 1, 0)) > 0)
    def _():
        def count_eq_below(pcand):
            p_b = jnp.broadcast_to(pcand, (QB, QB))

            def body(kb, acc):
                hit = (key_ref[kb] == thr_b) & (kb * QB + col < p_b)
                return acc + jnp.where(hit, 1, 0)

            acc = lax.fori_loop(0, nkb, body, jnp.zeros((QB, QB), I32))
            return jnp.sum(acc, axis=1, keepdims=True)

        def ibisect(it, p):
            cand = p | jnp.left_shift(jnp.int32(1), nbits - 1 - it)
            return jnp.where(count_eq_below(cand) < need, cand, p)

        p = lax.fori_loop(0, nbits, ibisect, jnp.zeros((QB, 1), I32))
        tie_ref[...] = jnp.broadcast_to(p, (QB, QB))

    tie_b = tie_ref[...]

    m_ref[...] = jnp.full(m_ref.shape, NEG_BIG, F32)
    l_ref[...] = jnp.zeros(l_ref.shape, F32)
    acc_ref[...] = jnp.zeros(acc_ref.shape, F32)
    qlat = qlat_ref[0]

    def attn_blk(kb, carry):
        k0 = pl.multiple_of(kb * QB, QB)
        ckv = ckv_ref[pl.ds(k0, QB), :]
        kk = key_ref[kb]
        s_glob = kb * QB + col
        sel = ((kk > thr_b) | ((kk == thr_b) & (s_glob <= tie_b))) & (s_glob <= t_glob)
        bias = bias_ref[jnp.minimum(i - kb, 2)]
        z = _dot_nt(qlat, ckv) + bias
        z3 = jnp.where(sel[None], z.reshape(H, QB, QB), NEG_BIG)
        m_old = m_ref[...].reshape(H, QB, 1)
        m_new = jnp.maximum(m_old, jnp.max(z3, axis=-1, keepdims=True))
        alpha = jnp.exp(m_old - m_new)
        p = jnp.where(sel[None], jnp.exp(z3 - m_new), 0.0)
        l_ref[...] = (alpha * l_ref[...].reshape(H, QB, 1) + jnp.sum(p, axis=-1, keepdims=True)).reshape(H * QB, 1)
        m_ref[...] = m_new.reshape(H * QB, 1)
        pv = _dot(p.reshape(H * QB, QB).astype(BF16), ckv)
        acc_ref[...] = alpha.reshape(H * QB, 1) * acc_ref[...] + pv
        return carry

    lax.fori_loop(0, nkb, attn_blk, 0)

    o_lat = (acc_ref[...] / l_ref[...]).astype(BF16)
    out = _dot(o_lat[0:QB], wuv_ref[0])
    for h in range(1, H):
        out = out + _dot(o_lat[h * QB:(h + 1) * QB], wuv_ref[h])
    o_ref[...] = out.astype(BF16)


def _dsa_attn(qlat, qidx, w_hm, ckvn, idxk, bias_tiles, wuv_pad, B, T):
    H, HI, dI, Dc = DSA_HEADS, IDX_HEADS, IDX_DIM, DSA_KV_RANK
    QB = min(Q_BLOCK, T)
    nq = T // QB
    topk = min(DSA_TOPK_MAX, T // 4)
    neg_key = int(np.array(NEG_BIG, np.float32).view(np.int32))
    neg_key = neg_key ^ ((neg_key >> 31) & 0x7FFFFFFF)
    return pl.pallas_call(
        functools.partial(_dsa_attn_kernel, topk, T, neg_key),
        grid=(B, nq),
        in_specs=[pl.BlockSpec((1, H * QB, Dc), lambda b, i: (b * nq + i, 0, 0)),
                  pl.BlockSpec((1, HI * QB, dI), lambda b, i: (b * nq + i, 0, 0)),
                  pl.BlockSpec((1, HI * QB, 1), lambda b, i: (b * nq + i, 0, 0)),
                  pl.BlockSpec((T, Dc), lambda b, i: (b, 0)),
                  pl.BlockSpec((T, dI), lambda b, i: (b, 0)),
                  pl.BlockSpec((3, H * QB, QB), lambda b, i: (0, 0, 0)),
                  pl.BlockSpec((H, Dc, DSA_W), lambda b, i: (0, 0, 0))],
        out_specs=pl.BlockSpec((QB, DSA_W), lambda b, i: (b * nq + i, 0)),
        out_shape=jax.ShapeDtypeStruct((B * T, DSA_W), BF16),
        scratch_shapes=[pltpu.VMEM((nq, QB, QB), I32),
                        pltpu.VMEM((H * QB, 1), F32), pltpu.VMEM((H * QB, 1), F32),
                        pltpu.VMEM((H * QB, Dc), F32), pltpu.VMEM((QB, QB), I32)],
        compiler_params=_params("parallel", "arbitrary"),
        name="dsa_attn",
    )(qlat, qidx, w_hm, ckvn, idxk, bias_tiles, wuv_pad)


def _rel_bias_tiles(rel_bias, QB):
    max_exact = REL_BUCKETS // 2
    n = jnp.arange(2 * QB, dtype=jnp.int32)
    nf = jnp.maximum(n, max_exact).astype(F32)
    large = max_exact + (jnp.log(nf / max_exact) / math.log(REL_MAX_DIST / max_exact)
                         * (REL_BUCKETS - max_exact)).astype(jnp.int32)
    large = jnp.minimum(large, REL_BUCKETS - 1)
    bucket = jnp.where(n < max_exact, n, large)
    tab = jnp.take(rel_bias, bucket, axis=0)
    tq = jnp.arange(QB)[:, None]
    sk = jnp.arange(QB)[None, :]
    d0 = jnp.maximum(tq - sk, 0)
    d1 = QB + tq - sk
    tile0 = jnp.moveaxis(tab[d0], -1, 0)
    tile1 = jnp.moveaxis(tab[d1], -1, 0)
    tile2 = jnp.broadcast_to(rel_bias[REL_BUCKETS - 1][:, None, None], tile0.shape)
    H = rel_bias.shape[1]
    return jnp.stack([tile0, tile1, tile2]).reshape(3, H * QB, QB).astype(F32)


def _merge_kernel(x_ref, yr_ref, yd_ref, yh_ref, g0_ref, g1_ref, g2_ref, wr_ref, wd_ref, wh_ref, wo_ref,
                  ln_ref, o_ref):
    m = _sigmoid(g0_ref[...]) * _dot(yr_ref[...], wr_ref[...])
    m = m + _sigmoid(g1_ref[...]) * _dot(yd_ref[...], wd_ref[...])
    m = m + _sigmoid(g2_ref[...]) * _dot(yh_ref[...], wh_ref[...])
    u = _dot(m.astype(BF16), wo_ref[...])
    o_ref[...] = x_ref[...] + _rms(u, ln_ref[...])


def _merge(x2, y_ret, y_dsa, y_hg, z, wr, wd, wh, wo, ln):
    M = x2.shape[0]
    tm = min(512, M)
    D = D_MODEL
    row = lambda w: pl.BlockSpec((tm, w), lambda i: (i, 0))
    const = lambda r, c: pl.BlockSpec((r, c), lambda i: (0, 0))
    gate = lambda k: pl.BlockSpec((tm, D), lambda i: (i, Z_GT // D + k))
    return pl.pallas_call(
        _merge_kernel,
        grid=(M // tm,),
        in_specs=[row(D), row(RET_W), row(DSA_W), row(HGRN_W), gate(0), gate(1), gate(2),
                  const(RET_W, D), const(DSA_W, D), const(HGRN_W, D), const(D, D), const(1, D)],
        out_specs=row(D),
        out_shape=jax.ShapeDtypeStruct((M, D), F32),
        compiler_params=_params("parallel"),
        name="merge",
    )(x2, y_ret, y_dsa, y_hg, z, z, z, wr, wd, wh, wo, ln)


def _gelu_tanh(x):
    return 0.5 * x * (1.0 + jnp.tanh(math.sqrt(2.0 / math.pi) * (x + 0.044715 * (x * x * x))))


def _ffn_kernel(tiles_per_seq, x_ref, lnpre_ref, wup_ref, cw_ref, cb_ref, wdn_ref, lnpost_ref,
                o_ref, buf_ref, prev_ref):
    tm = x_ref.shape[0]
    HALO = SUBLANES
    fc = buf_ref.shape[1]
    n_pass = D_FF // fc
    first = (pl.program_id(0) % tiles_per_seq) == 0

    @pl.when(first)
    def _():
        prev_ref[...] = jnp.zeros_like(prev_ref)

    x = x_ref[...]
    h = _rms(x, lnpre_ref[...]).astype(BF16)

    def conv(part, c):
        col = part * D_FF + c * fc
        up = _dot(h, wup_ref[:, col:col + fc])
        slot = part * n_pass + c
        buf_ref[0:HALO, :] = prev_ref[slot]
        buf_ref[HALO:HALO + tm, :] = up
        prev_ref[slot] = up[tm - HALO:tm, :]
        w = cw_ref[:, col:col + fc]
        y = (up * w[2:3] + buf_ref[HALO - 1:HALO - 1 + tm, :] * w[1:2]
             + buf_ref[HALO - 2:HALO - 2 + tm, :] * w[0:1])
        return y + cb_ref[:, col:col + fc]

    acc = None
    for c in range(n_pass):
        a = conv(0, c)
        u = conv(1, c)
        act = (_gelu_tanh(a) * u).astype(BF16)
        d = _dot(act, wdn_ref[c * fc:(c + 1) * fc, :])
        acc = d if acc is None else acc + d
    o_ref[...] = x + _rms(acc, lnpost_ref[...])


def _ffn(x2, lnpre, wup, cw, cb, wdn, lnpost, T):
    M = x2.shape[0]
    D = D_MODEL
    tm = min(512, T)
    fc = D_FF // 2
    const = lambda r, c: pl.BlockSpec((r, c), lambda i: (0, 0))
    return pl.pallas_call(
        functools.partial(_ffn_kernel, T // tm),
        grid=(M // tm,),
        in_specs=[pl.BlockSpec((tm, D), lambda i: (i, 0)), const(1, D), const(D, 2 * D_FF),
                  const(CONV_WIDTH, 2 * D_FF), const(1, 2 * D_FF), const(D_FF, D), const(1, D)],
        out_specs=pl.BlockSpec((tm, D), lambda i: (i, 0)),
        out_shape=jax.ShapeDtypeStruct((M, D), F32),
        scratch_shapes=[pltpu.VMEM((tm + SUBLANES, fc), F32),
                        pltpu.VMEM((2 * (D_FF // fc), SUBLANES, fc), F32)],
        compiler_params=_params("arbitrary"),
        name="conv_ffn",
    )(x2, lnpre, wup, cw, cb, wdn, lnpost)


def _permute_w_in(w):
    widths = (256, 256, 512, 512, 256, 128, 64, 8, 512, 512, 512, 512, 3 * D_MODEL)
    offs = np.concatenate([[0], np.cumsum(widths)])
    seg = [w[:, offs[k]:offs[k + 1]] for k in range(len(widths))]
    pad = jnp.zeros((w.shape[0], 256 - (128 + 64 + 8)), w.dtype)
    cols = seg[0:6] + [seg[6], seg[7], pad] + seg[8:]
    return jnp.concatenate(cols, axis=1).astype(BF16)


def kernel(x, rel_bias, hgrn_lb, ln_mix_pre, ln_mix_post, ln_ffn_pre, ln_ffn_post, w_in, dsa_q_norm, dsa_kv_norm, dsa_w_uq, dsa_w_uk, dsa_w_uv, hgrn_norm, w_br_ret, w_br_dsa, w_br_hgrn, w_out, ffn_w_up, ffn_conv_w, ffn_conv_b, ffn_w_down):
    B, T, D = x.shape
    depth = w_in.shape[0]
    H, dh, HI, dI, Dc, Rq = DSA_HEADS, DSA_DH, IDX_HEADS, IDX_DIM, DSA_KV_RANK, DSA_Q_RANK
    QB = min(Q_BLOCK, T)
    bias_tiles = _rel_bias_tiles(rel_bias, QB)
    x2 = x.reshape(B * T, D)
    row = lambda v: v.reshape(1, -1)
    for l in range(depth):
        z = _inproj(x2, row(ln_mix_pre[l]), _permute_w_in(w_in[l]))
        y_ret = _retention(z, B, T)
        y_hg = _hgrn(z, hgrn_lb, row(hgrn_norm[l]), l, B, T)
        wq = dsa_w_uq[l][:, :H * dh].reshape(Rq, H, dh).transpose(1, 0, 2).astype(BF16)
        wqi = dsa_w_uq[l][:, H * dh:].reshape(Rq, HI, dI).transpose(1, 0, 2).astype(BF16)
        qlat, qidx, w_hm, ckvn, idxk = _dsa_prep(z, row(dsa_q_norm[l]), row(dsa_kv_norm[l]), wq, wqi,
                                                 dsa_w_uk[l].astype(BF16), B, T)
        eye = jnp.eye(H, dtype=F32)
        wuv_pad = (dsa_w_uv[l][:, :, None, :] * eye[:, None, :, None]).reshape(H, Dc, H * dh).astype(BF16)
        y_dsa = _dsa_attn(qlat, qidx, w_hm, ckvn, idxk, bias_tiles, wuv_pad, B, T)
        x2 = _merge(x2, y_ret, y_dsa, y_hg, z, w_br_ret[l].astype(BF16), w_br_dsa[l].astype(BF16),
                    w_br_hgrn[l].astype(BF16), w_out[l].astype(BF16), row(ln_mix_post[l]))
        x2 = _ffn(x2, row(ln_ffn_pre[l]), ffn_w_up[l].astype(BF16), ffn_conv_w[l], row(ffn_conv_b[l]),
                  ffn_w_down[l].astype(BF16), row(ln_ffn_post[l]), T)
    return x2.reshape(B, T, D)
```

```python
import functools
import math

import jax
import jax.numpy as jnp
import numpy as np
from jax import lax
from jax.experimental import pallas as pl
from jax.experimental.pallas import tpu as pltpu

F32 = jnp.float32
BF16 = jnp.bfloat16
I32 = jnp.int32

D_MODEL = 1024
RET_HEADS, RET_DK, RET_DV, RET_CHUNK = 4, 64, 128, 128
DSA_HEADS, DSA_DH, DSA_Q_RANK, DSA_KV_RANK = 8, 64, 256, 128
IDX_HEADS, IDX_DIM, DSA_TOPK_MAX, Q_BLOCK = 8, 64, 256, 128
HGRN_HEADS, HGRN_EXPAND, HGRN_DV, HGRN_CHUNK = 4, 128, 128, 32
F_FLOOR = 1e-6
REL_BUCKETS, REL_MAX_DIST = 32, 128
D_FF = 2816
CONV_WIDTH = 3
EPS = 1e-6
NEG_BIG = -1e30

RET_W = RET_HEADS * RET_DV
DSA_W = DSA_HEADS * DSA_DH
HGRN_KW = HGRN_HEADS * HGRN_EXPAND
HGRN_W = HGRN_HEADS * HGRN_DV

Z_RQ, Z_RK, Z_RV, Z_RG = 0, 256, 512, 1024
Z_CQ, Z_CKV = 1536, 1792
Z_HQ, Z_HF, Z_HV, Z_HG = 2048, 2560, 3072, 3584
Z_GT = 4096
Z_W = 7168

VMEM_LIMIT_BYTES = 56 * 1024 * 1024
SUBLANES = 8
LANES = 128

KEY_BLOCK = 256
INT_MIN = -(2 ** 31)


def _params(*sem):
    return pltpu.CompilerParams(dimension_semantics=sem, vmem_limit_bytes=VMEM_LIMIT_BYTES)


def _dot(a, b):
    return jnp.dot(a, b, preferred_element_type=F32)


def _dot_nt(a, b):
    return lax.dot_general(a, b, (((1,), (1,)), ((), ())), preferred_element_type=F32)


def _dot_tn(a, b):
    return lax.dot_general(a, b, (((0,), (0,)), ((), ())), preferred_element_type=F32)


def _sigmoid(x):
    return 1.0 / (1.0 + jnp.exp(-x))


def _rms(x, g):
    return x * lax.rsqrt(jnp.mean(x * x, axis=-1, keepdims=True) + EPS) * g


def _group_sum(x):
    return jnp.sum(x.reshape(x.shape[0] // SUBLANES, SUBLANES, x.shape[1]), axis=0)


def _group_max(x):
    return jnp.max(x.reshape(x.shape[0] // SUBLANES, SUBLANES, x.shape[1]), axis=0)


def _inproj_kernel(x_ref, g_ref, w_ref, o_ref, h_ref):
    @pl.when(pl.program_id(1) == 0)
    def _():
        h_ref[...] = _rms(x_ref[...], g_ref[...]).astype(BF16)

    o_ref[...] = _dot(h_ref[...], w_ref[...])


def _inproj(x2, g, w):
    M = x2.shape[0]
    tm = min(1024, M)
    tn = 1024
    return pl.pallas_call(
        _inproj_kernel,
        grid=(M // tm, Z_W // tn),
        in_specs=[pl.BlockSpec((tm, D_MODEL), lambda i, j: (i, 0)),
                  pl.BlockSpec((1, D_MODEL), lambda i, j: (0, 0)),
                  pl.BlockSpec((D_MODEL, tn), lambda i, j: (0, j))],
        out_specs=pl.BlockSpec((tm, tn), lambda i, j: (i, j)),
        out_shape=jax.ShapeDtypeStruct((M, Z_W), F32),
        scratch_shapes=[pltpu.VMEM((tm, D_MODEL), BF16)],
        compiler_params=_params("parallel", "arbitrary"),
        name="inproj",
    )(x2, g, w)


def _ret_kernel(gam_ref, q_ref, k_ref, v_ref, g_ref, cos_ref, sin_ref, dm_ref, xi_ref, zeta_ref,
                o_ref, r_ref):
    C = q_ref.shape[0]
    H, dk, dv = RET_HEADS, RET_DK, RET_DV

    @pl.when(pl.program_id(1) == 0)
    def _():
        r_ref[...] = jnp.zeros_like(r_ref)

    cos = cos_ref[...]
    sin = sin_ref[...]
    lane = lax.broadcasted_iota(I32, (C, H * dk), 1)
    first_half = (lane % dk) < (dk // 2)

    def rot(x):
        swapped = jnp.where(first_half, pltpu.roll(x, H * dk - dk // 2, 1), pltpu.roll(x, dk // 2, 1))
        return x * cos + swapped * sin

    q = rot(q_ref[...])
    k = rot(k_ref[...]) * dk ** -0.5
    for h in range(H):
        qh = q[:, h * dk:(h + 1) * dk].astype(BF16)
        kh = k[:, h * dk:(h + 1) * dk]
        vh = v_ref[:, h * dv:(h + 1) * dv].astype(BF16)
        s = _dot_nt(qh, kh.astype(BF16)) * dm_ref[h]
        inner = _dot(s.astype(BF16), vh)
        rh = r_ref[h]
        cross = _dot(qh, rh.astype(BF16)) * xi_ref[h]
        r_ref[h] = gam_ref[h] * rh + _dot_tn((kh * zeta_ref[h]).astype(BF16), vh)
        o = inner + cross
        mu = jnp.mean(o, axis=-1, keepdims=True)
        oc = o - mu
        var = jnp.mean(oc * oc, axis=-1, keepdims=True)
        gh = g_ref[:, h * dv:(h + 1) * dv]
        o_ref[:, h * dv:(h + 1) * dv] = (gh * _sigmoid(gh) * (oc * lax.rsqrt(var + EPS))).astype(BF16)


def _retention(z, B, T):
    H, dk, dv = RET_HEADS, RET_DK, RET_DV
    C = min(RET_CHUNK, T)
    n = T // C
    pos = jnp.arange(T, dtype=F32)
    half = dk // 2
    freq = 1.0 / (10000.0 ** jnp.linspace(0.0, 1.0, half, dtype=F32))
    ang = pos[:, None] * freq[None, :]
    cos = jnp.tile(jnp.cos(ang), (1, 2 * H))
    sin = jnp.tile(jnp.concatenate([-jnp.sin(ang), jnp.sin(ang)], axis=1), (1, H))
    log_gamma = jnp.log1p(-(2.0 ** (-5.0 - jnp.arange(H, dtype=F32))))
    i = jnp.arange(C, dtype=F32)
    rel = i[:, None] - i[None, :]
    dmask = jnp.where(rel >= 0, jnp.exp(jnp.maximum(rel, 0.0)[None] * log_gamma[:, None, None]), 0.0)
    xi = jnp.exp((i + 1.0)[None, :] * log_gamma[:, None])[:, :, None]
    zeta = jnp.exp((C - 1.0 - i)[None, :] * log_gamma[:, None])[:, :, None]
    gamma_c = jnp.exp(C * log_gamma)

    wq = H * dk
    wv = H * dv
    full = lambda shape: pl.BlockSpec(shape, lambda b, c: (0,) * len(shape))
    return pl.pallas_call(
        _ret_kernel,
        grid=(B, n),
        in_specs=[pl.BlockSpec(memory_space=pltpu.SMEM),
                  pl.BlockSpec((C, wq), lambda b, c: (b * n + c, Z_RQ // wq)),
                  pl.BlockSpec((C, wq), lambda b, c: (b * n + c, Z_RK // wq)),
                  pl.BlockSpec((C, wv), lambda b, c: (b * n + c, Z_RV // wv)),
                  pl.BlockSpec((C, wv), lambda b, c: (b * n + c, Z_RG // wv)),
                  pl.BlockSpec((C, wq), lambda b, c: (c, 0)),
                  pl.BlockSpec((C, wq), lambda b, c: (c, 0)),
                  full((H, C, C)), full((H, C, 1)), full((H, C, 1))],
        out_specs=pl.BlockSpec((C, wv), lambda b, c: (b * n + c, 0)),
        out_shape=jax.ShapeDtypeStruct((B * T, wv), BF16),
        scratch_shapes=[pltpu.VMEM((H, dk, dv), F32)],
        compiler_params=_params("parallel", "arbitrary"),
        name="retention",
    )(gamma_c, z, z, z, z, cos, sin, dmask, xi, zeta)


def _hgrn_kernel(layer, q_ref, f_ref, v_ref, g_ref, lbraw_ref, gain_ref, tri_ref, o_ref, st_ref, oi_ref):
    TR = q_ref.shape[0]
    H, dk, dv, C = HGRN_HEADS, HGRN_EXPAND, HGRN_DV, HGRN_CHUNK
    C = min(C, TR)
    SB = SUBLANES

    @pl.when(pl.program_id(1) == 0)
    def _():
        st_ref[...] = jnp.zeros_like(st_ref)

    raw = lbraw_ref[...]
    e = jnp.exp(raw - jnp.max(raw, axis=0, keepdims=True))
    soft = e / jnp.sum(e, axis=0, keepdims=True)
    cs = soft[0:1]
    for l in range(1, layer + 1):
        cs = cs + soft[l:l + 1]
    lb = jnp.clip(cs - soft[0:1], 0.0, 1.0)
    tri = tri_ref[...]
    row_in_blk = lax.broadcasted_iota(I32, (SB, dk), 0)

    def chunk(c, carry):
        r0 = pl.multiple_of(c * C, C)
        zf = f_ref[pl.ds(r0, C), :]
        f = lb + (1.0 - lb) * _sigmoid(zf)
        log_f = jnp.log(jnp.maximum(f, F_FLOOR))
        kk = (1.0 - lb) * _sigmoid(-zf)
        b = jnp.dot(tri, log_f, preferred_element_type=F32, precision=lax.Precision.HIGHEST)
        qq = q_ref[pl.ds(r0, C), :]
        vv = v_ref[pl.ds(r0, C), :]
        eb = jnp.exp(b)
        b_last = b[C - 1:C, :]
        eb_last = eb[C - 1:C, :]
        q_dec = (qq * eb).astype(BF16)
        k_dec = (kk * jnp.exp(b_last - b)).astype(BF16)
        vb = vv.astype(BF16)
        for h in range(H):
            sl = slice(h * dk, (h + 1) * dk)
            bh, qh, kh, vh = b[:, sl], qq[:, sl], kk[:, sl], vv[:, h * dv:(h + 1) * dv]
            for t in range(C):
                ti = t // SB
                bt = bh[t:t + 1, :]
                qt = qh[t:t + 1, :]
                acc = None
                for j in range(ti + 1):
                    diff = bt - bh[j * SB:(j + 1) * SB, :]
                    if j == ti:
                        diff = jnp.where(row_in_blk <= (t - ti * SB), diff, NEG_BIG)
                    p = (qt * jnp.exp(diff)) * kh[j * SB:(j + 1) * SB, :]
                    a = jnp.sum(p, axis=-1, keepdims=True)
                    term = a * vh[j * SB:(j + 1) * SB, :]
                    acc = term if acc is None else acc + term
                oi_ref[t:t + 1, h * dv:(h + 1) * dv] = jnp.sum(acc, axis=0, keepdims=True)
            st = st_ref[h]
            cross = _dot_nt(q_dec[:, sl], st.astype(BF16))
            oi_ref[:, h * dv:(h + 1) * dv] = oi_ref[:, h * dv:(h + 1) * dv] + cross
            st_ref[h] = st * eb_last[:, sl] + _dot_tn(vb[:, h * dv:(h + 1) * dv], k_dec[:, sl])
        o = _rms(oi_ref[...], gain_ref[...])
        gg = g_ref[pl.ds(r0, C), :]
        o_ref[pl.ds(r0, C), :] = (gg * _sigmoid(gg) * o).astype(BF16)
        return carry

    lax.fori_loop(0, TR // C, chunk, 0)


def _hgrn(z, hgrn_lb, gain, layer, B, T):
    H, dk, dv = HGRN_HEADS, HGRN_EXPAND, HGRN_DV
    C = min(HGRN_CHUNK, T)
    TR = min(256, T)
    n = T // TR
    w = H * dk
    L = hgrn_lb.shape[0]
    tri = jnp.tril(jnp.ones((C, C), F32))
    return pl.pallas_call(
        functools.partial(_hgrn_kernel, layer),
        grid=(B, n),
        in_specs=[pl.BlockSpec((TR, w), lambda b, c: (b * n + c, Z_HQ // w)),
                  pl.BlockSpec((TR, w), lambda b, c: (b * n + c, Z_HF // w)),
                  pl.BlockSpec((TR, w), lambda b, c: (b * n + c, Z_HV // w)),
                  pl.BlockSpec((TR, w), lambda b, c: (b * n + c, Z_HG // w)),
                  pl.BlockSpec((L, w), lambda b, c: (0, 0)),
                  pl.BlockSpec((1, w), lambda b, c: (0, 0)),
                  pl.BlockSpec((C, C), lambda b, c: (0, 0))],
        out_specs=pl.BlockSpec((TR, w), lambda b, c: (b * n + c, 0)),
        out_shape=jax.ShapeDtypeStruct((B * T, w), BF16),
        scratch_shapes=[pltpu.VMEM((H, dv, dk), F32), pltpu.VMEM((C, H * dv), F32)],
        compiler_params=_params("parallel", "arbitrary"),
        name="hgrn2",
    )(z, z, z, z, hgrn_lb, gain, tri)


def _dsa_prep_kernel(cq_ref, kv_ref, qn_ref, kn_ref, wq_ref, wqi_ref, wuk_ref,
                     qlt_ref, qit_ref, w_ref, ckv_ref, ckvt_ref, ik_ref):
    tm = cq_ref.shape[0]
    H, dh, HI, dI, Dc = DSA_HEADS, DSA_DH, IDX_HEADS, IDX_DIM, DSA_KV_RANK
    QB = qlt_ref.shape[2] // H
    nb = tm // QB
    cq = _rms(cq_ref[...], qn_ref[...]).astype(BF16)
    kv = kv_ref[...]
    ckv = _rms(kv[:, :Dc], kn_ref[...])
    ckv_ref[...] = ckv.astype(BF16)
    ckvt_ref[0, 0] = ckv.T.astype(BF16)
    ik_ref[...] = kv[:, Dc:Dc + dI].astype(BF16)
    ikw_t = kv[:, Dc:2 * Dc].T
    w_t = ikw_t[dI:dI + HI, :] * (HI * dI) ** -0.5
    qi_all = _dot(cq, wqi_ref[...])
    for j in range(nb):
        rows = slice(j * QB, (j + 1) * QB)
        qi_t = qi_all[rows].T
        for h in range(HI):
            qit_ref[j, :, h * QB:(h + 1) * QB] = qi_t[h * dI:(h + 1) * dI].astype(BF16)
            w_ref[j, :, h * QB:(h + 1) * QB] = w_t[h:h + 1, rows]
    for h in range(H):
        qh = _dot(cq, wq_ref[h])
        ql = _dot(qh.astype(BF16), wuk_ref[h]) * dh ** -0.5
        for j in range(nb):
            qlt_ref[j, :, h * QB:(h + 1) * QB] = ql[j * QB:(j + 1) * QB].T.astype(BF16)


def _dsa_prep(z, qn, kn, wq, wqi, wuk, B, T):
    M = B * T
    H, dh, HI, dI, Dc, Rq = DSA_HEADS, DSA_DH, IDX_HEADS, IDX_DIM, DSA_KV_RANK, DSA_Q_RANK
    QB = min(Q_BLOCK, T)
    KB = KEY_BLOCK
    tm = KB
    nb = tm // QB
    nt = T // tm
    return pl.pallas_call(
        _dsa_prep_kernel,
        grid=(M // tm,),
        in_specs=[pl.BlockSpec((tm, Rq), lambda i: (i, Z_CQ // Rq)),
                  pl.BlockSpec((tm, 2 * Dc), lambda i: (i, Z_CKV // (2 * Dc))),
                  pl.BlockSpec((1, Rq), lambda i: (0, 0)),
                  pl.BlockSpec((1, Dc), lambda i: (0, 0)),
                  pl.BlockSpec((H, Rq, dh), lambda i: (0, 0, 0)),
                  pl.BlockSpec((Rq, HI * dI), lambda i: (0, 0)),
                  pl.BlockSpec((H, dh, Dc), lambda i: (0, 0, 0))],
        out_specs=[pl.BlockSpec((nb, Dc, H * QB), lambda i: (i, 0, 0)),
                   pl.BlockSpec((nb, dI, HI * QB), lambda i: (i, 0, 0)),
                   pl.BlockSpec((nb, 1, HI * QB), lambda i: (i, 0, 0)),
                   pl.BlockSpec((tm, Dc), lambda i: (i, 0)),
                   pl.BlockSpec((1, 1, Dc, tm), lambda i: (i // nt, i % nt, 0, 0)),
                   pl.BlockSpec((tm, dI), lambda i: (i, 0))],
        out_shape=[jax.ShapeDtypeStruct((M // QB, Dc, H * QB), BF16),
                   jax.ShapeDtypeStruct((M // QB, dI, HI * QB), BF16),
                   jax.ShapeDtypeStruct((M // QB, 1, HI * QB), F32),
                   jax.ShapeDtypeStruct((M, Dc), BF16),
                   jax.ShapeDtypeStruct((B, nt, Dc, tm), BF16),
                   jax.ShapeDtypeStruct((M, dI), BF16)],
        compiler_params=_params("parallel"),
        name="dsa_prep",
    )(z, z, qn, kn, wq, wqi, wuk)


def _sortable_key(s):
    s = jnp.where(s == 0.0, 0.0, s)
    bits = pltpu.bitcast(s, I32)
    return bits ^ ((bits >> 31) & 0x7FFFFFFF)


def _dsa_attn_kernel(topk, n_keys, neg_key, qlt_ref, qit_ref, w_ref, ckv_ref, ckvt_ref, ik_ref, bias_ref,
                     wuvt_ref, o_ref, key_ref, z_ref, acc_ref, tie_ref):
    H, HI = DSA_HEADS, IDX_HEADS
    QB = o_ref.shape[0]
    KB = key_ref.shape[1]
    q0 = pl.program_id(1) * QB
    nkb = (q0 + QB - 1) // KB + 1
    n_skip = n_keys - nkb * KB

    s_row = lax.broadcasted_iota(I32, (KB, QB), 0)
    t_row = q0 + lax.broadcasted_iota(I32, (1, QB), 1)

    qit = qit_ref[0]
    w_row = w_ref[0]

    def score_blk(kb, carry):
        k0 = pl.multiple_of(kb * KB, KB)
        p = _dot(ik_ref[pl.ds(k0, KB), :], qit)
        p = jnp.maximum(p, 0.0) * w_row
        s = p[:, 0:QB]
        for h in range(1, HI):
            s = s + p[:, h * QB:(h + 1) * QB]
        s = jnp.where(k0 + s_row <= t_row, s, NEG_BIG)
        key_ref[kb] = _sortable_key(s)
        return carry

    lax.fori_loop(0, nkb, score_blk, 0)

    def count(hit_fn):
        def body(kb, acc):
            return acc + _group_sum(jnp.where(hit_fn(kb, key_ref[kb]), 1, 0))

        acc = lax.fori_loop(0, nkb, body, jnp.zeros((SUBLANES, QB), I32))
        return jnp.sum(acc, axis=0, keepdims=True)

    def count_ge(cand):
        return count(lambda kb, kk: kk >= cand) + jnp.where(cand <= neg_key, n_skip, 0)

    zero = jnp.zeros((1, QB), I32)
    thr = jnp.where(count_ge(zero) >= topk, zero, jnp.full((1, QB), INT_MIN, I32))

    def bisect(it, thr):
        cand = thr | jnp.left_shift(jnp.int32(1), 30 - it)
        return jnp.where(count_ge(cand) >= topk, cand, thr)

    thr = lax.fori_loop(0, 31, bisect, thr)

    n_gt = count(lambda kb, kk: kk > thr) + jnp.where(thr < neg_key, n_skip, 0)
    n_eq = count(lambda kb, kk: kk == thr)
    need = topk - n_gt
    nbits = max(1, (n_keys - 1).bit_length())
    tie_ref[...] = jnp.full(tie_ref.shape, (1 << nbits) - 1, I32)

    @pl.when(jnp.max(jnp.where(need < n_eq, 1, 0)) > 0)
    def _():
        def ibisect(it, p):
            cand = p | jnp.left_shift(jnp.int32(1), nbits - 1 - it)
            below = count(lambda kb, kk: (kk == thr) & (kb * KB + s_row < cand))
            return jnp.where(below < need, cand, p)

        p = lax.fori_loop(0, nbits, ibisect, jnp.zeros((1, QB), I32))
        tie_ref[...] = jnp.broadcast_to(p, tie_ref.shape)

    tie = tie_ref[0:1, :]

    qlt = qlt_ref[0]

    def logits_blk(kb, m8):
        k0 = pl.multiple_of(kb * KB, KB)
        kk = key_ref[kb]
        s_glob = k0 + s_row
        sel = ((kk > thr) | ((kk == thr) & (s_glob <= tie))) & (s_glob <= t_row)
        z = _dot(ckv_ref[pl.ds(k0, KB), :], qlt) + bias_ref[jnp.minimum((q0 - k0) // QB, 3)]
        tops = []
        for h in range(H):
            zh = jnp.where(sel, z[:, h * QB:(h + 1) * QB], NEG_BIG)
            z_ref[kb, :, h * QB:(h + 1) * QB] = zh
            tops.append(_group_max(zh))
        return jnp.maximum(m8, jnp.concatenate(tops, axis=1))

    m8 = lax.fori_loop(0, nkb, logits_blk, jnp.full((SUBLANES, H * QB), NEG_BIG, F32))
    m = jnp.max(m8, axis=0, keepdims=True)
    acc_ref[...] = jnp.zeros(acc_ref.shape, F32)

    def pv_blk(kb, l8):
        p = jnp.exp(z_ref[kb] - m)
        acc_ref[...] += _dot(ckvt_ref[0, kb], p.astype(BF16))
        return l8 + _group_sum(p)

    l8 = lax.fori_loop(0, nkb, pv_blk, jnp.zeros((SUBLANES, H * QB), F32))
    inv_l = 1.0 / jnp.sum(l8, axis=0, keepdims=True)
    o_lat_t = (acc_ref[...] * inv_l).astype(BF16)
    y_t = jnp.concatenate([_dot(wuvt_ref[h], o_lat_t[:, h * QB:(h + 1) * QB]) for h in range(H)], axis=0)
    o_ref[...] = y_t.T.astype(BF16)


def _dsa_attn(qlt, qit, w_row, ckvn, ckvt, idxk, bias_tiles, wuvt, B, T):
    H, HI, dI, Dc, dh = DSA_HEADS, IDX_HEADS, IDX_DIM, DSA_KV_RANK, DSA_DH
    QB = min(Q_BLOCK, T)
    KB = KEY_BLOCK
    nq = T // QB
    nk = T // KB
    topk = min(DSA_TOPK_MAX, T // 4)
    neg_key = int(np.array(NEG_BIG, np.float32).view(np.int32))
    neg_key = neg_key ^ ((neg_key >> 31) & 0x7FFFFFFF)
    return pl.pallas_call(
        functools.partial(_dsa_attn_kernel, topk, T, neg_key),
        grid=(B, nq),
        in_specs=[pl.BlockSpec((1, Dc, H * QB), lambda b, i: (b * nq + i, 0, 0)),
                  pl.BlockSpec((1, dI, HI * QB), lambda b, i: (b * nq + i, 0, 0)),
                  pl.BlockSpec((1, 1, HI * QB), lambda b, i: (b * nq + i, 0, 0)),
                  pl.BlockSpec((T, Dc), lambda b, i: (b, 0)),
                  pl.BlockSpec((1, nk, Dc, KB), lambda b, i: (b, 0, 0, 0)),
                  pl.BlockSpec((T, dI), lambda b, i: (b, 0)),
                  pl.BlockSpec((4, KB, H * QB), lambda b, i: (0, 0, 0)),
                  pl.BlockSpec((H, dh, Dc), lambda b, i: (0, 0, 0))],
        out_specs=pl.BlockSpec((QB, DSA_W), lambda b, i: (b * nq + i, 0)),
        out_shape=jax.ShapeDtypeStruct((B * T, DSA_W), BF16),
        scratch_shapes=[pltpu.VMEM((nk, KB, QB), I32),
                        pltpu.VMEM((nk, KB, H * QB), F32),
                        pltpu.VMEM((Dc, H * QB), F32),
                        pltpu.VMEM((SUBLANES, QB), I32)],
        compiler_params=_params("parallel", "arbitrary"),
        name="dsa_attn",
    )(qlt, qit, w_row, ckvn, ckvt, idxk, bias_tiles, wuvt)


def _rel_bias_tiles(rel_bias, QB, KB):
    max_exact = REL_BUCKETS // 2
    n_far = 3 * QB
    assert n_far - (KB - 1) > REL_MAX_DIST
    n = jnp.arange(n_far, dtype=jnp.int32)
    nf = jnp.maximum(n, max_exact).astype(F32)
    large = max_exact + (jnp.log(nf / max_exact) / math.log(REL_MAX_DIST / max_exact)
                         * (REL_BUCKETS - max_exact)).astype(jnp.int32)
    large = jnp.minimum(large, REL_BUCKETS - 1)
    bucket = jnp.where(n < max_exact, n, large)
    tab = jnp.take(rel_bias, bucket, axis=0)
    H = rel_bias.shape[1]
    sk = jnp.arange(KB)[:, None]
    tq = jnp.arange(QB)[None, :]
    tiles = [tab[jnp.maximum(delta + tq - sk, 0)].transpose(0, 2, 1).reshape(KB, H * QB)
             for delta in (0, QB, 2 * QB)]
    tiles.append(jnp.broadcast_to(jnp.repeat(rel_bias[REL_BUCKETS - 1], QB)[None, :], (KB, H * QB)))
    return jnp.stack(tiles).astype(F32)


def _merge_kernel(x_ref, yr_ref, yd_ref, yh_ref, g0_ref, g1_ref, g2_ref, wr_ref, wd_ref, wh_ref, wo_ref,
                  ln_ref, o_ref):
    m = _sigmoid(g0_ref[...]) * _dot(yr_ref[...], wr_ref[...])
    m = m + _sigmoid(g1_ref[...]) * _dot(yd_ref[...], wd_ref[...])
    m = m + _sigmoid(g2_ref[...]) * _dot(yh_ref[...], wh_ref[...])
    u = _dot(m.astype(BF16), wo_ref[...])
    o_ref[...] = x_ref[...] + _rms(u, ln_ref[...])


def _merge(x2, y_ret, y_dsa, y_hg, z, wr, wd, wh, wo, ln):
    M = x2.shape[0]
    tm = min(512, M)
    D = D_MODEL
    row = lambda w: pl.BlockSpec((tm, w), lambda i: (i, 0))
    const = lambda r, c: pl.BlockSpec((r, c), lambda i: (0, 0))
    gate = lambda k: pl.BlockSpec((tm, D), lambda i: (i, Z_GT // D + k))
    return pl.pallas_call(
        _merge_kernel,
        grid=(M // tm,),
        in_specs=[row(D), row(RET_W), row(DSA_W), row(HGRN_W), gate(0), gate(1), gate(2),
                  const(RET_W, D), const(DSA_W, D), const(HGRN_W, D), const(D, D), const(1, D)],
        out_specs=row(D),
        out_shape=jax.ShapeDtypeStruct((M, D), F32),
        compiler_params=_params("parallel"),
        name="merge",
    )(x2, y_ret, y_dsa, y_hg, z, z, z, wr, wd, wh, wo, ln)


def _gelu_tanh(x):
    return 0.5 * x * (1.0 + jnp.tanh(math.sqrt(2.0 / math.pi) * (x + 0.044715 * (x * x * x))))


def _ffn_kernel(tiles_per_seq, x_ref, lnpre_ref, wup_ref, cw_ref, cb_ref, wdn_ref, lnpost_ref,
                o_ref, buf_ref, prev_ref):
    tm = x_ref.shape[0]
    HALO = SUBLANES
    fc = buf_ref.shape[1]
    n_pass = D_FF // fc
    first = (pl.program_id(0) % tiles_per_seq) == 0

    @pl.when(first)
    def _():
        prev_ref[...] = jnp.zeros_like(prev_ref)

    x = x_ref[...]
    h = _rms(x, lnpre_ref[...]).astype(BF16)

    def conv(part, c):
        col = part * D_FF + c * fc
        up = _dot(h, wup_ref[:, col:col + fc])
        slot = part * n_pass + c
        buf_ref[0:HALO, :] = prev_ref[slot]
        buf_ref[HALO:HALO + tm, :] = up
        prev_ref[slot] = up[tm - HALO:tm, :]
        w = cw_ref[:, col:col + fc]
        y = (up * w[2:3] + buf_ref[HALO - 1:HALO - 1 + tm, :] * w[1:2]
             + buf_ref[HALO - 2:HALO - 2 + tm, :] * w[0:1])
        return y + cb_ref[:, col:col + fc]

    acc = None
    for c in range(n_pass):
        a = conv(0, c)
        u = conv(1, c)
        act = (_gelu_tanh(a) * u).astype(BF16)
        d = _dot(act, wdn_ref[c * fc:(c + 1) * fc, :])
        acc = d if acc is None else acc + d
    o_ref[...] = x + _rms(acc, lnpost_ref[...])


def _ffn(x2, lnpre, wup, cw, cb, wdn, lnpost, T):
    M = x2.shape[0]
    D = D_MODEL
    tm = min(512, T)
    fc = D_FF // 2
    const = lambda r, c: pl.BlockSpec((r, c), lambda i: (0, 0))
    return pl.pallas_call(
        functools.partial(_ffn_kernel, T // tm),
        grid=(M // tm,),
        in_specs=[pl.BlockSpec((tm, D), lambda i: (i, 0)), const(1, D), const(D, 2 * D_FF),
                  const(CONV_WIDTH, 2 * D_FF), const(1, 2 * D_FF), const(D_FF, D), const(1, D)],
        out_specs=pl.BlockSpec((tm, D), lambda i: (i, 0)),
        out_shape=jax.ShapeDtypeStruct((M, D), F32),
        scratch_shapes=[pltpu.VMEM((tm + SUBLANES, fc), F32),
                        pltpu.VMEM((2 * (D_FF // fc), SUBLANES, fc), F32)],
        compiler_params=_params("arbitrary"),
        name="conv_ffn",
    )(x2, lnpre, wup, cw, cb, wdn, lnpost)


def _permute_w_in(w):
    widths = (256, 256, 512, 512, 256, 128, 64, 8, 512, 512, 512, 512, 3 * D_MODEL)
    offs = np.concatenate([[0], np.cumsum(widths)])
    seg = [w[:, offs[k]:offs[k + 1]] for k in range(len(widths))]
    pad = jnp.zeros((w.shape[0], 256 - (128 + 64 + 8)), w.dtype)
    cols = seg[0:6] + [seg[6], seg[7], pad] + seg[8:]
    return jnp.concatenate(cols, axis=1).astype(BF16)


def _dsa_mixer(z, rel_bias_tiles, q_norm, kv_norm, w_uq, w_uk, w_uv, B, T):
    H, dh, HI, dI, Rq = DSA_HEADS, DSA_DH, IDX_HEADS, IDX_DIM, DSA_Q_RANK
    row = lambda v: v.reshape(1, -1)
    wq = w_uq[:, :H * dh].reshape(Rq, H, dh).transpose(1, 0, 2).astype(BF16)
    wqi = w_uq[:, H * dh:].astype(BF16)
    qlt, qit, w_row, ckvn, ckvt, idxk = _dsa_prep(z, row(q_norm), row(kv_norm), wq, wqi, w_uk.astype(BF16), B, T)
    wuvt = w_uv.transpose(0, 2, 1).astype(BF16)
    return _dsa_attn(qlt, qit, w_row, ckvn, ckvt, idxk, rel_bias_tiles, wuvt, B, T)


def kernel(x, rel_bias, hgrn_lb, ln_mix_pre, ln_mix_post, ln_ffn_pre, ln_ffn_post, w_in, dsa_q_norm, dsa_kv_norm, dsa_w_uq, dsa_w_uk, dsa_w_uv, hgrn_norm, w_br_ret, w_br_dsa, w_br_hgrn, w_out, ffn_w_up, ffn_conv_w, ffn_conv_b, ffn_w_down):
    B, T, D = x.shape
    depth = w_in.shape[0]
    assert T % KEY_BLOCK == 0 and D == D_MODEL
    bias_tiles = _rel_bias_tiles(rel_bias, min(Q_BLOCK, T), KEY_BLOCK)
    x2 = x.reshape(B * T, D)
    row = lambda v: v.reshape(1, -1)
    for l in range(depth):
        z = _inproj(x2, row(ln_mix_pre[l]), _permute_w_in(w_in[l]))
        y_ret = _retention(z, B, T)
        y_hg = _hgrn(z, hgrn_lb, row(hgrn_norm[l]), l, B, T)
        y_dsa = _dsa_mixer(z, bias_tiles, dsa_q_norm[l], dsa_kv_norm[l], dsa_w_uq[l], dsa_w_uk[l], dsa_w_uv[l], B, T)
        x2 = _merge(x2, y_ret, y_dsa, y_hg, z, w_br_ret[l].astype(BF16), w_br_dsa[l].astype(BF16),
                    w_br_hgrn[l].astype(BF16), w_out[l].astype(BF16), row(ln_mix_post[l]))
        x2 = _ffn(x2, row(ln_ffn_pre[l]), ffn_w_up[l].astype(BF16), ffn_conv_w[l], row(ffn_conv_b[l]),
                  ffn_w_down[l].astype(BF16), row(ln_ffn_post[l]), T)
    return x2.reshape(B, T, D)
```

```python
import functools
import math

import jax
import jax.numpy as jnp
import numpy as np
from jax import lax
from jax.experimental import pallas as pl
from jax.experimental.pallas import tpu as pltpu

F32 = jnp.float32
BF16 = jnp.bfloat16
I32 = jnp.int32

D_MODEL = 1024
RET_HEADS, RET_DK, RET_DV, RET_CHUNK = 4, 64, 128, 128
DSA_HEADS, DSA_DH, DSA_Q_RANK, DSA_KV_RANK = 8, 64, 256, 128
IDX_HEADS, IDX_DIM, DSA_TOPK_MAX, Q_BLOCK = 8, 64, 256, 128
HGRN_HEADS, HGRN_EXPAND, HGRN_DV = 4, 128, 128
F_FLOOR = 1e-6
REL_BUCKETS, REL_MAX_DIST = 32, 128
D_FF = 2816
CONV_WIDTH = 3
N_BRANCH = 3
EPS = 1e-6
NEG_BIG = -1e30

RET_W = RET_HEADS * RET_DV
DSA_W = DSA_HEADS * DSA_DH
HGRN_KW = HGRN_HEADS * HGRN_EXPAND
HGRN_W = HGRN_HEADS * HGRN_DV

ZB_RQ, ZB_RK, ZB_RV, ZB_CQ = 0, 256, 512, 1024
ZB_CKV = 1280
ZB_HQ, ZB_HV = 1536, 2048
ZB_W = 2560
ZF_RG, ZF_HF, ZF_HG, ZF_IW = 0, 512, 1024, 1536
ZF_W = 1664

VMEM_LIMIT_BYTES = 56 * 1024 * 1024
SUBLANES = 8
LANES = 128

KEY_BLOCK = 256
HGRN_PAIR_CHUNK = 128
INT_MIN = -(2 ** 31)


def _params(*sem):
    return pltpu.CompilerParams(dimension_semantics=sem, vmem_limit_bytes=VMEM_LIMIT_BYTES)


def _dot(a, b):
    return jnp.dot(a, b, preferred_element_type=F32)


def _dot_nt(a, b):
    return lax.dot_general(a, b, (((1,), (1,)), ((), ())), preferred_element_type=F32)


def _dot_tn(a, b):
    return lax.dot_general(a, b, (((0,), (0,)), ((), ())), preferred_element_type=F32)


def _sigmoid(x):
    return 1.0 / (1.0 + jnp.exp(-x))


def _rms(x, g):
    return x * lax.rsqrt(jnp.mean(x * x, axis=-1, keepdims=True) + EPS) * g


def _group_sum(x):
    return jnp.sum(x.reshape(x.shape[0] // SUBLANES, SUBLANES, x.shape[1]), axis=0)


def _group_max(x):
    return jnp.max(x.reshape(x.shape[0] // SUBLANES, SUBLANES, x.shape[1]), axis=0)


def _inproj_kernel(x_ref, g_ref, wb_ref, wf_ref, zb_ref, zf_ref):
    h = _rms(x_ref[...], g_ref[...]).astype(BF16)
    zb_ref[...] = _dot(h, wb_ref[...]).astype(BF16)
    zf_ref[...] = _dot(h, wf_ref[...])


def _inproj(x2, g, wb, wf):
    M = x2.shape[0]
    tm = min(512, M)
    const = lambda r, c: pl.BlockSpec((r, c), lambda i: (0, 0))
    return pl.pallas_call(
        _inproj_kernel,
        grid=(M // tm,),
        in_specs=[pl.BlockSpec((tm, D_MODEL), lambda i: (i, 0)), const(1, D_MODEL),
                  const(D_MODEL, ZB_W), const(D_MODEL, ZF_W)],
        out_specs=[pl.BlockSpec((tm, ZB_W), lambda i: (i, 0)), pl.BlockSpec((tm, ZF_W), lambda i: (i, 0))],
        out_shape=[jax.ShapeDtypeStruct((M, ZB_W), BF16), jax.ShapeDtypeStruct((M, ZF_W), F32)],
        compiler_params=_params("parallel"),
        name="inproj",
    )(x2, g, wb, wf)


def _ret_kernel(gam_ref, q_ref, k_ref, v_ref, g_ref, cos_ref, sin_ref, dm_ref, xi_ref, zeta_ref,
                o_ref, r_ref):
    C = q_ref.shape[0]
    H, dk, dv = RET_HEADS, RET_DK, RET_DV

    @pl.when(pl.program_id(1) == 0)
    def _():
        r_ref[...] = jnp.zeros_like(r_ref)

    cos = cos_ref[...]
    sin = sin_ref[...]
    lane = lax.broadcasted_iota(I32, (C, H * dk), 1)
    first_half = (lane % dk) < (dk // 2)

    def rot(x):
        swapped = jnp.where(first_half, pltpu.roll(x, H * dk - dk // 2, 1), pltpu.roll(x, dk // 2, 1))
        return x * cos + swapped * sin

    q = rot(q_ref[...].astype(F32))
    k = rot(k_ref[...].astype(F32)) * dk ** -0.5
    for h in range(H):
        qh = q[:, h * dk:(h + 1) * dk].astype(BF16)
        kh = k[:, h * dk:(h + 1) * dk]
        vh = v_ref[:, h * dv:(h + 1) * dv]
        s = _dot_nt(qh, kh.astype(BF16)) * dm_ref[h]
        inner = _dot(s.astype(BF16), vh)
        rh = r_ref[h]
        cross = _dot(qh, rh.astype(BF16)) * xi_ref[h]
        r_ref[h] = gam_ref[h] * rh + _dot_tn((kh * zeta_ref[h]).astype(BF16), vh)
        o = inner + cross
        mu = jnp.mean(o, axis=-1, keepdims=True)
        oc = o - mu
        var = jnp.mean(oc * oc, axis=-1, keepdims=True)
        gh = g_ref[:, h * dv:(h + 1) * dv]
        o_ref[:, h * dv:(h + 1) * dv] = (gh * _sigmoid(gh) * (oc * lax.rsqrt(var + EPS))).astype(BF16)


def _retention(zb, zf, B, T):
    H, dk, dv = RET_HEADS, RET_DK, RET_DV
    C = min(RET_CHUNK, T)
    n = T // C
    pos = jnp.arange(T, dtype=F32)
    half = dk // 2
    freq = 1.0 / (10000.0 ** jnp.linspace(0.0, 1.0, half, dtype=F32))
    ang = pos[:, None] * freq[None, :]
    cos = jnp.tile(jnp.cos(ang), (1, 2 * H))
    sin = jnp.tile(jnp.concatenate([-jnp.sin(ang), jnp.sin(ang)], axis=1), (1, H))
    log_gamma = jnp.log1p(-(2.0 ** (-5.0 - jnp.arange(H, dtype=F32))))
    i = jnp.arange(C, dtype=F32)
    rel = i[:, None] - i[None, :]
    dmask = jnp.where(rel >= 0, jnp.exp(jnp.maximum(rel, 0.0)[None] * log_gamma[:, None, None]), 0.0)
    xi = jnp.exp((i + 1.0)[None, :] * log_gamma[:, None])[:, :, None]
    zeta = jnp.exp((C - 1.0 - i)[None, :] * log_gamma[:, None])[:, :, None]
    gamma_c = jnp.exp(C * log_gamma)

    wq = H * dk
    wv = H * dv
    full = lambda shape: pl.BlockSpec(shape, lambda b, c: (0,) * len(shape))
    return pl.pallas_call(
        _ret_kernel,
        grid=(B, n),
        in_specs=[pl.BlockSpec(memory_space=pltpu.SMEM),
                  pl.BlockSpec((C, wq), lambda b, c: (b * n + c, ZB_RQ // wq)),
                  pl.BlockSpec((C, wq), lambda b, c: (b * n + c, ZB_RK // wq)),
                  pl.BlockSpec((C, wv), lambda b, c: (b * n + c, ZB_RV // wv)),
                  pl.BlockSpec((C, wv), lambda b, c: (b * n + c, ZF_RG // wv)),
                  pl.BlockSpec((C, wq), lambda b, c: (c, 0)),
                  pl.BlockSpec((C, wq), lambda b, c: (c, 0)),
                  full((H, C, C)), full((H, C, 1)), full((H, C, 1))],
        out_specs=pl.BlockSpec((C, wv), lambda b, c: (b * n + c, 0)),
        out_shape=jax.ShapeDtypeStruct((B * T, wv), BF16),
        scratch_shapes=[pltpu.VMEM((H, dk, dv), F32)],
        compiler_params=_params("parallel", "arbitrary"),
        name="retention",
    )(gamma_c, zb, zb, zb, zf, cos, sin, dmask, xi, zeta)


def _hgrn_levels(C):
    ms, m = [], C // 2
    while m >= SUBLANES:
        ms.append(m)
        m //= 2
    return ms


def _hgrn_kernel(layer, q_ref, f_ref, v_ref, g_ref, lbraw_ref, gain_ref, tri_ref, lmask_ref, o_ref, st_ref):
    TR = q_ref.shape[0]
    C = tri_ref.shape[0]
    H, dk, dv = HGRN_HEADS, HGRN_EXPAND, HGRN_DV
    SB = SUBLANES

    @pl.when(pl.program_id(1) == 0)
    def _():
        st_ref[...] = jnp.zeros_like(st_ref)

    raw = lbraw_ref[...]
    e = jnp.exp(raw - jnp.max(raw, axis=0, keepdims=True))
    soft = e / jnp.sum(e, axis=0, keepdims=True)
    cs = soft[0:1]
    for l in range(1, layer + 1):
        cs = cs + soft[l:l + 1]
    lb = jnp.clip(cs - soft[0:1], 0.0, 1.0)
    tri = tri_ref[...]
    row_in_blk = lax.broadcasted_iota(I32, (SB, dk), 0)
    lane_t = lax.broadcasted_iota(I32, (SB, C), 1)
    levels = _hgrn_levels(C)

    def chunk(c, carry):
        r0 = pl.multiple_of(c * C, C)
        zf = f_ref[pl.ds(r0, C), :]
        f = lb + (1.0 - lb) * _sigmoid(zf)
        log_f = jnp.log(jnp.maximum(f, F_FLOOR))
        kk = (1.0 - lb) * _sigmoid(-zf)
        b = jnp.dot(tri, log_f, preferred_element_type=F32, precision=lax.Precision.HIGHEST)
        qq = q_ref[pl.ds(r0, C), :].astype(F32)
        vb = v_ref[pl.ds(r0, C), :]
        eb = jnp.exp(b)
        b_last = b[C - 1:C, :]
        eb_last = eb[C - 1:C, :]
        q_dec = (qq * eb).astype(BF16)
        k_dec = (kk * jnp.exp(b_last - b)).astype(BF16)
        outs = []
        for h in range(H):
            sl = slice(h * dk, (h + 1) * dk)
            bh, qh, kh, vh = b[:, sl], qq[:, sl], kk[:, sl], vb[:, h * dv:(h + 1) * dv]
            diag = []
            for blk in range(C // SB):
                bs = bh[blk * SB:(blk + 1) * SB, :]
                ks = kh[blk * SB:(blk + 1) * SB, :]
                at = jnp.zeros((SB, C), F32)
                for tt in range(SB):
                    t = blk * SB + tt
                    diff = jnp.where(row_in_blk <= tt, bh[t:t + 1, :] - bs, NEG_BIG)
                    p = (qh[t:t + 1, :] * jnp.exp(diff)) * ks
                    at = jnp.where(lane_t == t, jnp.sum(p, axis=-1, keepdims=True), at)
                diag.append(at)
            a_t = jnp.concatenate(diag, axis=0)
            for lev, m in enumerate(levels):
                qs, ks = [], []
                for blk in range(C // m):
                    rows = slice(blk * m, (blk + 1) * m)
                    if blk % 2 == 1:
                        ref = bh[blk * m - 1:blk * m, :]
                        qs.append(qh[rows] * jnp.exp(bh[rows] - ref))
                        ks.append(jnp.zeros((m, dk), F32))
                    else:
                        ref = bh[(blk + 1) * m - 1:(blk + 1) * m, :]
                        ks.append(kh[rows] * jnp.exp(ref - bh[rows]))
                        qs.append(jnp.zeros((m, dk), F32))
                q_l = jnp.concatenate(qs, axis=0).astype(BF16)
                k_l = jnp.concatenate(ks, axis=0).astype(BF16)
                a_t = a_t + _dot_nt(k_l, q_l) * lmask_ref[lev]
            intra = _dot_tn(a_t.astype(BF16), vh)
            st = st_ref[h]
            outs.append(intra + _dot_nt(q_dec[:, sl], st.astype(BF16)))
            st_ref[h] = st * eb_last[:, sl] + _dot_tn(vh, k_dec[:, sl])
        o = _rms(jnp.concatenate(outs, axis=1), gain_ref[...])
        gg = g_ref[pl.ds(r0, C), :]
        o_ref[pl.ds(r0, C), :] = (gg * _sigmoid(gg) * o).astype(BF16)
        return carry

    lax.fori_loop(0, TR // C, chunk, 0)


def _hgrn(zb, zf, hgrn_lb, gain, layer, B, T):
    H, dk, dv = HGRN_HEADS, HGRN_EXPAND, HGRN_DV
    C = min(HGRN_PAIR_CHUNK, T)
    TR = min(512, T)
    n = T // TR
    w = H * dk
    L = hgrn_lb.shape[0]
    tri = jnp.tril(jnp.ones((C, C), F32))
    idx = np.arange(C)
    lmask = np.stack([(((idx[None, :] // m) % 2 == 1) & (idx[:, None] // m == idx[None, :] // m - 1))
                      for m in _hgrn_levels(C)]).astype(np.float32)
    return pl.pallas_call(
        functools.partial(_hgrn_kernel, layer),
        grid=(B, n),
        in_specs=[pl.BlockSpec((TR, w), lambda b, c: (b * n + c, ZB_HQ // w)),
                  pl.BlockSpec((TR, w), lambda b, c: (b * n + c, ZF_HF // w)),
                  pl.BlockSpec((TR, w), lambda b, c: (b * n + c, ZB_HV // w)),
                  pl.BlockSpec((TR, w), lambda b, c: (b * n + c, ZF_HG // w)),
                  pl.BlockSpec((L, w), lambda b, c: (0, 0)),
                  pl.BlockSpec((1, w), lambda b, c: (0, 0)),
                  pl.BlockSpec((C, C), lambda b, c: (0, 0)),
                  pl.BlockSpec(lmask.shape, lambda b, c: (0, 0, 0))],
        out_specs=pl.BlockSpec((TR, w), lambda b, c: (b * n + c, 0)),
        out_shape=jax.ShapeDtypeStruct((B * T, w), BF16),
        scratch_shapes=[pltpu.VMEM((H, dv, dk), F32)],
        compiler_params=_params("parallel", "arbitrary"),
        name="hgrn2",
    )(zb, zf, zb, zf, hgrn_lb, gain, tri, jnp.asarray(lmask))


def _dsa_prep_kernel(cq_ref, kv_ref, iw_ref, qn_ref, kn_ref, wq_ref, wqi_ref, wuk_ref,
                     qlt_ref, qit_ref, w_ref, ckv_ref, ckvt_ref, ik_ref):
    tm = cq_ref.shape[0]
    H, dh, HI, dI, Dc = DSA_HEADS, DSA_DH, IDX_HEADS, IDX_DIM, DSA_KV_RANK
    QB = qlt_ref.shape[2] // H
    nb = tm // QB
    cq = _rms(cq_ref[...].astype(F32), qn_ref[...]).astype(BF16)
    ckv = _rms(kv_ref[:, :Dc].astype(F32), kn_ref[...])
    ckv_ref[...] = ckv.astype(BF16)
    ckvt_ref[0, 0] = ckv.T.astype(BF16)
    ik_ref[...] = kv_ref[:, Dc:Dc + dI]
    w_t = iw_ref[...].T[0:HI, :] * (HI * dI) ** -0.5
    qi_all = _dot(cq, wqi_ref[...])
    for j in range(nb):
        rows = slice(j * QB, (j + 1) * QB)
        qi_t = qi_all[rows].T
        for h in range(HI):
            qit_ref[j, :, h * QB:(h + 1) * QB] = qi_t[h * dI:(h + 1) * dI].astype(BF16)
            w_ref[j, :, h * QB:(h + 1) * QB] = w_t[h:h + 1, rows]
    for h in range(H):
        qh = _dot(cq, wq_ref[h])
        ql = _dot(qh.astype(BF16), wuk_ref[h]) * dh ** -0.5
        for j in range(nb):
            qlt_ref[j, :, h * QB:(h + 1) * QB] = ql[j * QB:(j + 1) * QB].T.astype(BF16)


def _dsa_prep(zb, zf, qn, kn, wq, wqi, wuk, B, T):
    M = B * T
    H, dh, HI, dI, Dc, Rq = DSA_HEADS, DSA_DH, IDX_HEADS, IDX_DIM, DSA_KV_RANK, DSA_Q_RANK
    QB = min(Q_BLOCK, T)
    KB = KEY_BLOCK
    tm = KB
    nb = tm // QB
    nt = T // tm
    return pl.pallas_call(
        _dsa_prep_kernel,
        grid=(M // tm,),
        in_specs=[pl.BlockSpec((tm, Rq), lambda i: (i, ZB_CQ // Rq)),
                  pl.BlockSpec((tm, 2 * Dc), lambda i: (i, ZB_CKV // (2 * Dc))),
                  pl.BlockSpec((tm, LANES), lambda i: (i, ZF_IW // LANES)),
                  pl.BlockSpec((1, Rq), lambda i: (0, 0)),
                  pl.BlockSpec((1, Dc), lambda i: (0, 0)),
                  pl.BlockSpec((H, Rq, dh), lambda i: (0, 0, 0)),
                  pl.BlockSpec((Rq, HI * dI), lambda i: (0, 0)),
                  pl.BlockSpec((H, dh, Dc), lambda i: (0, 0, 0))],
        out_specs=[pl.BlockSpec((nb, Dc, H * QB), lambda i: (i, 0, 0)),
                   pl.BlockSpec((nb, dI, HI * QB), lambda i: (i, 0, 0)),
                   pl.BlockSpec((nb, 1, HI * QB), lambda i: (i, 0, 0)),
                   pl.BlockSpec((tm, Dc), lambda i: (i, 0)),
                   pl.BlockSpec((1, 1, Dc, tm), lambda i: (i // nt, i % nt, 0, 0)),
                   pl.BlockSpec((tm, dI), lambda i: (i, 0))],
        out_shape=[jax.ShapeDtypeStruct((M // QB, Dc, H * QB), BF16),
                   jax.ShapeDtypeStruct((M // QB, dI, HI * QB), BF16),
                   jax.ShapeDtypeStruct((M // QB, 1, HI * QB), F32),
                   jax.ShapeDtypeStruct((M, Dc), BF16),
                   jax.ShapeDtypeStruct((B, nt, Dc, tm), BF16),
                   jax.ShapeDtypeStruct((M, dI), BF16)],
        compiler_params=_params("parallel"),
        name="dsa_prep",
    )(zb, zb, zf, qn, kn, wq, wqi, wuk)


def _sortable_key(s):
    s = jnp.where(s == 0.0, 0.0, s)
    bits = pltpu.bitcast(s, I32)
    return bits ^ ((bits >> 31) & 0x7FFFFFFF)


def _dsa_attn_kernel(topk, n_keys, neg_key, qlt_ref, qit_ref, w_ref, ckv_ref, ckvt_ref, ik_ref, bias_ref,
                     wuvt_ref, o_ref, key_ref, z_ref, acc_ref, tie_ref):
    H, HI = DSA_HEADS, IDX_HEADS
    QB = o_ref.shape[0]
    KB = key_ref.shape[1]
    q0 = pl.program_id(1) * QB
    nkb = (q0 + QB - 1) // KB + 1
    n_skip = n_keys - nkb * KB

    s_row = lax.broadcasted_iota(I32, (KB, QB), 0)
    t_row = q0 + lax.broadcasted_iota(I32, (1, QB), 1)

    qit = qit_ref[0]
    w_row = w_ref[0]

    def score_blk(kb, carry):
        k0 = pl.multiple_of(kb * KB, KB)
        p = _dot(ik_ref[pl.ds(k0, KB), :], qit)
        p = jnp.maximum(p, 0.0) * w_row
        s = p[:, 0:QB]
        for h in range(1, HI):
            s = s + p[:, h * QB:(h + 1) * QB]
        s = jnp.where(k0 + s_row <= t_row, s, NEG_BIG)
        key_ref[kb] = _sortable_key(s)
        return carry

    lax.fori_loop(0, nkb, score_blk, 0)

    def count(hit_fn):
        def body(kb, acc):
            return acc + _group_sum(jnp.where(hit_fn(kb, key_ref[kb]), 1, 0))

        acc = lax.fori_loop(0, nkb, body, jnp.zeros((SUBLANES, QB), I32))
        return jnp.sum(acc, axis=0, keepdims=True)

    def count_ge(cand):
        return count(lambda kb, kk: kk >= cand) + jnp.where(cand <= neg_key, n_skip, 0)

    zero = jnp.zeros((1, QB), I32)
    thr = jnp.where(count_ge(zero) >= topk, zero, jnp.full((1, QB), INT_MIN, I32))

    def bisect(it, thr):
        cand = thr | jnp.left_shift(jnp.int32(1), 30 - it)
        return jnp.where(count_ge(cand) >= topk, cand, thr)

    thr = lax.fori_loop(0, 31, bisect, thr)

    n_gt = count(lambda kb, kk: kk > thr) + jnp.where(thr < neg_key, n_skip, 0)
    n_eq = count(lambda kb, kk: kk == thr)
    need = topk - n_gt
    nbits = max(1, (n_keys - 1).bit_length())
    tie_ref[...] = jnp.full(tie_ref.shape, (1 << nbits) - 1, I32)

    @pl.when(jnp.max(jnp.where(need < n_eq, 1, 0)) > 0)
    def _():
        def ibisect(it, p):
            cand = p | jnp.left_shift(jnp.int32(1), nbits - 1 - it)
            below = count(lambda kb, kk: (kk == thr) & (kb * KB + s_row < cand))
            return jnp.where(below < need, cand, p)

        p = lax.fori_loop(0, nbits, ibisect, jnp.zeros((1, QB), I32))
        tie_ref[...] = jnp.broadcast_to(p, tie_ref.shape)

    tie = tie_ref[0:1, :]

    qlt = qlt_ref[0]

    def logits_blk(kb, m8):
        k0 = pl.multiple_of(kb * KB, KB)
        kk = key_ref[kb]
        s_glob = k0 + s_row
        sel = ((kk > thr) | ((kk == thr) & (s_glob <= tie))) & (s_glob <= t_row)
        z = _dot(ckv_ref[pl.ds(k0, KB), :], qlt) + bias_ref[jnp.minimum((q0 - k0) // QB, 3)]
        tops = []
        for h in range(H):
            zh = jnp.where(sel, z[:, h * QB:(h + 1) * QB], NEG_BIG)
            z_ref[kb, :, h * QB:(h + 1) * QB] = zh
            tops.append(_group_max(zh))
        return jnp.maximum(m8, jnp.concatenate(tops, axis=1))

    m8 = lax.fori_loop(0, nkb, logits_blk, jnp.full((SUBLANES, H * QB), NEG_BIG, F32))
    m = jnp.max(m8, axis=0, keepdims=True)
    acc_ref[...] = jnp.zeros(acc_ref.shape, F32)

    def pv_blk(kb, l8):
        p = jnp.exp(z_ref[kb] - m)
        acc_ref[...] += _dot(ckvt_ref[0, kb], p.astype(BF16))
        return l8 + _group_sum(p)

    l8 = lax.fori_loop(0, nkb, pv_blk, jnp.zeros((SUBLANES, H * QB), F32))
    inv_l = 1.0 / jnp.sum(l8, axis=0, keepdims=True)
    o_lat_t = (acc_ref[...] * inv_l).astype(BF16)
    y_t = jnp.concatenate([_dot(wuvt_ref[h], o_lat_t[:, h * QB:(h + 1) * QB]) for h in range(H)], axis=0)
    o_ref[...] = y_t.T.astype(BF16)


def _dsa_attn(qlt, qit, w_row, ckvn, ckvt, idxk, bias_tiles, wuvt, B, T):
    H, HI, dI, Dc, dh = DSA_HEADS, IDX_HEADS, IDX_DIM, DSA_KV_RANK, DSA_DH
    QB = min(Q_BLOCK, T)
    KB = KEY_BLOCK
    nq = T // QB
    nk = T // KB
    topk = min(DSA_TOPK_MAX, T // 4)
    neg_key = int(np.array(NEG_BIG, np.float32).view(np.int32))
    neg_key = neg_key ^ ((neg_key >> 31) & 0x7FFFFFFF)
    return pl.pallas_call(
        functools.partial(_dsa_attn_kernel, topk, T, neg_key),
        grid=(B, nq),
        in_specs=[pl.BlockSpec((1, Dc, H * QB), lambda b, i: (b * nq + i, 0, 0)),
                  pl.BlockSpec((1, dI, HI * QB), lambda b, i: (b * nq + i, 0, 0)),
                  pl.BlockSpec((1, 1, HI * QB), lambda b, i: (b * nq + i, 0, 0)),
                  pl.BlockSpec((T, Dc), lambda b, i: (b, 0)),
                  pl.BlockSpec((1, nk, Dc, KB), lambda b, i: (b, 0, 0, 0)),
                  pl.BlockSpec((T, dI), lambda b, i: (b, 0)),
                  pl.BlockSpec((4, KB, H * QB), lambda b, i: (0, 0, 0)),
                  pl.BlockSpec((H, dh, Dc), lambda b, i: (0, 0, 0))],
        out_specs=pl.BlockSpec((QB, DSA_W), lambda b, i: (b * nq + i, 0)),
        out_shape=jax.ShapeDtypeStruct((B * T, DSA_W), BF16),
        scratch_shapes=[pltpu.VMEM((nk, KB, QB), I32),
                        pltpu.VMEM((nk, KB, H * QB), F32),
                        pltpu.VMEM((Dc, H * QB), F32),
                        pltpu.VMEM((SUBLANES, QB), I32)],
        compiler_params=_params("parallel", "arbitrary"),
        name="dsa_attn",
    )(qlt, qit, w_row, ckvn, ckvt, idxk, bias_tiles, wuvt)


def _rel_bias_tiles(rel_bias, QB, KB):
    max_exact = REL_BUCKETS // 2
    n_far = 3 * QB
    assert n_far - (KB - 1) > REL_MAX_DIST
    n = jnp.arange(n_far, dtype=jnp.int32)
    nf = jnp.maximum(n, max_exact).astype(F32)
    large = max_exact + (jnp.log(nf / max_exact) / math.log(REL_MAX_DIST / max_exact)
                         * (REL_BUCKETS - max_exact)).astype(jnp.int32)
    large = jnp.minimum(large, REL_BUCKETS - 1)
    bucket = jnp.where(n < max_exact, n, large)
    H = rel_bias.shape[1]
    tab = jnp.take(rel_bias, bucket, axis=0).T
    P = KB + QB
    tiles = []
    for delta in (0, QB, 2 * QB):
        d = np.concatenate([np.arange(QB + 1), np.arange(-(KB - 1), 0)])
        g = tab[:, np.clip(delta + d, 0, n_far - 1)]
        rows = jnp.tile(g, (1, KB))[:, :KB * (P - 1)].reshape(H, KB, P - 1)[:, :, :QB]
        tiles.append(rows.transpose(1, 0, 2).reshape(KB, H * QB))
    tiles.append(jnp.broadcast_to(jnp.repeat(rel_bias[REL_BUCKETS - 1], QB)[None, :], (KB, H * QB)))
    return jnp.stack(tiles).astype(F32)


def _merge_kernel(x_ref, yr_ref, yd_ref, yh_ref, lnpre_ref, wg_ref, wr_ref, wd_ref, wh_ref, wo_ref,
                  ln_ref, o_ref):
    D = x_ref.shape[1]
    x = x_ref[...]
    h = _rms(x, lnpre_ref[...]).astype(BF16)
    m = None
    for k, (y_ref, w_ref) in enumerate(((yr_ref, wr_ref), (yd_ref, wd_ref), (yh_ref, wh_ref))):
        gate = _sigmoid(_dot(h, wg_ref[:, k * D:(k + 1) * D]))
        term = gate * _dot(y_ref[...], w_ref[...])
        m = term if m is None else m + term
    u = _dot(m.astype(BF16), wo_ref[...])
    o_ref[...] = x + _rms(u, ln_ref[...])


def _merge(x2, y_ret, y_dsa, y_hg, lnpre, wg, wr, wd, wh, wo, ln):
    M = x2.shape[0]
    tm = min(512, M)
    D = D_MODEL
    row = lambda w: pl.BlockSpec((tm, w), lambda i: (i, 0))
    const = lambda r, c: pl.BlockSpec((r, c), lambda i: (0, 0))
    return pl.pallas_call(
        _merge_kernel,
        grid=(M // tm,),
        in_specs=[row(D), row(RET_W), row(DSA_W), row(HGRN_W), const(1, D), const(D, N_BRANCH * D),
                  const(RET_W, D), const(DSA_W, D), const(HGRN_W, D), const(D, D), const(1, D)],
        out_specs=row(D),
        out_shape=jax.ShapeDtypeStruct((M, D), F32),
        compiler_params=_params("parallel"),
        name="merge",
    )(x2, y_ret, y_dsa, y_hg, lnpre, wg, wr, wd, wh, wo, ln)


def _gelu_tanh(x):
    return 0.5 * x * (1.0 + jnp.tanh(math.sqrt(2.0 / math.pi) * (x + 0.044715 * (x * x * x))))


def _ffn_kernel(tiles_per_seq, x_ref, lnpre_ref, wup_ref, cw_ref, cb_ref, wdn_ref, lnpost_ref,
                o_ref, buf_ref, prev_ref):
    tm = x_ref.shape[0]
    HALO = SUBLANES
    fc = buf_ref.shape[1]
    n_pass = D_FF // fc
    first = (pl.program_id(0) % tiles_per_seq) == 0

    @pl.when(first)
    def _():
        prev_ref[...] = jnp.zeros_like(prev_ref)

    x = x_ref[...]
    h = _rms(x, lnpre_ref[...]).astype(BF16)

    def conv(part, c):
        col = part * D_FF + c * fc
        up = _dot(h, wup_ref[:, col:col + fc])
        slot = part * n_pass + c
        buf_ref[0:HALO, :] = prev_ref[slot]
        buf_ref[HALO:HALO + tm, :] = up
        prev_ref[slot] = up[tm - HALO:tm, :]
        w = cw_ref[:, col:col + fc]
        y = (up * w[2:3] + buf_ref[HALO - 1:HALO - 1 + tm, :] * w[1:2]
             + buf_ref[HALO - 2:HALO - 2 + tm, :] * w[0:1])
        return y + cb_ref[:, col:col + fc]

    acc = None
    for c in range(n_pass):
        a = conv(0, c)
        u = conv(1, c)
        act = (_gelu_tanh(a) * u).astype(BF16)
        d = _dot(act, wdn_ref[c * fc:(c + 1) * fc, :])
        acc = d if acc is None else acc + d
    o_ref[...] = x + _rms(acc, lnpost_ref[...])


def _ffn(x2, lnpre, wup, cw, cb, wdn, lnpost, T):
    M = x2.shape[0]
    D = D_MODEL
    tm = min(512, T)
    fc = D_FF // 2
    const = lambda r, c: pl.BlockSpec((r, c), lambda i: (0, 0))
    return pl.pallas_call(
        functools.partial(_ffn_kernel, T // tm),
        grid=(M // tm,),
        in_specs=[pl.BlockSpec((tm, D), lambda i: (i, 0)), const(1, D), const(D, 2 * D_FF),
                  const(CONV_WIDTH, 2 * D_FF), const(1, 2 * D_FF), const(D_FF, D), const(1, D)],
        out_specs=pl.BlockSpec((tm, D), lambda i: (i, 0)),
        out_shape=jax.ShapeDtypeStruct((M, D), F32),
        scratch_shapes=[pltpu.VMEM((tm + SUBLANES, fc), F32),
                        pltpu.VMEM((2 * (D_FF // fc), SUBLANES, fc), F32)],
        compiler_params=_params("arbitrary"),
        name="conv_ffn",
    )(x2, lnpre, wup, cw, cb, wdn, lnpost)


def _split_w_in(w):
    widths = (256, 256, 512, 512, 256, 128, 64, 8, 512, 512, 512, 512, N_BRANCH * D_MODEL)
    offs = np.concatenate([[0], np.cumsum(widths)])
    rq, rk, rv, rg, cq, ckv, ik, iw, hq, hf, hv, hg, gt = [w[:, offs[k]:offs[k + 1]] for k in range(len(widths))]
    zeros = lambda n: jnp.zeros((w.shape[0], n), w.dtype)
    wb = jnp.concatenate([rq, rk, rv, cq, ckv, ik, zeros(2 * DSA_KV_RANK - DSA_KV_RANK - IDX_DIM), hq, hv], axis=1)
    wf = jnp.concatenate([rg, hf, hg, iw, zeros(LANES - IDX_HEADS)], axis=1)
    assert wb.shape[1] == ZB_W and wf.shape[1] == ZF_W
    return wb.astype(BF16), wf.astype(BF16), gt.astype(BF16)


def _dsa_mixer(zb, zf, rel_bias_tiles, q_norm, kv_norm, w_uq, w_uk, w_uv, B, T):
    H, dh, HI, dI, Rq = DSA_HEADS, DSA_DH, IDX_HEADS, IDX_DIM, DSA_Q_RANK
    row = lambda v: v.reshape(1, -1)
    wq = w_uq[:, :H * dh].reshape(Rq, H, dh).transpose(1, 0, 2).astype(BF16)
    wqi = w_uq[:, H * dh:].astype(BF16)
    qlt, qit, w_row, ckvn, ckvt, idxk = _dsa_prep(zb, zf, row(q_norm), row(kv_norm), wq, wqi,
                                                  w_uk.astype(BF16), B, T)
    wuvt = w_uv.transpose(0, 2, 1).astype(BF16)
    return _dsa_attn(qlt, qit, w_row, ckvn, ckvt, idxk, rel_bias_tiles, wuvt, B, T)


def kernel(x, rel_bias, hgrn_lb, ln_mix_pre, ln_mix_post, ln_ffn_pre, ln_ffn_post, w_in, dsa_q_norm, dsa_kv_norm, dsa_w_uq, dsa_w_uk, dsa_w_uv, hgrn_norm, w_br_ret, w_br_dsa, w_br_hgrn, w_out, ffn_w_up, ffn_conv_w, ffn_conv_b, ffn_w_down):
    B, T, D = x.shape
    depth = w_in.shape[0]
    assert T % KEY_BLOCK == 0 and D == D_MODEL
    bias_tiles = _rel_bias_tiles(rel_bias, min(Q_BLOCK, T), KEY_BLOCK)
    x2 = x.reshape(B * T, D)
    row = lambda v: v.reshape(1, -1)
    for l in range(depth):
        wb, wf, wg = _split_w_in(w_in[l])
        zb, zf = _inproj(x2, row(ln_mix_pre[l]), wb, wf)
        y_ret = _retention(zb, zf, B, T)
        y_hg = _hgrn(zb, zf, hgrn_lb, row(hgrn_norm[l]), l, B, T)
        y_dsa = _dsa_mixer(zb, zf, bias_tiles, dsa_q_norm[l], dsa_kv_norm[l], dsa_w_uq[l], dsa_w_uk[l],
                           dsa_w_uv[l], B, T)
        x2 = _merge(x2, y_ret, y_dsa, y_hg, row(ln_mix_pre[l]), wg, w_br_ret[l].astype(BF16),
                    w_br_dsa[l].astype(BF16), w_br_hgrn[l].astype(BF16), w_out[l].astype(BF16),
                    row(ln_mix_post[l]))
        x2 = _ffn(x2, row(ln_ffn_pre[l]), ffn_w_up[l].astype(BF16), ffn_conv_w[l], row(ffn_conv_b[l]),
                  ffn_w_down[l].astype(BF16), row(ln_ffn_post[l]), T)
    return x2.reshape(B, T, D)
```

```python
import functools
import math

import jax
import jax.numpy as jnp
import numpy as np
from jax import lax
from jax.experimental import pallas as pl
from jax.experimental.pallas import tpu as pltpu

F32 = jnp.float32
BF16 = jnp.bfloat16
I32 = jnp.int32
I16 = jnp.int16

D_MODEL = 1024
RET_HEADS, RET_DK, RET_DV, RET_CHUNK = 4, 64, 128, 128
DSA_HEADS, DSA_DH, DSA_Q_RANK, DSA_KV_RANK = 8, 64, 256, 128
IDX_HEADS, IDX_DIM, DSA_TOPK_MAX, Q_BLOCK = 8, 64, 256, 128
HGRN_HEADS, HGRN_EXPAND, HGRN_DV = 4, 128, 128
F_FLOOR = 1e-6
REL_BUCKETS, REL_MAX_DIST = 32, 128
D_FF = 2816
CONV_WIDTH = 3
N_BRANCH = 3
EPS = 1e-6
NEG_BIG = -1e30

RET_W = RET_HEADS * RET_DV
DSA_W = DSA_HEADS * DSA_DH
HGRN_KW = HGRN_HEADS * HGRN_EXPAND
HGRN_W = HGRN_HEADS * HGRN_DV

ZB_RQ, ZB_RK, ZB_RV, ZB_CQ = 0, 256, 512, 1024
ZB_CKV = 1280
ZB_HQ, ZB_HV = 1536, 2048
ZB_W = 2560
ZF_RG, ZF_HF, ZF_HG, ZF_IW = 0, 512, 1024, 1536
ZF_W = 1664

VMEM_LIMIT_BYTES = 56 * 1024 * 1024
SUBLANES = 8
LANES = 128

KEY_BLOCK = 256
HGRN_PAIR_CHUNK = 128
INT_MIN = -(2 ** 31)
INT16_MIN = -(2 ** 15)


def _params(*sem):
    return pltpu.CompilerParams(dimension_semantics=sem, vmem_limit_bytes=VMEM_LIMIT_BYTES)


def _dot(a, b):
    return jnp.dot(a, b, preferred_element_type=F32)


def _dot_nt(a, b):
    return lax.dot_general(a, b, (((1,), (1,)), ((), ())), preferred_element_type=F32)


def _dot_tn(a, b):
    return lax.dot_general(a, b, (((0,), (0,)), ((), ())), preferred_element_type=F32)


def _sigmoid(x):
    return 1.0 / (1.0 + jnp.exp(-x))


def _rms(x, g):
    return x * lax.rsqrt(jnp.mean(x * x, axis=-1, keepdims=True) + EPS) * g


def _group_sum(x):
    return jnp.sum(x.reshape(x.shape[0] // SUBLANES, SUBLANES, x.shape[1]), axis=0)


def _group_max(x):
    return jnp.max(x.reshape(x.shape[0] // SUBLANES, SUBLANES, x.shape[1]), axis=0)


def _inproj_kernel(x_ref, g_ref, wb_ref, wf_ref, zb_ref, zf_ref):
    h = _rms(x_ref[...], g_ref[...]).astype(BF16)
    zb_ref[...] = _dot(h, wb_ref[...]).astype(BF16)
    zf_ref[...] = _dot(h, wf_ref[...])


def _inproj(x2, g, wb, wf):
    M = x2.shape[0]
    tm = min(512, M)
    const = lambda r, c: pl.BlockSpec((r, c), lambda i: (0, 0))
    return pl.pallas_call(
        _inproj_kernel,
        grid=(M // tm,),
        in_specs=[pl.BlockSpec((tm, D_MODEL), lambda i: (i, 0)), const(1, D_MODEL),
                  const(D_MODEL, ZB_W), const(D_MODEL, ZF_W)],
        out_specs=[pl.BlockSpec((tm, ZB_W), lambda i: (i, 0)), pl.BlockSpec((tm, ZF_W), lambda i: (i, 0))],
        out_shape=[jax.ShapeDtypeStruct((M, ZB_W), BF16), jax.ShapeDtypeStruct((M, ZF_W), F32)],
        compiler_params=_params("parallel"),
        name="inproj",
    )(x2, g, wb, wf)


def _ret_kernel(gam_ref, q_ref, k_ref, v_ref, g_ref, cos_ref, sin_ref, dm_ref, xi_ref, zeta_ref,
                o_ref, r_ref):
    C = q_ref.shape[0]
    H, dk, dv = RET_HEADS, RET_DK, RET_DV

    @pl.when(pl.program_id(1) == 0)
    def _():
        r_ref[...] = jnp.zeros_like(r_ref)

    cos = cos_ref[...]
    sin = sin_ref[...]
    lane = lax.broadcasted_iota(I32, (C, H * dk), 1)
    first_half = (lane % dk) < (dk // 2)

    def rot(x):
        swapped = jnp.where(first_half, pltpu.roll(x, H * dk - dk // 2, 1), pltpu.roll(x, dk // 2, 1))
        return x * cos + swapped * sin

    q = rot(q_ref[...].astype(F32))
    k = rot(k_ref[...].astype(F32)) * dk ** -0.5
    for h in range(H):
        qh = q[:, h * dk:(h + 1) * dk].astype(BF16)
        kh = k[:, h * dk:(h + 1) * dk]
        vh = v_ref[:, h * dv:(h + 1) * dv]
        s = _dot_nt(qh, kh.astype(BF16)) * dm_ref[h]
        inner = _dot(s.astype(BF16), vh)
        rh = r_ref[h]
        cross = _dot(qh, rh.astype(BF16)) * xi_ref[h]
        r_ref[h] = gam_ref[h] * rh + _dot_tn((kh * zeta_ref[h]).astype(BF16), vh)
        o = inner + cross
        mu = jnp.mean(o, axis=-1, keepdims=True)
        oc = o - mu
        var = jnp.mean(oc * oc, axis=-1, keepdims=True)
        gh = g_ref[:, h * dv:(h + 1) * dv]
        o_ref[:, h * dv:(h + 1) * dv] = (gh * _sigmoid(gh) * (oc * lax.rsqrt(var + EPS))).astype(BF16)


def _retention(zb, zf, B, T):
    H, dk, dv = RET_HEADS, RET_DK, RET_DV
    C = min(RET_CHUNK, T)
    n = T // C
    pos = jnp.arange(T, dtype=F32)
    half = dk // 2
    freq = 1.0 / (10000.0 ** jnp.linspace(0.0, 1.0, half, dtype=F32))
    ang = pos[:, None] * freq[None, :]
    cos = jnp.tile(jnp.cos(ang), (1, 2 * H))
    sin = jnp.tile(jnp.concatenate([-jnp.sin(ang), jnp.sin(ang)], axis=1), (1, H))
    log_gamma = jnp.log1p(-(2.0 ** (-5.0 - jnp.arange(H, dtype=F32))))
    i = jnp.arange(C, dtype=F32)
    rel = i[:, None] - i[None, :]
    dmask = jnp.where(rel >= 0, jnp.exp(jnp.maximum(rel, 0.0)[None] * log_gamma[:, None, None]), 0.0)
    xi = jnp.exp((i + 1.0)[None, :] * log_gamma[:, None])[:, :, None]
    zeta = jnp.exp((C - 1.0 - i)[None, :] * log_gamma[:, None])[:, :, None]
    gamma_c = jnp.exp(C * log_gamma)

    wq = H * dk
    wv = H * dv
    full = lambda shape: pl.BlockSpec(shape, lambda b, c: (0,) * len(shape))
    return pl.pallas_call(
        _ret_kernel,
        grid=(B, n),
        in_specs=[pl.BlockSpec(memory_space=pltpu.SMEM),
                  pl.BlockSpec((C, wq), lambda b, c: (b * n + c, ZB_RQ // wq)),
                  pl.BlockSpec((C, wq), lambda b, c: (b * n + c, ZB_RK // wq)),
                  pl.BlockSpec((C, wv), lambda b, c: (b * n + c, ZB_RV // wv)),
                  pl.BlockSpec((C, wv), lambda b, c: (b * n + c, ZF_RG // wv)),
                  pl.BlockSpec((C, wq), lambda b, c: (c, 0)),
                  pl.BlockSpec((C, wq), lambda b, c: (c, 0)),
                  full((H, C, C)), full((H, C, 1)), full((H, C, 1))],
        out_specs=pl.BlockSpec((C, wv), lambda b, c: (b * n + c, 0)),
        out_shape=jax.ShapeDtypeStruct((B * T, wv), BF16),
        scratch_shapes=[pltpu.VMEM((H, dk, dv), F32)],
        compiler_params=_params("parallel", "arbitrary"),
        name="retention",
    )(gamma_c, zb, zb, zb, zf, cos, sin, dmask, xi, zeta)


def _hgrn_levels(C):
    ms, m = [], C // 2
    while m >= SUBLANES:
        ms.append(m)
        m //= 2
    return ms


def _hgrn_kernel(layer, q_ref, f_ref, v_ref, g_ref, lbraw_ref, gain_ref, tri_ref, lmask_ref, o_ref, st_ref):
    TR = q_ref.shape[0]
    C = tri_ref.shape[0]
    H, dk, dv = HGRN_HEADS, HGRN_EXPAND, HGRN_DV
    SB = SUBLANES

    @pl.when(pl.program_id(1) == 0)
    def _():
        st_ref[...] = jnp.zeros_like(st_ref)

    raw = lbraw_ref[...]
    e = jnp.exp(raw - jnp.max(raw, axis=0, keepdims=True))
    soft = e / jnp.sum(e, axis=0, keepdims=True)
    cs = soft[0:1]
    for l in range(1, layer + 1):
        cs = cs + soft[l:l + 1]
    lb = jnp.clip(cs - soft[0:1], 0.0, 1.0)
    tri = tri_ref[...]
    row_in_blk = lax.broadcasted_iota(I32, (SB, dk), 0)
    lane_t = lax.broadcasted_iota(I32, (SB, C), 1)
    levels = _hgrn_levels(C)

    def chunk(c, carry):
        r0 = pl.multiple_of(c * C, C)
        zf = f_ref[pl.ds(r0, C), :]
        f = lb + (1.0 - lb) * _sigmoid(zf)
        log_f = jnp.log(jnp.maximum(f, F_FLOOR))
        kk = (1.0 - lb) * _sigmoid(-zf)
        b = jnp.dot(tri, log_f, preferred_element_type=F32, precision=lax.Precision.HIGHEST)
        qq = q_ref[pl.ds(r0, C), :].astype(F32)
        vb = v_ref[pl.ds(r0, C), :]
        eb = jnp.exp(b)
        b_last = b[C - 1:C, :]
        eb_last = eb[C - 1:C, :]
        q_dec = (qq * eb).astype(BF16)
        k_dec = (kk * jnp.exp(b_last - b)).astype(BF16)
        outs = []
        for h in range(H):
            sl = slice(h * dk, (h + 1) * dk)
            bh, qh, kh, vh = b[:, sl], qq[:, sl], kk[:, sl], vb[:, h * dv:(h + 1) * dv]
            diag = []
            for blk in range(C // SB):
                bs = bh[blk * SB:(blk + 1) * SB, :]
                ks = kh[blk * SB:(blk + 1) * SB, :]
                at = jnp.zeros((SB, C), F32)
                for tt in range(SB):
                    t = blk * SB + tt
                    diff = jnp.where(row_in_blk <= tt, bh[t:t + 1, :] - bs, NEG_BIG)
                    p = (qh[t:t + 1, :] * jnp.exp(diff)) * ks
                    at = jnp.where(lane_t == t, jnp.sum(p, axis=-1, keepdims=True), at)
                diag.append(at)
            a_t = jnp.concatenate(diag, axis=0)
            for lev, m in enumerate(levels):
                qs, ks = [], []
                for blk in range(C // m):
                    rows = slice(blk * m, (blk + 1) * m)
                    if blk % 2 == 1:
                        ref = bh[blk * m - 1:blk * m, :]
                        qs.append(qh[rows] * jnp.exp(bh[rows] - ref))
                        ks.append(jnp.zeros((m, dk), F32))
                    else:
                        ref = bh[(blk + 1) * m - 1:(blk + 1) * m, :]
                        ks.append(kh[rows] * jnp.exp(ref - bh[rows]))
                        qs.append(jnp.zeros((m, dk), F32))
                q_l = jnp.concatenate(qs, axis=0).astype(BF16)
                k_l = jnp.concatenate(ks, axis=0).astype(BF16)
                a_t = a_t + _dot_nt(k_l, q_l) * lmask_ref[lev]
            intra = _dot_tn(a_t.astype(BF16), vh)
            st = st_ref[h]
            outs.append(intra + _dot_nt(q_dec[:, sl], st.astype(BF16)))
            st_ref[h] = st * eb_last[:, sl] + _dot_tn(vh, k_dec[:, sl])
        o = _rms(jnp.concatenate(outs, axis=1), gain_ref[...])
        gg = g_ref[pl.ds(r0, C), :]
        o_ref[pl.ds(r0, C), :] = (gg * _sigmoid(gg) * o).astype(BF16)
        return carry

    lax.fori_loop(0, TR // C, chunk, 0)


def _hgrn(zb, zf, hgrn_lb, gain, layer, B, T):
    H, dk, dv = HGRN_HEADS, HGRN_EXPAND, HGRN_DV
    C = min(HGRN_PAIR_CHUNK, T)
    TR = min(512, T)
    n = T // TR
    w = H * dk
    L = hgrn_lb.shape[0]
    tri = jnp.tril(jnp.ones((C, C), F32))
    idx = np.arange(C)
    lmask = np.stack([(((idx[None, :] // m) % 2 == 1) & (idx[:, None] // m == idx[None, :] // m - 1))
                      for m in _hgrn_levels(C)]).astype(np.float32)
    return pl.pallas_call(
        functools.partial(_hgrn_kernel, layer),
        grid=(B, n),
        in_specs=[pl.BlockSpec((TR, w), lambda b, c: (b * n + c, ZB_HQ // w)),
                  pl.BlockSpec((TR, w), lambda b, c: (b * n + c, ZF_HF // w)),
                  pl.BlockSpec((TR, w), lambda b, c: (b * n + c, ZB_HV // w)),
                  pl.BlockSpec((TR, w), lambda b, c: (b * n + c, ZF_HG // w)),
                  pl.BlockSpec((L, w), lambda b, c: (0, 0)),
                  pl.BlockSpec((1, w), lambda b, c: (0, 0)),
                  pl.BlockSpec((C, C), lambda b, c: (0, 0)),
                  pl.BlockSpec(lmask.shape, lambda b, c: (0, 0, 0))],
        out_specs=pl.BlockSpec((TR, w), lambda b, c: (b * n + c, 0)),
        out_shape=jax.ShapeDtypeStruct((B * T, w), BF16),
        scratch_shapes=[pltpu.VMEM((H, dv, dk), F32)],
        compiler_params=_params("parallel", "arbitrary"),
        name="hgrn2",
    )(zb, zf, zb, zf, hgrn_lb, gain, tri, jnp.asarray(lmask))


def _dsa_prep_kernel(cq_ref, kv_ref, iw_ref, qn_ref, kn_ref, wq_ref, wqi_ref, wuk_ref,
                     qlt_ref, qit_ref, w_ref, ckv_ref, ckvt_ref, ik_ref):
    tm = cq_ref.shape[0]
    H, dh, HI, dI, Dc = DSA_HEADS, DSA_DH, IDX_HEADS, IDX_DIM, DSA_KV_RANK
    QB = qlt_ref.shape[2] // H
    nb = tm // QB
    cq = _rms(cq_ref[...].astype(F32), qn_ref[...]).astype(BF16)
    ckv = _rms(kv_ref[:, :Dc].astype(F32), kn_ref[...])
    ckv_ref[...] = ckv.astype(BF16)
    ckvt_ref[0, 0] = ckv.T.astype(BF16)
    ik_ref[...] = kv_ref[:, Dc:Dc + dI]
    w_t = iw_ref[...].T[0:HI, :] * (HI * dI) ** -0.5
    qi_all = _dot(cq, wqi_ref[...])
    for j in range(nb):
        rows = slice(j * QB, (j + 1) * QB)
        qi_t = qi_all[rows].T
        for h in range(HI):
            qit_ref[j, :, h * QB:(h + 1) * QB] = qi_t[h * dI:(h + 1) * dI].astype(BF16)
            w_ref[j, :, h * QB:(h + 1) * QB] = w_t[h:h + 1, rows]
    for h in range(H):
        qh = _dot(cq, wq_ref[h])
        ql = _dot(qh.astype(BF16), wuk_ref[h]) * dh ** -0.5
        for j in range(nb):
            qlt_ref[j, :, h * QB:(h + 1) * QB] = ql[j * QB:(j + 1) * QB].T.astype(BF16)


def _dsa_prep(zb, zf, qn, kn, wq, wqi, wuk, B, T):
    M = B * T
    H, dh, HI, dI, Dc, Rq = DSA_HEADS, DSA_DH, IDX_HEADS, IDX_DIM, DSA_KV_RANK, DSA_Q_RANK
    QB = min(Q_BLOCK, T)
    KB = KEY_BLOCK
    tm = KB
    nb = tm // QB
    nt = T // tm
    return pl.pallas_call(
        _dsa_prep_kernel,
        grid=(M // tm,),
        in_specs=[pl.BlockSpec((tm, Rq), lambda i: (i, ZB_CQ // Rq)),
                  pl.BlockSpec((tm, 2 * Dc), lambda i: (i, ZB_CKV // (2 * Dc))),
                  pl.BlockSpec((tm, LANES), lambda i: (i, ZF_IW // LANES)),
                  pl.BlockSpec((1, Rq), lambda i: (0, 0)),
                  pl.BlockSpec((1, Dc), lambda i: (0, 0)),
                  pl.BlockSpec((H, Rq, dh), lambda i: (0, 0, 0)),
                  pl.BlockSpec((Rq, HI * dI), lambda i: (0, 0)),
                  pl.BlockSpec((H, dh, Dc), lambda i: (0, 0, 0))],
        out_specs=[pl.BlockSpec((nb, Dc, H * QB), lambda i: (i, 0, 0)),
                   pl.BlockSpec((nb, dI, HI * QB), lambda i: (i, 0, 0)),
                   pl.BlockSpec((nb, 1, HI * QB), lambda i: (i, 0, 0)),
                   pl.BlockSpec((tm, Dc), lambda i: (i, 0)),
                   pl.BlockSpec((1, 1, Dc, tm), lambda i: (i // nt, (i % nt) // 2, 0, i % 2)),
                   pl.BlockSpec((tm, dI), lambda i: (i, 0))],
        out_shape=[jax.ShapeDtypeStruct((M // QB, Dc, H * QB), BF16),
                   jax.ShapeDtypeStruct((M // QB, dI, HI * QB), BF16),
                   jax.ShapeDtypeStruct((M // QB, 1, HI * QB), F32),
                   jax.ShapeDtypeStruct((M, Dc), BF16),
                   jax.ShapeDtypeStruct((B, nt // 2, Dc, 2 * tm), BF16),
                   jax.ShapeDtypeStruct((M, dI), BF16)],
        compiler_params=_params("parallel"),
        name="dsa_prep",
    )(zb, zb, zf, qn, kn, wq, wqi, wuk)


def _sortable_key(s):
    s = jnp.where(s == 0.0, 0.0, s)
    bits = pltpu.bitcast(s, I32)
    return bits ^ ((bits >> 31) & 0x7FFFFFFF)


def _two_at_a_time(nkb, issue, finish, buf_a, buf_b):
    issue(0, buf_a)

    def pair(p, carry):
        issue(2 * p + 1, buf_b)
        finish(2 * p, buf_a, 0)
        issue(jnp.minimum(2 * p + 2, nkb - 1), buf_a)
        finish(2 * p + 1, buf_b, 1)
        return carry

    lax.fori_loop(0, nkb // 2, pair, 0)

    @pl.when(nkb % 2 == 1)
    def _():
        finish(nkb - 1, buf_a, 0)


def _slab_sum(x, rows):
    parts = [x[r:r + rows] for r in range(0, x.shape[0], rows)]
    while len(parts) > 1:
        parts = [parts[i] + parts[i + 1] for i in range(0, len(parts), 2)]
    return parts[0]


def _dsa_attn_kernel(topk, n_keys, neg_key, qlt_ref, qit_ref, w_ref, ckv_ref, ckvt_ref, ik_ref, bias_ref,
                     wuvt_ref, o_ref, key_ref, hi_ref, lo_ref, z_ref, raw_a, raw_b, acc_ref, m8_ref, l8_ref, tie_ref):
    H, HI = DSA_HEADS, IDX_HEADS
    QB = o_ref.shape[0]
    KB = key_ref.shape[1]
    PACK = 2 * SUBLANES
    RC = 64
    q0 = pl.program_id(1) * QB
    nkb = (q0 + QB - 1) // KB + 1
    npair = (nkb + 1) // 2
    n_skip = n_keys - nkb * KB

    s_row = lax.broadcasted_iota(I32, (KB, QB), 0)
    t_row = q0 + lax.broadcasted_iota(I32, (1, QB), 1)

    qit = qit_ref[0]
    w_row = w_ref[0]

    def issue_score(kb, raw):
        raw[...] = _dot(ik_ref[pl.ds(pl.multiple_of(kb * KB, KB), KB), :], qit)

    def finish_score(kb, raw, half):
        for r in range(0, KB, RC):
            s = None
            for h in range(HI):
                hs = slice(h * QB, (h + 1) * QB)
                term = jnp.maximum(raw[r:r + RC, hs], 0.0) * w_row[:, hs]
                s = term if s is None else s + term
            s = jnp.where(kb * KB + r + s_row[0:RC] <= t_row, s, NEG_BIG)
            key = _sortable_key(s)
            key_ref[kb, r:r + RC, :] = key
            rows = slice(half * KB + r, half * KB + r + RC)
            hi_ref[kb // 2, rows, :] = (key >> 16).astype(I16)
            lo_ref[kb // 2, rows, :] = ((key & 0xFFFF) + INT16_MIN).astype(I16)

    _two_at_a_time(nkb, issue_score, finish_score, raw_a, raw_b)

    @pl.when(nkb % 2 == 1)
    def _():
        hi_ref[nkb // 2, KB:2 * KB, :] = jnp.full((KB, QB), INT16_MIN, I16)
        lo_ref[nkb // 2, KB:2 * KB, :] = jnp.full((KB, QB), INT16_MIN, I16)

    def count16(ref, hit_fn):
        def body(p, acc):
            hit = jnp.where(hit_fn(ref[p]), jnp.int16(1), jnp.int16(0))
            return acc + _slab_sum(hit, PACK)

        acc = lax.fori_loop(0, npair, body, jnp.zeros((PACK, QB), I16))
        return jnp.sum(acc.astype(I32), axis=0, keepdims=True)

    neg_hi, neg_lo = neg_key >> 16, neg_key & 0xFFFF

    def count_ge_hi(cand):
        c16 = cand.astype(I16)
        return count16(hi_ref, lambda v: v >= c16) + jnp.where(cand <= neg_hi, n_skip, 0)

    zero = jnp.zeros((1, QB), I32)
    thr_hi = jnp.where(count_ge_hi(zero) >= topk, zero, jnp.full((1, QB), INT16_MIN, I32))

    def bisect_hi(it, thr_hi):
        cand = thr_hi | jnp.left_shift(jnp.int32(1), 14 - it)
        return jnp.where(count_ge_hi(cand) >= topk, cand, thr_hi)

    thr_hi = lax.fori_loop(0, 15, bisect_hi, thr_hi)
    t16 = thr_hi.astype(I16)
    n_above = count16(hi_ref, lambda v: v > t16) + jnp.where(thr_hi < neg_hi, n_skip, 0)

    def keep_equal_hi(p, carry):
        lo_ref[p] = jnp.where(hi_ref[p] == t16, lo_ref[p], jnp.int16(INT16_MIN))
        return carry

    lax.fori_loop(0, npair, keep_equal_hi, 0)
    skip_eq = thr_hi == neg_hi

    def bisect_lo(it, thr_lo):
        cand = thr_lo | jnp.left_shift(jnp.int32(1), 15 - it)
        c16 = (cand + INT16_MIN).astype(I16)
        cnt = (n_above + count16(lo_ref, lambda v: v >= c16)
               + jnp.where(skip_eq & (cand <= neg_lo), n_skip, 0))
        return jnp.where(cnt >= topk, cand, thr_lo)

    thr_lo = lax.fori_loop(0, 16, bisect_lo, zero)
    thr = jnp.left_shift(thr_hi, 16) | thr_lo

    def count(hit_fn):
        def body(kb, acc):
            return acc + _group_sum(jnp.where(hit_fn(kb, key_ref[kb]), 1, 0))

        acc = lax.fori_loop(0, nkb, body, jnp.zeros((SUBLANES, QB), I32))
        return jnp.sum(acc, axis=0, keepdims=True)

    n_gt = count(lambda kb, kk: kk > thr) + jnp.where(thr < neg_key, n_skip, 0)
    n_eq = count(lambda kb, kk: kk == thr)
    need = topk - n_gt
    nbits = max(1, (n_keys - 1).bit_length())
    tie_ref[...] = jnp.full(tie_ref.shape, (1 << nbits) - 1, I32)

    @pl.when(jnp.max(jnp.where(need < n_eq, 1, 0)) > 0)
    def _():
        def ibisect(it, p):
            cand = p | jnp.left_shift(jnp.int32(1), nbits - 1 - it)
            below = count(lambda kb, kk: (kk == thr) & (kb * KB + s_row < cand))
            return jnp.where(below < need, cand, p)

        p = lax.fori_loop(0, nbits, ibisect, jnp.zeros((1, QB), I32))
        tie_ref[...] = jnp.broadcast_to(p, tie_ref.shape)

    tie = tie_ref[0:1, :]

    qlt = qlt_ref[0]
    m8_ref[...] = jnp.full(m8_ref.shape, NEG_BIG, F32)

    def issue_logits(kb, raw):
        raw[...] = _dot(ckv_ref[pl.ds(pl.multiple_of(kb * KB, KB), KB), :], qlt)

    def finish_logits(kb, raw, half):
        tile = jnp.minimum((q0 - kb * KB) // QB, 3)
        tops = [None] * H
        for r in range(0, KB, RC):
            kk = key_ref[kb, r:r + RC, :]
            s_glob = kb * KB + r + s_row[0:RC]
            sel = ((kk > thr) | ((kk == thr) & (s_glob <= tie))) & (s_glob <= t_row)
            off = jnp.where(sel, 0.0, NEG_BIG)
            for h in range(H):
                hs = slice(h * QB, (h + 1) * QB)
                zh = (raw[r:r + RC, hs] + bias_ref[tile, r:r + RC, hs]) + off
                z_ref[kb // 2, half * KB + r:half * KB + r + RC, hs] = zh
                top = _group_max(zh)
                tops[h] = top if tops[h] is None else jnp.maximum(tops[h], top)
        m8_ref[...] = jnp.maximum(m8_ref[...], jnp.concatenate(tops, axis=1))

    _two_at_a_time(nkb, issue_logits, finish_logits, raw_a, raw_b)
    m = jnp.max(m8_ref[...], axis=0, keepdims=True)
    acc_ref[...] = jnp.zeros(acc_ref.shape, F32)
    l8_ref[...] = jnp.zeros(l8_ref.shape, F32)

    def pv(z, ckv_t):
        p = jnp.exp(z - m)
        acc_ref[...] += _dot(ckv_t, p.astype(BF16))
        l8_ref[...] += _group_sum(p)

    def pv_pair(p, carry):
        pv(z_ref[p], ckvt_ref[0, p])
        return carry

    lax.fori_loop(0, nkb // 2, pv_pair, 0)

    @pl.when(nkb % 2 == 1)
    def _():
        pv(z_ref[nkb // 2, 0:KB, :], ckvt_ref[0, nkb // 2, :, 0:KB])

    inv_l = 1.0 / jnp.sum(l8_ref[...], axis=0, keepdims=True)
    o_lat_t = (acc_ref[...] * inv_l).astype(BF16)
    y_t = jnp.concatenate([_dot(wuvt_ref[h], o_lat_t[:, h * QB:(h + 1) * QB]) for h in range(H)], axis=0)
    o_ref[...] = y_t.T.astype(BF16)


def _dsa_attn(qlt, qit, w_row, ckvn, ckvt, idxk, bias_tiles, wuvt, B, T):
    H, HI, dI, Dc, dh = DSA_HEADS, IDX_HEADS, IDX_DIM, DSA_KV_RANK, DSA_DH
    QB = min(Q_BLOCK, T)
    KB = KEY_BLOCK
    nq = T // QB
    nk = T // KB
    topk = min(DSA_TOPK_MAX, T // 4)
    neg_key = int(np.array(NEG_BIG, np.float32).view(np.int32))
    neg_key = neg_key ^ ((neg_key >> 31) & 0x7FFFFFFF)
    return pl.pallas_call(
        functools.partial(_dsa_attn_kernel, topk, T, neg_key),
        grid=(B, nq),
        in_specs=[pl.BlockSpec((1, Dc, H * QB), lambda b, i: (b * nq + i, 0, 0)),
                  pl.BlockSpec((1, dI, HI * QB), lambda b, i: (b * nq + i, 0, 0)),
                  pl.BlockSpec((1, 1, HI * QB), lambda b, i: (b * nq + i, 0, 0)),
                  pl.BlockSpec((T, Dc), lambda b, i: (b, 0)),
                  pl.BlockSpec((1, nk // 2, Dc, 2 * KB), lambda b, i: (b, 0, 0, 0)),
                  pl.BlockSpec((T, dI), lambda b, i: (b, 0)),
                  pl.BlockSpec((4, KB, H * QB), lambda b, i: (0, 0, 0)),
                  pl.BlockSpec((H, dh, Dc), lambda b, i: (0, 0, 0))],
        out_specs=pl.BlockSpec((QB, DSA_W), lambda b, i: (b * nq + i, 0)),
        out_shape=jax.ShapeDtypeStruct((B * T, DSA_W), BF16),
        scratch_shapes=[pltpu.VMEM((nk, KB, QB), I32),
                        pltpu.VMEM((nk // 2, 2 * KB, QB), I16),
                        pltpu.VMEM((nk // 2, 2 * KB, QB), I16),
                        pltpu.VMEM((nk // 2, 2 * KB, H * QB), F32),
                        pltpu.VMEM((KB, H * QB), F32),
                        pltpu.VMEM((KB, H * QB), F32),
                        pltpu.VMEM((Dc, H * QB), F32),
                        pltpu.VMEM((SUBLANES, H * QB), F32),
                        pltpu.VMEM((SUBLANES, H * QB), F32),
                        pltpu.VMEM((SUBLANES, QB), I32)],
        compiler_params=_params("parallel", "arbitrary"),
        name="dsa_attn",
    )(qlt, qit, w_row, ckvn, ckvt, idxk, bias_tiles, wuvt)


def _rel_bias_tiles(rel_bias, QB, KB):
    max_exact = REL_BUCKETS // 2
    n_far = 3 * QB
    assert n_far - (KB - 1) > REL_MAX_DIST
    n = jnp.arange(n_far, dtype=jnp.int32)
    nf = jnp.maximum(n, max_exact).astype(F32)
    large = max_exact + (jnp.log(nf / max_exact) / math.log(REL_MAX_DIST / max_exact)
                         * (REL_BUCKETS - max_exact)).astype(jnp.int32)
    large = jnp.minimum(large, REL_BUCKETS - 1)
    bucket = jnp.where(n < max_exact, n, large)
    H = rel_bias.shape[1]
    tab = jnp.take(rel_bias, bucket, axis=0).T
    P = KB + QB
    tiles = []
    for delta in (0, QB, 2 * QB):
        d = np.concatenate([np.arange(QB + 1), np.arange(-(KB - 1), 0)])
        g = tab[:, np.clip(delta + d, 0, n_far - 1)]
        rows = jnp.tile(g, (1, KB))[:, :KB * (P - 1)].reshape(H, KB, P - 1)[:, :, :QB]
        tiles.append(rows.transpose(1, 0, 2).reshape(KB, H * QB))
    tiles.append(jnp.broadcast_to(jnp.repeat(rel_bias[REL_BUCKETS - 1], QB)[None, :], (KB, H * QB)))
    return jnp.stack(tiles).astype(F32)


def _merge_kernel(x_ref, yr_ref, yd_ref, yh_ref, lnpre_ref, wg_ref, wr_ref, wd_ref, wh_ref, wo_ref,
                  ln_ref, o_ref):
    D = x_ref.shape[1]
    x = x_ref[...]
    h = _rms(x, lnpre_ref[...]).astype(BF16)
    m = None
    for k, (y_ref, w_ref) in enumerate(((yr_ref, wr_ref), (yd_ref, wd_ref), (yh_ref, wh_ref))):
        gate = _sigmoid(_dot(h, wg_ref[:, k * D:(k + 1) * D]))
        term = gate * _dot(y_ref[...], w_ref[...])
        m = term if m is None else m + term
    u = _dot(m.astype(BF16), wo_ref[...])
    o_ref[...] = x + _rms(u, ln_ref[...])


def _merge(x2, y_ret, y_dsa, y_hg, lnpre, wg, wr, wd, wh, wo, ln):
    M = x2.shape[0]
    tm = min(512, M)
    D = D_MODEL
    row = lambda w: pl.BlockSpec((tm, w), lambda i: (i, 0))
    const = lambda r, c: pl.BlockSpec((r, c), lambda i: (0, 0))
    return pl.pallas_call(
        _merge_kernel,
        grid=(M // tm,),
        in_specs=[row(D), row(RET_W), row(DSA_W), row(HGRN_W), const(1, D), const(D, N_BRANCH * D),
                  const(RET_W, D), const(DSA_W, D), const(HGRN_W, D), const(D, D), const(1, D)],
        out_specs=row(D),
        out_shape=jax.ShapeDtypeStruct((M, D), F32),
        compiler_params=_params("parallel"),
        name="merge",
    )(x2, y_ret, y_dsa, y_hg, lnpre, wg, wr, wd, wh, wo, ln)


def _gelu_tanh(x):
    return 0.5 * x * (1.0 + jnp.tanh(math.sqrt(2.0 / math.pi) * (x + 0.044715 * (x * x * x))))


def _ffn_kernel(tiles_per_seq, x_ref, lnpre_ref, wup_ref, cw_ref, cb_ref, wdn_ref, lnpost_ref,
                o_ref, buf_ref, prev_ref):
    tm = x_ref.shape[0]
    HALO = SUBLANES
    fc = buf_ref.shape[1]
    n_pass = D_FF // fc
    first = (pl.program_id(0) % tiles_per_seq) == 0

    @pl.when(first)
    def _():
        prev_ref[...] = jnp.zeros_like(prev_ref)

    x = x_ref[...]
    h = _rms(x, lnpre_ref[...]).astype(BF16)

    def conv(part, c):
        col = part * D_FF + c * fc
        up = _dot(h, wup_ref[:, col:col + fc])
        slot = part * n_pass + c
        buf_ref[0:HALO, :] = prev_ref[slot]
        buf_ref[HALO:HALO + tm, :] = up
        prev_ref[slot] = up[tm - HALO:tm, :]
        w = cw_ref[:, col:col + fc]
        y = (up * w[2:3] + buf_ref[HALO - 1:HALO - 1 + tm, :] * w[1:2]
             + buf_ref[HALO - 2:HALO - 2 + tm, :] * w[0:1])
        return y + cb_ref[:, col:col + fc]

    acc = None
    for c in range(n_pass):
        a = conv(0, c)
        u = conv(1, c)
        act = (_gelu_tanh(a) * u).astype(BF16)
        d = _dot(act, wdn_ref[c * fc:(c + 1) * fc, :])
        acc = d if acc is None else acc + d
    o_ref[...] = x + _rms(acc, lnpost_ref[...])


def _ffn(x2, lnpre, wup, cw, cb, wdn, lnpost, T):
    M = x2.shape[0]
    D = D_MODEL
    tm = min(512, T)
    fc = D_FF // 2
    const = lambda r, c: pl.BlockSpec((r, c), lambda i: (0, 0))
    return pl.pallas_call(
        functools.partial(_ffn_kernel, T // tm),
        grid=(M // tm,),
        in_specs=[pl.BlockSpec((tm, D), lambda i: (i, 0)), const(1, D), const(D, 2 * D_FF),
                  const(CONV_WIDTH, 2 * D_FF), const(1, 2 * D_FF), const(D_FF, D), const(1, D)],
        out_specs=pl.BlockSpec((tm, D), lambda i: (i, 0)),
        out_shape=jax.ShapeDtypeStruct((M, D), F32),
        scratch_shapes=[pltpu.VMEM((tm + SUBLANES, fc), F32),
                        pltpu.VMEM((2 * (D_FF // fc), SUBLANES, fc), F32)],
        compiler_params=_params("arbitrary"),
        name="conv_ffn",
    )(x2, lnpre, wup, cw, cb, wdn, lnpost)


def _split_w_in(w):
    widths = (256, 256, 512, 512, 256, 128, 64, 8, 512, 512, 512, 512, N_BRANCH * D_MODEL)
    offs = np.concatenate([[0], np.cumsum(widths)])
    rq, rk, rv, rg, cq, ckv, ik, iw, hq, hf, hv, hg, gt = [w[:, offs[k]:offs[k + 1]] for k in range(len(widths))]
    zeros = lambda n: jnp.zeros((w.shape[0], n), w.dtype)
    wb = jnp.concatenate([rq, rk, rv, cq, ckv, ik, zeros(2 * DSA_KV_RANK - DSA_KV_RANK - IDX_DIM), hq, hv], axis=1)
    wf = jnp.concatenate([rg, hf, hg, iw, zeros(LANES - IDX_HEADS)], axis=1)
    assert wb.shape[1] == ZB_W and wf.shape[1] == ZF_W
    return wb.astype(BF16), wf.astype(BF16), gt.astype(BF16)


def _dsa_mixer(zb, zf, rel_bias_tiles, q_norm, kv_norm, w_uq, w_uk, w_uv, B, T):
    H, dh, HI, dI, Rq = DSA_HEADS, DSA_DH, IDX_HEADS, IDX_DIM, DSA_Q_RANK
    row = lambda v: v.reshape(1, -1)
    wq = w_uq[:, :H * dh].reshape(Rq, H, dh).transpose(1, 0, 2).astype(BF16)
    wqi = w_uq[:, H * dh:].astype(BF16)
    qlt, qit, w_row, ckvn, ckvt, idxk = _dsa_prep(zb, zf, row(q_norm), row(kv_norm), wq, wqi,
                                                  w_uk.astype(BF16), B, T)
    wuvt = w_uv.transpose(0, 2, 1).astype(BF16)
    return _dsa_attn(qlt, qit, w_row, ckvn, ckvt, idxk, rel_bias_tiles, wuvt, B, T)


def kernel(x, rel_bias, hgrn_lb, ln_mix_pre, ln_mix_post, ln_ffn_pre, ln_ffn_post, w_in, dsa_q_norm, dsa_kv_norm, dsa_w_uq, dsa_w_uk, dsa_w_uv, hgrn_norm, w_br_ret, w_br_dsa, w_br_hgrn, w_out, ffn_w_up, ffn_conv_w, ffn_conv_b, ffn_w_down):
    B, T, D = x.shape
    depth = w_in.shape[0]
    assert T % (2 * KEY_BLOCK) == 0 and D == D_MODEL
    bias_tiles = _rel_bias_tiles(rel_bias, min(Q_BLOCK, T), KEY_BLOCK)
    x2 = x.reshape(B * T, D)
    row = lambda v: v.reshape(1, -1)
    for l in range(depth):
        wb, wf, wg = _split_w_in(w_in[l])
        zb, zf = _inproj(x2, row(ln_mix_pre[l]), wb, wf)
        y_ret = _retention(zb, zf, B, T)
        y_hg = _hgrn(zb, zf, hgrn_lb, row(hgrn_norm[l]), l, B, T)
        y_dsa = _dsa_mixer(zb, zf, bias_tiles, dsa_q_norm[l], dsa_kv_norm[l], dsa_w_uq[l], dsa_w_uk[l],
                           dsa_w_uv[l], B, T)
        x2 = _merge(x2, y_ret, y_dsa, y_hg, row(ln_mix_pre[l]), wg, w_br_ret[l].astype(BF16),
                    w_br_dsa[l].astype(BF16), w_br_hgrn[l].astype(BF16), w_out[l].astype(BF16),
                    row(ln_mix_post[l]))
        x2 = _ffn(x2, row(ln_ffn_pre[l]), ffn_w_up[l].astype(BF16), ffn_conv_w[l], row(ffn_conv_b[l]),
                  ffn_w_down[l].astype(BF16), row(ln_ffn_post[l]), T)
    return x2.reshape(B, T, D)
```

```python
import functools
import math

import jax
import jax.numpy as jnp
import numpy as np
from jax import lax
from jax.experimental import pallas as pl
from jax.experimental.pallas import tpu as pltpu

F32 = jnp.float32
BF16 = jnp.bfloat16
I32 = jnp.int32
I16 = jnp.int16

D_MODEL = 1024
RET_HEADS, RET_DK, RET_DV, RET_CHUNK = 4, 64, 128, 128
DSA_HEADS, DSA_DH, DSA_Q_RANK, DSA_KV_RANK = 8, 64, 256, 128
IDX_HEADS, IDX_DIM, DSA_TOPK_MAX, Q_BLOCK = 8, 64, 256, 128
HGRN_HEADS, HGRN_EXPAND, HGRN_DV = 4, 128, 128
F_FLOOR = 1e-6
REL_BUCKETS, REL_MAX_DIST = 32, 128
D_FF = 2816
CONV_WIDTH = 3
N_BRANCH = 3
EPS = 1e-6
NEG_BIG = -1e30

RET_W = RET_HEADS * RET_DV
DSA_W = DSA_HEADS * DSA_DH
HGRN_KW = HGRN_HEADS * HGRN_EXPAND
HGRN_W = HGRN_HEADS * HGRN_DV

ZB_RQ, ZB_RK, ZB_RV, ZB_CQ = 0, 256, 512, 1024
ZB_CKV = 1280
ZB_HQ, ZB_HV = 1536, 2048
ZB_W = 2560
ZF_RG, ZF_HF, ZF_HG, ZF_IW = 0, 512, 1024, 1536
ZF_W = 1664

VMEM_LIMIT_BYTES = 56 * 1024 * 1024
SUBLANES = 8
LANES = 128

KEY_BLOCK = 256
HGRN_PAIR_CHUNK = 128
INT_MIN = -(2 ** 31)
INT16_MIN = -(2 ** 15)


def _params(*sem):
    return pltpu.CompilerParams(dimension_semantics=sem, vmem_limit_bytes=VMEM_LIMIT_BYTES)


def _dot(a, b):
    return jnp.dot(a, b, preferred_element_type=F32)


def _dot_nt(a, b):
    return lax.dot_general(a, b, (((1,), (1,)), ((), ())), preferred_element_type=F32)


def _dot_tn(a, b):
    return lax.dot_general(a, b, (((0,), (0,)), ((), ())), preferred_element_type=F32)


def _sigmoid(x):
    return 1.0 / (1.0 + jnp.exp(-x))


def _rms(x, g):
    return x * lax.rsqrt(jnp.mean(x * x, axis=-1, keepdims=True) + EPS) * g


def _group_sum(x):
    return jnp.sum(x.reshape(x.shape[0] // SUBLANES, SUBLANES, x.shape[1]), axis=0)


def _group_max(x):
    return jnp.max(x.reshape(x.shape[0] // SUBLANES, SUBLANES, x.shape[1]), axis=0)


def _inproj_kernel(x_ref, g_ref, wb_ref, wf_ref, zb_ref, zf_ref):
    h = _rms(x_ref[...], g_ref[...]).astype(BF16)
    zb_ref[...] = _dot(h, wb_ref[...]).astype(BF16)
    zf_ref[...] = _dot(h, wf_ref[...])


def _inproj(x2, g, wb, wf):
    M = x2.shape[0]
    tm = min(512, M)
    const = lambda r, c: pl.BlockSpec((r, c), lambda i: (0, 0))
    return pl.pallas_call(
        _inproj_kernel,
        grid=(M // tm,),
        in_specs=[pl.BlockSpec((tm, D_MODEL), lambda i: (i, 0)), const(1, D_MODEL),
                  const(D_MODEL, ZB_W), const(D_MODEL, ZF_W)],
        out_specs=[pl.BlockSpec((tm, ZB_W), lambda i: (i, 0)), pl.BlockSpec((tm, ZF_W), lambda i: (i, 0))],
        out_shape=[jax.ShapeDtypeStruct((M, ZB_W), BF16), jax.ShapeDtypeStruct((M, ZF_W), F32)],
        compiler_params=_params("parallel"),
        name="inproj",
    )(x2, g, wb, wf)


def _ret_kernel(gam_ref, q_ref, k_ref, v_ref, g_ref, cos_ref, sin_ref, dm_ref, xi_ref, zeta_ref,
                o_ref, r_ref):
    C = q_ref.shape[0]
    H, dk, dv = RET_HEADS, RET_DK, RET_DV

    @pl.when(pl.program_id(1) == 0)
    def _():
        r_ref[...] = jnp.zeros_like(r_ref)

    cos = cos_ref[...]
    sin = sin_ref[...]
    lane = lax.broadcasted_iota(I32, (C, H * dk), 1)
    first_half = (lane % dk) < (dk // 2)

    def rot(x):
        swapped = jnp.where(first_half, pltpu.roll(x, H * dk - dk // 2, 1), pltpu.roll(x, dk // 2, 1))
        return x * cos + swapped * sin

    q = rot(q_ref[...].astype(F32))
    k = rot(k_ref[...].astype(F32)) * dk ** -0.5
    for h in range(H):
        qh = q[:, h * dk:(h + 1) * dk].astype(BF16)
        kh = k[:, h * dk:(h + 1) * dk]
        vh = v_ref[:, h * dv:(h + 1) * dv]
        s = _dot_nt(qh, kh.astype(BF16)) * dm_ref[h]
        inner = _dot(s.astype(BF16), vh)
        rh = r_ref[h]
        cross = _dot(qh, rh.astype(BF16)) * xi_ref[h]
        r_ref[h] = gam_ref[h] * rh + _dot_tn((kh * zeta_ref[h]).astype(BF16), vh)
        o = inner + cross
        mu = jnp.mean(o, axis=-1, keepdims=True)
        oc = o - mu
        var = jnp.mean(oc * oc, axis=-1, keepdims=True)
        gh = g_ref[:, h * dv:(h + 1) * dv]
        o_ref[:, h * dv:(h + 1) * dv] = (gh * _sigmoid(gh) * (oc * lax.rsqrt(var + EPS))).astype(BF16)


def _retention(zb, zf, B, T):
    H, dk, dv = RET_HEADS, RET_DK, RET_DV
    C = min(RET_CHUNK, T)
    n = T // C
    pos = jnp.arange(T, dtype=F32)
    half = dk // 2
    freq = 1.0 / (10000.0 ** jnp.linspace(0.0, 1.0, half, dtype=F32))
    ang = pos[:, None] * freq[None, :]
    cos = jnp.tile(jnp.cos(ang), (1, 2 * H))
    sin = jnp.tile(jnp.concatenate([-jnp.sin(ang), jnp.sin(ang)], axis=1), (1, H))
    log_gamma = jnp.log1p(-(2.0 ** (-5.0 - jnp.arange(H, dtype=F32))))
    i = jnp.arange(C, dtype=F32)
    rel = i[:, None] - i[None, :]
    dmask = jnp.where(rel >= 0, jnp.exp(jnp.maximum(rel, 0.0)[None] * log_gamma[:, None, None]), 0.0)
    xi = jnp.exp((i + 1.0)[None, :] * log_gamma[:, None])[:, :, None]
    zeta = jnp.exp((C - 1.0 - i)[None, :] * log_gamma[:, None])[:, :, None]
    gamma_c = jnp.exp(C * log_gamma)

    wq = H * dk
    wv = H * dv
    full = lambda shape: pl.BlockSpec(shape, lambda b, c: (0,) * len(shape))
    return pl.pallas_call(
        _ret_kernel,
        grid=(B, n),
        in_specs=[pl.BlockSpec(memory_space=pltpu.SMEM),
                  pl.BlockSpec((C, wq), lambda b, c: (b * n + c, ZB_RQ // wq)),
                  pl.BlockSpec((C, wq), lambda b, c: (b * n + c, ZB_RK // wq)),
                  pl.BlockSpec((C, wv), lambda b, c: (b * n + c, ZB_RV // wv)),
                  pl.BlockSpec((C, wv), lambda b, c: (b * n + c, ZF_RG // wv)),
                  pl.BlockSpec((C, wq), lambda b, c: (c, 0)),
                  pl.BlockSpec((C, wq), lambda b, c: (c, 0)),
                  full((H, C, C)), full((H, C, 1)), full((H, C, 1))],
        out_specs=pl.BlockSpec((C, wv), lambda b, c: (b * n + c, 0)),
        out_shape=jax.ShapeDtypeStruct((B * T, wv), BF16),
        scratch_shapes=[pltpu.VMEM((H, dk, dv), F32)],
        compiler_params=_params("parallel", "arbitrary"),
        name="retention",
    )(gamma_c, zb, zb, zb, zf, cos, sin, dmask, xi, zeta)


def _hgrn_levels(C):
    ms, m = [], C // 2
    while m >= SUBLANES:
        ms.append(m)
        m //= 2
    return ms


def _hgrn_kernel(layer, q_ref, f_ref, v_ref, g_ref, lbraw_ref, gain_ref, tri_ref, lmask_ref, o_ref, st_ref):
    TR = q_ref.shape[0]
    C = tri_ref.shape[0]
    H, dk, dv = HGRN_HEADS, HGRN_EXPAND, HGRN_DV
    SB = SUBLANES

    @pl.when(pl.program_id(1) == 0)
    def _():
        st_ref[...] = jnp.zeros_like(st_ref)

    raw = lbraw_ref[...]
    e = jnp.exp(raw - jnp.max(raw, axis=0, keepdims=True))
    soft = e / jnp.sum(e, axis=0, keepdims=True)
    cs = soft[0:1]
    for l in range(1, layer + 1):
        cs = cs + soft[l:l + 1]
    lb = jnp.clip(cs - soft[0:1], 0.0, 1.0)
    tri = tri_ref[...]
    row_in_blk = lax.broadcasted_iota(I32, (SB, dk), 0)
    lane_t = lax.broadcasted_iota(I32, (SB, C), 1)
    levels = _hgrn_levels(C)

    def chunk(c, carry):
        r0 = pl.multiple_of(c * C, C)
        zf = f_ref[pl.ds(r0, C), :]
        f = lb + (1.0 - lb) * _sigmoid(zf)
        log_f = jnp.log(jnp.maximum(f, F_FLOOR))
        kk = (1.0 - lb) * _sigmoid(-zf)
        b = jnp.dot(tri, log_f, preferred_element_type=F32, precision=lax.Precision.HIGHEST)
        qq = q_ref[pl.ds(r0, C), :].astype(F32)
        vb = v_ref[pl.ds(r0, C), :]
        eb = jnp.exp(b)
        b_last = b[C - 1:C, :]
        eb_last = eb[C - 1:C, :]
        q_dec = (qq * eb).astype(BF16)
        k_dec = (kk * jnp.exp(b_last - b)).astype(BF16)
        outs = []
        for h in range(H):
            sl = slice(h * dk, (h + 1) * dk)
            bh, qh, kh, vh = b[:, sl], qq[:, sl], kk[:, sl], vb[:, h * dv:(h + 1) * dv]
            diag = []
            for blk in range(C // SB):
                bs = bh[blk * SB:(blk + 1) * SB, :]
                ks = kh[blk * SB:(blk + 1) * SB, :]
                at = jnp.zeros((SB, C), F32)
                for tt in range(SB):
                    t = blk * SB + tt
                    diff = jnp.where(row_in_blk <= tt, bh[t:t + 1, :] - bs, NEG_BIG)
                    p = (qh[t:t + 1, :] * jnp.exp(diff)) * ks
                    at = jnp.where(lane_t == t, jnp.sum(p, axis=-1, keepdims=True), at)
                diag.append(at)
            a_t = jnp.concatenate(diag, axis=0)
            for lev, m in enumerate(levels):
                qs, ks = [], []
                for blk in range(C // m):
                    rows = slice(blk * m, (blk + 1) * m)
                    if blk % 2 == 1:
                        ref = bh[blk * m - 1:blk * m, :]
                        qs.append(qh[rows] * jnp.exp(bh[rows] - ref))
                        ks.append(jnp.zeros((m, dk), F32))
                    else:
                        ref = bh[(blk + 1) * m - 1:(blk + 1) * m, :]
                        ks.append(kh[rows] * jnp.exp(ref - bh[rows]))
                        qs.append(jnp.zeros((m, dk), F32))
                q_l = jnp.concatenate(qs, axis=0).astype(BF16)
                k_l = jnp.concatenate(ks, axis=0).astype(BF16)
                a_t = a_t + _dot_nt(k_l, q_l) * lmask_ref[lev]
            intra = _dot_tn(a_t.astype(BF16), vh)
            st = st_ref[h]
            outs.append(intra + _dot_nt(q_dec[:, sl], st.astype(BF16)))
            st_ref[h] = st * eb_last[:, sl] + _dot_tn(vh, k_dec[:, sl])
        o = _rms(jnp.concatenate(outs, axis=1), gain_ref[...])
        gg = g_ref[pl.ds(r0, C), :]
        o_ref[pl.ds(r0, C), :] = (gg * _sigmoid(gg) * o).astype(BF16)
        return carry

    lax.fori_loop(0, TR // C, chunk, 0)


def _hgrn(zb, zf, hgrn_lb, gain, layer, B, T):
    H, dk, dv = HGRN_HEADS, HGRN_EXPAND, HGRN_DV
    C = min(HGRN_PAIR_CHUNK, T)
    TR = min(512, T)
    n = T // TR
    w = H * dk
    L = hgrn_lb.shape[0]
    tri = jnp.tril(jnp.ones((C, C), F32))
    idx = np.arange(C)
    lmask = np.stack([(((idx[None, :] // m) % 2 == 1) & (idx[:, None] // m == idx[None, :] // m - 1))
                      for m in _hgrn_levels(C)]).astype(np.float32)
    return pl.pallas_call(
        functools.partial(_hgrn_kernel, layer),
        grid=(B, n),
        in_specs=[pl.BlockSpec((TR, w), lambda b, c: (b * n + c, ZB_HQ // w)),
                  pl.BlockSpec((TR, w), lambda b, c: (b * n + c, ZF_HF // w)),
                  pl.BlockSpec((TR, w), lambda b, c: (b * n + c, ZB_HV // w)),
                  pl.BlockSpec((TR, w), lambda b, c: (b * n + c, ZF_HG // w)),
                  pl.BlockSpec((L, w), lambda b, c: (0, 0)),
                  pl.BlockSpec((1, w), lambda b, c: (0, 0)),
                  pl.BlockSpec((C, C), lambda b, c: (0, 0)),
                  pl.BlockSpec(lmask.shape, lambda b, c: (0, 0, 0))],
        out_specs=pl.BlockSpec((TR, w), lambda b, c: (b * n + c, 0)),
        out_shape=jax.ShapeDtypeStruct((B * T, w), BF16),
        scratch_shapes=[pltpu.VMEM((H, dv, dk), F32)],
        compiler_params=_params("parallel", "arbitrary"),
        name="hgrn2",
    )(zb, zf, zb, zf, hgrn_lb, gain, tri, jnp.asarray(lmask))


def _dsa_prep_kernel(cq_ref, kv_ref, iw_ref, qn_ref, kn_ref, wq_ref, wqi_ref, wuk_ref,
                     qlt_ref, qit_ref, w_ref, ckv_ref, ckvt_ref, ik_ref):
    tm = cq_ref.shape[0]
    H, dh, HI, dI, Dc = DSA_HEADS, DSA_DH, IDX_HEADS, IDX_DIM, DSA_KV_RANK
    QB = qlt_ref.shape[2] // H
    nb = tm // QB
    cq = _rms(cq_ref[...].astype(F32), qn_ref[...]).astype(BF16)
    ckv = _rms(kv_ref[:, :Dc].astype(F32), kn_ref[...])
    ckv_ref[...] = ckv.astype(BF16)
    ckvt_ref[0, 0] = ckv.T.astype(BF16)
    ik_ref[...] = kv_ref[:, Dc:Dc + dI]
    w_t = iw_ref[...].T[0:HI, :] * (HI * dI) ** -0.5
    qi_all = _dot(cq, wqi_ref[...])
    for j in range(nb):
        rows = slice(j * QB, (j + 1) * QB)
        qi_t = qi_all[rows].T
        for h in range(HI):
            qit_ref[j, :, h * QB:(h + 1) * QB] = qi_t[h * dI:(h + 1) * dI].astype(BF16)
            w_ref[j, :, h * QB:(h + 1) * QB] = w_t[h:h + 1, rows]
    for h in range(H):
        qh = _dot(cq, wq_ref[h])
        ql = _dot(qh.astype(BF16), wuk_ref[h]) * dh ** -0.5
        for j in range(nb):
            qlt_ref[j, :, h * QB:(h + 1) * QB] = ql[j * QB:(j + 1) * QB].T.astype(BF16)


def _dsa_prep(zb, zf, qn, kn, wq, wqi, wuk, B, T):
    M = B * T
    H, dh, HI, dI, Dc, Rq = DSA_HEADS, DSA_DH, IDX_HEADS, IDX_DIM, DSA_KV_RANK, DSA_Q_RANK
    QB = min(Q_BLOCK, T)
    KB = KEY_BLOCK
    tm = KB
    nb = tm // QB
    nt = T // tm
    return pl.pallas_call(
        _dsa_prep_kernel,
        grid=(M // tm,),
        in_specs=[pl.BlockSpec((tm, Rq), lambda i: (i, ZB_CQ // Rq)),
                  pl.BlockSpec((tm, 2 * Dc), lambda i: (i, ZB_CKV // (2 * Dc))),
                  pl.BlockSpec((tm, LANES), lambda i: (i, ZF_IW // LANES)),
                  pl.BlockSpec((1, Rq), lambda i: (0, 0)),
                  pl.BlockSpec((1, Dc), lambda i: (0, 0)),
                  pl.BlockSpec((H, Rq, dh), lambda i: (0, 0, 0)),
                  pl.BlockSpec((Rq, HI * dI), lambda i: (0, 0)),
                  pl.BlockSpec((H, dh, Dc), lambda i: (0, 0, 0))],
        out_specs=[pl.BlockSpec((nb, Dc, H * QB), lambda i: (i, 0, 0)),
                   pl.BlockSpec((nb, dI, HI * QB), lambda i: (i, 0, 0)),
                   pl.BlockSpec((nb, 1, HI * QB), lambda i: (i, 0, 0)),
                   pl.BlockSpec((tm, Dc), lambda i: (i, 0)),
                   pl.BlockSpec((1, 1, Dc, tm), lambda i: (i // nt, (i % nt) // 2, 0, i % 2)),
                   pl.BlockSpec((tm, dI), lambda i: (i, 0))],
        out_shape=[jax.ShapeDtypeStruct((M // QB, Dc, H * QB), BF16),
                   jax.ShapeDtypeStruct((M // QB, dI, HI * QB), BF16),
                   jax.ShapeDtypeStruct((M // QB, 1, HI * QB), F32),
                   jax.ShapeDtypeStruct((M, Dc), BF16),
                   jax.ShapeDtypeStruct((B, nt // 2, Dc, 2 * tm), BF16),
                   jax.ShapeDtypeStruct((M, dI), BF16)],
        compiler_params=_params("parallel"),
        name="dsa_prep",
    )(zb, zb, zf, qn, kn, wq, wqi, wuk)


def _sortable_key(s):
    s = jnp.where(s == 0.0, 0.0, s)
    bits = pltpu.bitcast(s, I32)
    return bits ^ ((bits >> 31) & 0x7FFFFFFF)


def _slab_sum(x, rows):
    parts = [x[r:r + rows] for r in range(0, x.shape[0], rows)]
    while len(parts) > 1:
        parts = [parts[i] + parts[i + 1] for i in range(0, len(parts), 2)]
    return parts[0]


def _dsa_attn_kernel(topk, n_keys, neg_key, qlt_ref, qit_ref, w_ref, ckv_ref, ckvt_ref, ik_ref, bias_ref,
                     wuvt_ref, o_ref, key_ref, hi_ref, lo_ref, z_ref, acc_ref, m8_ref, l8_ref, tie_ref):
    H, HI = DSA_HEADS, IDX_HEADS
    QB = o_ref.shape[0]
    KB = key_ref.shape[1]
    PACK = 2 * SUBLANES
    RC = 64
    q0 = pl.program_id(1) * QB
    nkb = (q0 + QB - 1) // KB + 1
    npair = (nkb + 1) // 2
    n_skip = n_keys - nkb * KB

    s_row = lax.broadcasted_iota(I32, (KB, QB), 0)
    t_row = q0 + lax.broadcasted_iota(I32, (1, QB), 1)

    qit = qit_ref[0]
    w_row = w_ref[0]

    def half_rows(kb, r):
        return pl.ds(pl.multiple_of((kb % 2) * KB + r, RC), RC)

    def score_blk(kb, carry):
        raw = _dot(ik_ref[pl.ds(pl.multiple_of(kb * KB, KB), KB), :], qit)
        for r in range(0, KB, RC):
            s = None
            for h in range(HI):
                hs = slice(h * QB, (h + 1) * QB)
                term = jnp.maximum(raw[r:r + RC, hs], 0.0) * w_row[:, hs]
                s = term if s is None else s + term
            s = jnp.where(kb * KB + r + s_row[0:RC] <= t_row, s, NEG_BIG)
            key = _sortable_key(s)
            key_ref[kb, r:r + RC, :] = key
            hi_ref[kb // 2, half_rows(kb, r), :] = (key >> 16).astype(I16)
            lo_ref[kb // 2, half_rows(kb, r), :] = ((key & 0xFFFF) + INT16_MIN).astype(I16)
        return carry

    lax.fori_loop(0, nkb, score_blk, 0)

    @pl.when(nkb % 2 == 1)
    def _():
        hi_ref[nkb // 2, KB:2 * KB, :] = jnp.full((KB, QB), INT16_MIN, I16)
        lo_ref[nkb // 2, KB:2 * KB, :] = jnp.full((KB, QB), INT16_MIN, I16)

    def count16(ref, hit_fn):
        def body(p, acc):
            hit = jnp.where(hit_fn(ref[p]), jnp.int16(1), jnp.int16(0))
            return acc + _slab_sum(hit, PACK)

        acc = lax.fori_loop(0, npair, body, jnp.zeros((PACK, QB), I16))
        return jnp.sum(acc.astype(I32), axis=0, keepdims=True)

    neg_hi, neg_lo = neg_key >> 16, neg_key & 0xFFFF

    def count_ge_hi(cand):
        c16 = cand.astype(I16)
        return count16(hi_ref, lambda v: v >= c16) + jnp.where(cand <= neg_hi, n_skip, 0)

    zero = jnp.zeros((1, QB), I32)
    thr_hi = jnp.where(count_ge_hi(zero) >= topk, zero, jnp.full((1, QB), INT16_MIN, I32))

    def bisect_hi(it, thr_hi):
        cand = thr_hi | jnp.left_shift(jnp.int32(1), 14 - it)
        return jnp.where(count_ge_hi(cand) >= topk, cand, thr_hi)

    thr_hi = lax.fori_loop(0, 15, bisect_hi, thr_hi)
    t16 = thr_hi.astype(I16)
    n_above = count16(hi_ref, lambda v: v > t16) + jnp.where(thr_hi < neg_hi, n_skip, 0)

    def keep_equal_hi(p, carry):
        lo_ref[p] = jnp.where(hi_ref[p] == t16, lo_ref[p], jnp.int16(INT16_MIN))
        return carry

    lax.fori_loop(0, npair, keep_equal_hi, 0)
    skip_eq = thr_hi == neg_hi

    def bisect_lo(it, thr_lo):
        cand = thr_lo | jnp.left_shift(jnp.int32(1), 15 - it)
        c16 = (cand + INT16_MIN).astype(I16)
        cnt = (n_above + count16(lo_ref, lambda v: v >= c16)
               + jnp.where(skip_eq & (cand <= neg_lo), n_skip, 0))
        return jnp.where(cnt >= topk, cand, thr_lo)

    thr_lo = lax.fori_loop(0, 16, bisect_lo, zero)
    thr = jnp.left_shift(thr_hi, 16) | thr_lo

    def count(hit_fn):
        def body(kb, acc):
            return acc + _group_sum(jnp.where(hit_fn(kb, key_ref[kb]), 1, 0))

        acc = lax.fori_loop(0, nkb, body, jnp.zeros((SUBLANES, QB), I32))
        return jnp.sum(acc, axis=0, keepdims=True)

    n_gt = count(lambda kb, kk: kk > thr) + jnp.where(thr < neg_key, n_skip, 0)
    n_eq = count(lambda kb, kk: kk == thr)
    need = topk - n_gt
    nbits = max(1, (n_keys - 1).bit_length())
    tie_ref[...] = jnp.full(tie_ref.shape, (1 << nbits) - 1, I32)

    @pl.when(jnp.max(jnp.where(need < n_eq, 1, 0)) > 0)
    def _():
        def ibisect(it, p):
            cand = p | jnp.left_shift(jnp.int32(1), nbits - 1 - it)
            below = count(lambda kb, kk: (kk == thr) & (kb * KB + s_row < cand))
            return jnp.where(below < need, cand, p)

        p = lax.fori_loop(0, nbits, ibisect, jnp.zeros((1, QB), I32))
        tie_ref[...] = jnp.broadcast_to(p, tie_ref.shape)

    tie = tie_ref[0:1, :]

    qlt = qlt_ref[0]
    m8_ref[...] = jnp.full(m8_ref.shape, NEG_BIG, F32)

    def logits_blk(kb, carry):
        raw = _dot(ckv_ref[pl.ds(pl.multiple_of(kb * KB, KB), KB), :], qlt)
        tile = jnp.minimum((q0 - kb * KB) // QB, 3)
        tops = [None] * H
        for r in range(0, KB, RC):
            kk = key_ref[kb, r:r + RC, :]
            s_glob = kb * KB + r + s_row[0:RC]
            sel = ((kk > thr) | ((kk == thr) & (s_glob <= tie))) & (s_glob <= t_row)
            off = jnp.where(sel, 0.0, NEG_BIG)
            for h in range(H):
                hs = slice(h * QB, (h + 1) * QB)
                zh = (raw[r:r + RC, hs] + bias_ref[tile, r:r + RC, hs]) + off
                z_ref[kb // 2, half_rows(kb, r), hs] = zh
                top = _group_max(zh)
                tops[h] = top if tops[h] is None else jnp.maximum(tops[h], top)
        m8_ref[...] = jnp.maximum(m8_ref[...], jnp.concatenate(tops, axis=1))
        return carry

    lax.fori_loop(0, nkb, logits_blk, 0)
    m = jnp.max(m8_ref[...], axis=0, keepdims=True)
    acc_ref[...] = jnp.zeros(acc_ref.shape, F32)
    l8_ref[...] = jnp.zeros(l8_ref.shape, F32)

    def pv(z, ckv_t):
        p = jnp.exp(z - m)
        acc_ref[...] += _dot(ckv_t, p.astype(BF16))
        l8_ref[...] += _group_sum(p)

    def pv_pair(p, carry):
        pv(z_ref[p], ckvt_ref[0, p])
        return carry

    lax.fori_loop(0, nkb // 2, pv_pair, 0)

    @pl.when(nkb % 2 == 1)
    def _():
        pv(z_ref[nkb // 2, 0:KB, :], ckvt_ref[0, nkb // 2, :, 0:KB])

    inv_l = 1.0 / jnp.sum(l8_ref[...], axis=0, keepdims=True)
    o_lat_t = (acc_ref[...] * inv_l).astype(BF16)
    y_t = jnp.concatenate([_dot(wuvt_ref[h], o_lat_t[:, h * QB:(h + 1) * QB]) for h in range(H)], axis=0)
    o_ref[...] = y_t.T.astype(BF16)


def _dsa_attn(qlt, qit, w_row, ckvn, ckvt, idxk, bias_tiles, wuvt, B, T):
    H, HI, dI, Dc, dh = DSA_HEADS, IDX_HEADS, IDX_DIM, DSA_KV_RANK, DSA_DH
    QB = min(Q_BLOCK, T)
    KB = KEY_BLOCK
    nq = T // QB
    nk = T // KB
    topk = min(DSA_TOPK_MAX, T // 4)
    neg_key = int(np.array(NEG_BIG, np.float32).view(np.int32))
    neg_key = neg_key ^ ((neg_key >> 31) & 0x7FFFFFFF)
    return pl.pallas_call(
        functools.partial(_dsa_attn_kernel, topk, T, neg_key),
        grid=(B, nq),
        in_specs=[pl.BlockSpec((1, Dc, H * QB), lambda b, i: (b * nq + i, 0, 0)),
                  pl.BlockSpec((1, dI, HI * QB), lambda b, i: (b * nq + i, 0, 0)),
                  pl.BlockSpec((1, 1, HI * QB), lambda b, i: (b * nq + i, 0, 0)),
                  pl.BlockSpec((T, Dc), lambda b, i: (b, 0)),
                  pl.BlockSpec((1, nk // 2, Dc, 2 * KB), lambda b, i: (b, 0, 0, 0)),
                  pl.BlockSpec((T, dI), lambda b, i: (b, 0)),
                  pl.BlockSpec((4, KB, H * QB), lambda b, i: (0, 0, 0)),
                  pl.BlockSpec((H, dh, Dc), lambda b, i: (0, 0, 0))],
        out_specs=pl.BlockSpec((QB, DSA_W), lambda b, i: (b * nq + i, 0)),
        out_shape=jax.ShapeDtypeStruct((B * T, DSA_W), BF16),
        scratch_shapes=[pltpu.VMEM((nk, KB, QB), I32),
                        pltpu.VMEM((nk // 2, 2 * KB, QB), I16),
                        pltpu.VMEM((nk // 2, 2 * KB, QB), I16),
                        pltpu.VMEM((nk // 2, 2 * KB, H * QB), F32),
                        pltpu.VMEM((Dc, H * QB), F32),
                        pltpu.VMEM((SUBLANES, H * QB), F32),
                        pltpu.VMEM((SUBLANES, H * QB), F32),
                        pltpu.VMEM((SUBLANES, QB), I32)],
        compiler_params=_params("parallel", "arbitrary"),
        name="dsa_attn",
    )(qlt, qit, w_row, ckvn, ckvt, idxk, bias_tiles, wuvt)


def _rel_bias_tiles(rel_bias, QB, KB):
    max_exact = REL_BUCKETS // 2
    n_far = 3 * QB
    assert n_far - (KB - 1) > REL_MAX_DIST
    n = jnp.arange(n_far, dtype=jnp.int32)
    nf = jnp.maximum(n, max_exact).astype(F32)
    large = max_exact + (jnp.log(nf / max_exact) / math.log(REL_MAX_DIST / max_exact)
                         * (REL_BUCKETS - max_exact)).astype(jnp.int32)
    large = jnp.minimum(large, REL_BUCKETS - 1)
    bucket = jnp.where(n < max_exact, n, large)
    H = rel_bias.shape[1]
    tab = jnp.take(rel_bias, bucket, axis=0).T
    P = KB + QB
    tiles = []
    for delta in (0, QB, 2 * QB):
        d = np.concatenate([np.arange(QB + 1), np.arange(-(KB - 1), 0)])
        g = tab[:, np.clip(delta + d, 0, n_far - 1)]
        rows = jnp.tile(g, (1, KB))[:, :KB * (P - 1)].reshape(H, KB, P - 1)[:, :, :QB]
        tiles.append(rows.transpose(1, 0, 2).reshape(KB, H * QB))
    tiles.append(jnp.broadcast_to(jnp.repeat(rel_bias[REL_BUCKETS - 1], QB)[None, :], (KB, H * QB)))
    return jnp.stack(tiles).astype(F32)


def _merge_kernel(x_ref, yr_ref, yd_ref, yh_ref, lnpre_ref, wg_ref, wr_ref, wd_ref, wh_ref, wo_ref,
                  ln_ref, o_ref):
    D = x_ref.shape[1]
    x = x_ref[...]
    h = _rms(x, lnpre_ref[...]).astype(BF16)
    m = None
    for k, (y_ref, w_ref) in enumerate(((yr_ref, wr_ref), (yd_ref, wd_ref), (yh_ref, wh_ref))):
        gate = _sigmoid(_dot(h, wg_ref[:, k * D:(k + 1) * D]))
        term = gate * _dot(y_ref[...], w_ref[...])
        m = term if m is None else m + term
    u = _dot(m.astype(BF16), wo_ref[...])
    o_ref[...] = x + _rms(u, ln_ref[...])


def _merge(x2, y_ret, y_dsa, y_hg, lnpre, wg, wr, wd, wh, wo, ln):
    M = x2.shape[0]
    tm = min(512, M)
    D = D_MODEL
    row = lambda w: pl.BlockSpec((tm, w), lambda i: (i, 0))
    const = lambda r, c: pl.BlockSpec((r, c), lambda i: (0, 0))
    return pl.pallas_call(
        _merge_kernel,
        grid=(M // tm,),
        in_specs=[row(D), row(RET_W), row(DSA_W), row(HGRN_W), const(1, D), const(D, N_BRANCH * D),
                  const(RET_W, D), const(DSA_W, D), const(HGRN_W, D), const(D, D), const(1, D)],
        out_specs=row(D),
        out_shape=jax.ShapeDtypeStruct((M, D), F32),
        compiler_params=_params("parallel"),
        name="merge",
    )(x2, y_ret, y_dsa, y_hg, lnpre, wg, wr, wd, wh, wo, ln)


def _gelu_tanh(x):
    return 0.5 * x * (1.0 + jnp.tanh(math.sqrt(2.0 / math.pi) * (x + 0.044715 * (x * x * x))))


def _ffn_kernel(tiles_per_seq, x_ref, lnpre_ref, wup_ref, cw_ref, cb_ref, wdn_ref, lnpost_ref,
                o_ref, buf_ref, prev_ref):
    tm = x_ref.shape[0]
    HALO = SUBLANES
    fc = buf_ref.shape[1]
    n_pass = D_FF // fc
    first = (pl.program_id(0) % tiles_per_seq) == 0

    @pl.when(first)
    def _():
        prev_ref[...] = jnp.zeros_like(prev_ref)

    x = x_ref[...]
    h = _rms(x, lnpre_ref[...]).astype(BF16)

    def conv(part, c):
        col = part * D_FF + c * fc
        up = _dot(h, wup_ref[:, col:col + fc])
        slot = part * n_pass + c
        buf_ref[0:HALO, :] = prev_ref[slot]
        buf_ref[HALO:HALO + tm, :] = up
        prev_ref[slot] = up[tm - HALO:tm, :]
        w = cw_ref[:, col:col + fc]
        y = (up * w[2:3] + buf_ref[HALO - 1:HALO - 1 + tm, :] * w[1:2]
             + buf_ref[HALO - 2:HALO - 2 + tm, :] * w[0:1])
        return y + cb_ref[:, col:col + fc]

    acc = None
    for c in range(n_pass):
        a = conv(0, c)
        u = conv(1, c)
        act = (_gelu_tanh(a) * u).astype(BF16)
        d = _dot(act, wdn_ref[c * fc:(c + 1) * fc, :])
        acc = d if acc is None else acc + d
    o_ref[...] = x + _rms(acc, lnpost_ref[...])


def _ffn(x2, lnpre, wup, cw, cb, wdn, lnpost, T):
    M = x2.shape[0]
    D = D_MODEL
    tm = min(512, T)
    fc = D_FF // 2
    const = lambda r, c: pl.BlockSpec((r, c), lambda i: (0, 0))
    return pl.pallas_call(
        functools.partial(_ffn_kernel, T // tm),
        grid=(M // tm,),
        in_specs=[pl.BlockSpec((tm, D), lambda i: (i, 0)), const(1, D), const(D, 2 * D_FF),
                  const(CONV_WIDTH, 2 * D_FF), const(1, 2 * D_FF), const(D_FF, D), const(1, D)],
        out_specs=pl.BlockSpec((tm, D), lambda i: (i, 0)),
        out_shape=jax.ShapeDtypeStruct((M, D), F32),
        scratch_shapes=[pltpu.VMEM((tm + SUBLANES, fc), F32),
                        pltpu.VMEM((2 * (D_FF // fc), SUBLANES, fc), F32)],
        compiler_params=_params("arbitrary"),
        name="conv_ffn",
    )(x2, lnpre, wup, cw, cb, wdn, lnpost)


def _split_w_in(w):
    widths = (256, 256, 512, 512, 256, 128, 64, 8, 512, 512, 512, 512, N_BRANCH * D_MODEL)
    offs = np.concatenate([[0], np.cumsum(widths)])
    rq, rk, rv, rg, cq, ckv, ik, iw, hq, hf, hv, hg, gt = [w[:, offs[k]:offs[k + 1]] for k in range(len(widths))]
    zeros = lambda n: jnp.zeros((w.shape[0], n), w.dtype)
    wb = jnp.concatenate([rq, rk, rv, cq, ckv, ik, zeros(2 * DSA_KV_RANK - DSA_KV_RANK - IDX_DIM), hq, hv], axis=1)
    wf = jnp.concatenate([rg, hf, hg, iw, zeros(LANES - IDX_HEADS)], axis=1)
    assert wb.shape[1] == ZB_W and wf.shape[1] == ZF_W
    return wb.astype(BF16), wf.astype(BF16), gt.astype(BF16)


def _dsa_mixer(zb, zf, rel_bias_tiles, q_norm, kv_norm, w_uq, w_uk, w_uv, B, T):
    H, dh, HI, dI, Rq = DSA_HEADS, DSA_DH, IDX_HEADS, IDX_DIM, DSA_Q_RANK
    row = lambda v: v.reshape(1, -1)
    wq = w_uq[:, :H * dh].reshape(Rq, H, dh).transpose(1, 0, 2).astype(BF16)
    wqi = w_uq[:, H * dh:].astype(BF16)
    qlt, qit, w_row, ckvn, ckvt, idxk = _dsa_prep(zb, zf, row(q_norm), row(kv_norm), wq, wqi,
                                                  w_uk.astype(BF16), B, T)
    wuvt = w_uv.transpose(0, 2, 1).astype(BF16)
    return _dsa_attn(qlt, qit, w_row, ckvn, ckvt, idxk, rel_bias_tiles, wuvt, B, T)


def kernel(x, rel_bias, hgrn_lb, ln_mix_pre, ln_mix_post, ln_ffn_pre, ln_ffn_post, w_in, dsa_q_norm, dsa_kv_norm, dsa_w_uq, dsa_w_uk, dsa_w_uv, hgrn_norm, w_br_ret, w_br_dsa, w_br_hgrn, w_out, ffn_w_up, ffn_conv_w, ffn_conv_b, ffn_w_down):
    B, T, D = x.shape
    depth = w_in.shape[0]
    assert T % (2 * KEY_BLOCK) == 0 and D == D_MODEL
    bias_tiles = _rel_bias_tiles(rel_bias, min(Q_BLOCK, T), KEY_BLOCK)
    x2 = x.reshape(B * T, D)
    row = lambda v: v.reshape(1, -1)
    for l in range(depth):
        wb, wf, wg = _split_w_in(w_in[l])
        zb, zf = _inproj(x2, row(ln_mix_pre[l]), wb, wf)
        y_ret = _retention(zb, zf, B, T)
        y_hg = _hgrn(zb, zf, hgrn_lb, row(hgrn_norm[l]), l, B, T)
        y_dsa = _dsa_mixer(zb, zf, bias_tiles, dsa_q_norm[l], dsa_kv_norm[l], dsa_w_uq[l], dsa_w_uk[l],
                           dsa_w_uv[l], B, T)
        x2 = _merge(x2, y_ret, y_dsa, y_hg, row(ln_mix_pre[l]), wg, w_br_ret[l].astype(BF16),
                    w_br_dsa[l].astype(BF16), w_br_hgrn[l].astype(BF16), w_out[l].astype(BF16),
                    row(ln_mix_post[l]))
        x2 = _ffn(x2, row(ln_ffn_pre[l]), ffn_w_up[l].astype(BF16), ffn_conv_w[l], row(ffn_conv_b[l]),
                  ffn_w_down[l].astype(BF16), row(ln_ffn_post[l]), T)
    return x2.reshape(B, T, D)
```

```python
import functools
import math

import jax
import jax.numpy as jnp
import numpy as np
from jax import lax
from jax.experimental import pallas as pl
from jax.experimental.pallas import tpu as pltpu

F32 = jnp.float32
BF16 = jnp.bfloat16
I32 = jnp.int32

D_MODEL = 1024
RET_HEADS, RET_DK, RET_DV, RET_CHUNK = 4, 64, 128, 128
DSA_HEADS, DSA_DH, DSA_Q_RANK, DSA_KV_RANK = 8, 64, 256, 128
IDX_HEADS, IDX_DIM, DSA_TOPK_MAX, Q_BLOCK = 8, 64, 256, 256
HGRN_HEADS, HGRN_EXPAND, HGRN_DV = 4, 128, 128
F_FLOOR = 1e-6
REL_BUCKETS, REL_MAX_DIST = 32, 128
D_FF = 2816
CONV_WIDTH = 3
N_BRANCH = 3
EPS = 1e-6
NEG_BIG = -1e30

RET_W = RET_HEADS * RET_DV
DSA_W = DSA_HEADS * DSA_DH
HGRN_KW = HGRN_HEADS * HGRN_EXPAND
HGRN_W = HGRN_HEADS * HGRN_DV

ZB_RQ, ZB_RK, ZB_RV, ZB_CQ = 0, 256, 512, 1024
ZB_CKV = 1280
ZB_HQ, ZB_HV = 1536, 2048
ZB_W = 2560
ZF_RG, ZF_HF, ZF_HG, ZF_IW = 0, 512, 1024, 1536
ZF_W = 1664

VMEM_LIMIT_BYTES = 56 * 1024 * 1024
SUBLANES = 8
LANES = 128

KEY_BLOCK = 256
DSA_PREP_ROWS = 512
HGRN_PAIR_CHUNK = 128
INT_MIN = -(2 ** 31)


def _params(*sem):
    return pltpu.CompilerParams(dimension_semantics=sem, vmem_limit_bytes=VMEM_LIMIT_BYTES)


def _dot(a, b):
    return jnp.dot(a, b, preferred_element_type=F32)


def _dot_nt(a, b):
    return lax.dot_general(a, b, (((1,), (1,)), ((), ())), preferred_element_type=F32)


def _dot_tn(a, b):
    return lax.dot_general(a, b, (((0,), (0,)), ((), ())), preferred_element_type=F32)


def _sigmoid(x):
    return 1.0 / (1.0 + jnp.exp(-x))


def _sigmoid_pair(x):
    t = jnp.exp(-jnp.abs(x))
    big = 1.0 / (1.0 + t)
    small = t * big
    pos = x >= 0.0
    return jnp.where(pos, big, small), jnp.where(pos, small, big)


def _rms(x, g):
    return x * lax.rsqrt(jnp.mean(x * x, axis=-1, keepdims=True) + EPS) * g


def _group_sum(x):
    return jnp.sum(x.reshape(x.shape[0] // SUBLANES, SUBLANES, x.shape[1]), axis=0)


def _group_max(x):
    return jnp.max(x.reshape(x.shape[0] // SUBLANES, SUBLANES, x.shape[1]), axis=0)


def _inproj_kernel(x_ref, g_ref, wb_ref, wf_ref, zb_ref, zf_ref):
    h = _rms(x_ref[...], g_ref[...]).astype(BF16)
    zb_ref[...] = _dot(h, wb_ref[...]).astype(BF16)
    zf_ref[...] = _dot(h, wf_ref[...])


def _inproj(x2, g, wb, wf):
    M = x2.shape[0]
    tm = min(1024, M)
    const = lambda r, c: pl.BlockSpec((r, c), lambda i: (0, 0))
    return pl.pallas_call(
        _inproj_kernel,
        grid=(M // tm,),
        in_specs=[pl.BlockSpec((tm, D_MODEL), lambda i: (i, 0)), const(1, D_MODEL),
                  const(D_MODEL, ZB_W), const(D_MODEL, ZF_W)],
        out_specs=[pl.BlockSpec((tm, ZB_W), lambda i: (i, 0)), pl.BlockSpec((tm, ZF_W), lambda i: (i, 0))],
        out_shape=[jax.ShapeDtypeStruct((M, ZB_W), BF16), jax.ShapeDtypeStruct((M, ZF_W), F32)],
        compiler_params=_params("parallel"),
        name="inproj",
    )(x2, g, wb, wf)


def _ret_kernel(gam_ref, q_ref, k_ref, v_ref, g_ref, cos_ref, sin_ref, dm_ref, xi_ref, zeta_ref,
                o_ref, r_ref):
    C = q_ref.shape[0]
    H, dk, dv = RET_HEADS, RET_DK, RET_DV

    @pl.when(pl.program_id(1) == 0)
    def _():
        r_ref[...] = jnp.zeros_like(r_ref)

    cos = cos_ref[...]
    sin = sin_ref[...]
    lane = lax.broadcasted_iota(I32, (C, H * dk), 1)
    first_half = (lane % dk) < (dk // 2)

    def rot(x):
        swapped = jnp.where(first_half, pltpu.roll(x, H * dk - dk // 2, 1), pltpu.roll(x, dk // 2, 1))
        return x * cos + swapped * sin

    q = rot(q_ref[...].astype(F32))
    k = rot(k_ref[...].astype(F32)) * dk ** -0.5
    for h in range(H):
        qh = q[:, h * dk:(h + 1) * dk].astype(BF16)
        kh = k[:, h * dk:(h + 1) * dk]
        vh = v_ref[:, h * dv:(h + 1) * dv]
        s = _dot_nt(qh, kh.astype(BF16)) * dm_ref[h]
        inner = _dot(s.astype(BF16), vh)
        rh = r_ref[h]
        cross = _dot(qh, rh.astype(BF16)) * xi_ref[h]
        r_ref[h] = gam_ref[h] * rh + _dot_tn((kh * zeta_ref[h]).astype(BF16), vh)
        o = inner + cross
        mu = jnp.mean(o, axis=-1, keepdims=True)
        oc = o - mu
        var = jnp.mean(oc * oc, axis=-1, keepdims=True)
        gh = g_ref[:, h * dv:(h + 1) * dv]
        o_ref[:, h * dv:(h + 1) * dv] = (gh * _sigmoid(gh) * (oc * lax.rsqrt(var + EPS))).astype(BF16)


def _retention(zb, zf, B, T):
    H, dk, dv = RET_HEADS, RET_DK, RET_DV
    C = min(RET_CHUNK, T)
    n = T // C
    pos = np.arange(T, dtype=np.float64)
    half = dk // 2
    freq = 1.0 / (10000.0 ** np.linspace(0.0, 1.0, half))
    ang = pos[:, None] * freq[None, :]
    cos = jnp.asarray(np.tile(np.cos(ang), (1, 2 * H)), F32)
    sin = jnp.asarray(np.tile(np.concatenate([-np.sin(ang), np.sin(ang)], axis=1), (1, H)), F32)
    log_gamma = np.log1p(-(2.0 ** (-5.0 - np.arange(H, dtype=np.float64))))
    i = np.arange(C, dtype=np.float64)
    rel = i[:, None] - i[None, :]
    dmask = jnp.asarray(np.where(rel >= 0, np.exp(np.maximum(rel, 0.0)[None] * log_gamma[:, None, None]), 0.0), F32)
    xi = jnp.asarray(np.exp((i + 1.0)[None, :] * log_gamma[:, None])[:, :, None], F32)
    zeta = jnp.asarray(np.exp((C - 1.0 - i)[None, :] * log_gamma[:, None])[:, :, None], F32)
    gamma_c = jnp.asarray(np.exp(C * log_gamma), F32)

    wq = H * dk
    wv = H * dv
    full = lambda shape: pl.BlockSpec(shape, lambda b, c: (0,) * len(shape))
    return pl.pallas_call(
        _ret_kernel,
        grid=(B, n),
        in_specs=[pl.BlockSpec(memory_space=pltpu.SMEM),
                  pl.BlockSpec((C, wq), lambda b, c: (b * n + c, ZB_RQ // wq)),
                  pl.BlockSpec((C, wq), lambda b, c: (b * n + c, ZB_RK // wq)),
                  pl.BlockSpec((C, wv), lambda b, c: (b * n + c, ZB_RV // wv)),
                  pl.BlockSpec((C, wv), lambda b, c: (b * n + c, ZF_RG // wv)),
                  pl.BlockSpec((C, wq), lambda b, c: (c, 0)),
                  pl.BlockSpec((C, wq), lambda b, c: (c, 0)),
                  full((H, C, C)), full((H, C, 1)), full((H, C, 1))],
        out_specs=pl.BlockSpec((C, wv), lambda b, c: (b * n + c, 0)),
        out_shape=jax.ShapeDtypeStruct((B * T, wv), BF16),
        scratch_shapes=[pltpu.VMEM((H, dk, dv), F32)],
        compiler_params=_params("parallel", "arbitrary"),
        name="retention",
    )(gamma_c, zb, zb, zb, zf, cos, sin, dmask, xi, zeta)


def _hgrn_levels(C):
    ms, m = [], C // 2
    while m >= SUBLANES:
        ms.append(m)
        m //= 2
    return ms


def _cumsum_rows(tri, x):
    hi = x.astype(BF16)
    rest = x - hi.astype(F32)
    mid = rest.astype(BF16)
    lo = (rest - mid.astype(F32)).astype(BF16)
    return _dot(tri, hi) + (_dot(tri, mid) + _dot(tri, lo))


def _hgrn_kernel(layer, q_ref, f_ref, v_ref, g_ref, lbraw_ref, gain_ref, tri_ref, lmask_ref, o_ref, st_ref):
    TR = q_ref.shape[0]
    C = tri_ref.shape[0]
    H, dk, dv = HGRN_HEADS, HGRN_EXPAND, HGRN_DV
    SB = SUBLANES

    @pl.when(pl.program_id(1) == 0)
    def _():
        st_ref[...] = jnp.zeros_like(st_ref)

    raw = lbraw_ref[...]
    e = jnp.exp(raw - jnp.max(raw, axis=0, keepdims=True))
    soft = e / jnp.sum(e, axis=0, keepdims=True)
    cs = soft[0:1]
    for l in range(1, layer + 1):
        cs = cs + soft[l:l + 1]
    lb = jnp.clip(cs - soft[0:1], 0.0, 1.0)
    tri = tri_ref[...]
    row_in_blk = lax.broadcasted_iota(I32, (SB, dk), 0)
    lane_t = lax.broadcasted_iota(I32, (SB, C), 1)
    levels = _hgrn_levels(C)

    def chunk(c, carry):
        r0 = pl.multiple_of(c * C, C)
        sig_pos, sig_neg = _sigmoid_pair(f_ref[pl.ds(r0, C), :])
        f = lb + (1.0 - lb) * sig_pos
        log_f = jnp.log(jnp.maximum(f, F_FLOOR))
        kk = (1.0 - lb) * sig_neg
        b = _cumsum_rows(tri, log_f)
        qq = q_ref[pl.ds(r0, C), :].astype(F32)
        vb = v_ref[pl.ds(r0, C), :]
        eb = jnp.exp(b)
        b_last = b[C - 1:C, :]
        eb_last = eb[C - 1:C, :]
        q_dec = (qq * eb).astype(BF16)
        k_dec = (kk * jnp.exp(b_last - b)).astype(BF16)
        outs = []
        for h in range(H):
            sl = slice(h * dk, (h + 1) * dk)
            bh, qh, kh, vh = b[:, sl], qq[:, sl], kk[:, sl], vb[:, h * dv:(h + 1) * dv]
            diag = []
            for blk in range(C // SB):
                bs = bh[blk * SB:(blk + 1) * SB, :]
                ks = kh[blk * SB:(blk + 1) * SB, :]
                at = jnp.zeros((SB, C), F32)
                for tt in range(SB):
                    t = blk * SB + tt
                    diff = jnp.where(row_in_blk <= tt, bh[t:t + 1, :] - bs, NEG_BIG)
                    p = (qh[t:t + 1, :] * jnp.exp(diff)) * ks
                    at = jnp.where(lane_t == t, jnp.sum(p, axis=-1, keepdims=True), at)
                diag.append(at)
            a_t = jnp.concatenate(diag, axis=0)
            for lev, m in enumerate(levels):
                qs, ks = [], []
                for blk in range(C // m):
                    rows = slice(blk * m, (blk + 1) * m)
                    if blk % 2 == 1:
                        ref = bh[blk * m - 1:blk * m, :]
                        qs.append(qh[rows] * jnp.exp(bh[rows] - ref))
                        ks.append(jnp.zeros((m, dk), F32))
                    else:
                        ref = bh[(blk + 1) * m - 1:(blk + 1) * m, :]
                        ks.append(kh[rows] * jnp.exp(ref - bh[rows]))
                        qs.append(jnp.zeros((m, dk), F32))
                q_l = jnp.concatenate(qs, axis=0).astype(BF16)
                k_l = jnp.concatenate(ks, axis=0).astype(BF16)
                a_t = a_t + _dot_nt(k_l, q_l) * lmask_ref[lev]
            intra = _dot_tn(a_t.astype(BF16), vh)
            st = st_ref[h]
            outs.append(intra + _dot_nt(q_dec[:, sl], st.astype(BF16)))
            st_ref[h] = st * eb_last[:, sl] + _dot_tn(vh, k_dec[:, sl])
        o = _rms(jnp.concatenate(outs, axis=1), gain_ref[...])
        gg = g_ref[pl.ds(r0, C), :]
        o_ref[pl.ds(r0, C), :] = (gg * _sigmoid(gg) * o).astype(BF16)
        return carry

    lax.fori_loop(0, TR // C, chunk, 0)


def _hgrn(zb, zf, hgrn_lb, gain, layer, B, T):
    H, dk, dv = HGRN_HEADS, HGRN_EXPAND, HGRN_DV
    C = min(HGRN_PAIR_CHUNK, T)
    TR = min(512, T)
    n = T // TR
    w = H * dk
    L = hgrn_lb.shape[0]
    tri = jnp.tril(jnp.ones((C, C), BF16))
    idx = np.arange(C)
    lmask = np.stack([(((idx[None, :] // m) % 2 == 1) & (idx[:, None] // m == idx[None, :] // m - 1))
                      for m in _hgrn_levels(C)]).astype(np.float32)
    return pl.pallas_call(
        functools.partial(_hgrn_kernel, layer),
        grid=(B, n),
        in_specs=[pl.BlockSpec((TR, w), lambda b, c: (b * n + c, ZB_HQ // w)),
                  pl.BlockSpec((TR, w), lambda b, c: (b * n + c, ZF_HF // w)),
                  pl.BlockSpec((TR, w), lambda b, c: (b * n + c, ZB_HV // w)),
                  pl.BlockSpec((TR, w), lambda b, c: (b * n + c, ZF_HG // w)),
                  pl.BlockSpec((L, w), lambda b, c: (0, 0)),
                  pl.BlockSpec((1, w), lambda b, c: (0, 0)),
                  pl.BlockSpec((C, C), lambda b, c: (0, 0)),
                  pl.BlockSpec(lmask.shape, lambda b, c: (0, 0, 0))],
        out_specs=pl.BlockSpec((TR, w), lambda b, c: (b * n + c, 0)),
        out_shape=jax.ShapeDtypeStruct((B * T, w), BF16),
        scratch_shapes=[pltpu.VMEM((H, dv, dk), F32)],
        compiler_params=_params("parallel", "arbitrary"),
        name="hgrn2",
    )(zb, zf, zb, zf, hgrn_lb, gain, tri, jnp.asarray(lmask))


def _dsa_prep_kernel(cq_ref, kv_ref, iw_ref, qn_ref, kn_ref, wq_ref, wqi_ref, wuk_ref,
                     qlt_ref, qit_ref, w_ref, ckv_ref, ckvt_ref, ik_ref):
    tm = cq_ref.shape[0]
    H, dh, HI, dI, Dc = DSA_HEADS, DSA_DH, IDX_HEADS, IDX_DIM, DSA_KV_RANK
    QB = qlt_ref.shape[2] // H
    nb = tm // QB
    cq = _rms(cq_ref[...].astype(F32), qn_ref[...]).astype(BF16)
    ckv = _rms(kv_ref[:, :Dc].astype(F32), kn_ref[...])
    ckv_ref[...] = ckv.astype(BF16)
    KB = ckvt_ref.shape[3]
    for j in range(tm // KB):
        ckvt_ref[0, j] = ckv[j * KB:(j + 1) * KB].T.astype(BF16)
    ik_ref[...] = kv_ref[:, Dc:Dc + dI]
    w_t = iw_ref[...].T[0:HI, :] * (HI * dI) ** -0.5
    qi_all = _dot(cq, wqi_ref[...])
    for j in range(nb):
        rows = slice(j * QB, (j + 1) * QB)
        qi_t = qi_all[rows].T
        for h in range(HI):
            qit_ref[j, :, h * QB:(h + 1) * QB] = qi_t[h * dI:(h + 1) * dI].astype(BF16)
            w_ref[j, :, h * QB:(h + 1) * QB] = w_t[h:h + 1, rows]
    for h in range(H):
        qh = _dot(cq, wq_ref[h])
        ql = _dot(qh.astype(BF16), wuk_ref[h]) * dh ** -0.5
        for j in range(nb):
            qlt_ref[j, :, h * QB:(h + 1) * QB] = ql[j * QB:(j + 1) * QB].T.astype(BF16)


def _dsa_prep(zb, zf, qn, kn, wq, wqi, wuk, B, T):
    M = B * T
    H, dh, HI, dI, Dc, Rq = DSA_HEADS, DSA_DH, IDX_HEADS, IDX_DIM, DSA_KV_RANK, DSA_Q_RANK
    QB = min(Q_BLOCK, T)
    KB = KEY_BLOCK
    tm = min(DSA_PREP_ROWS, T)
    nb = tm // QB
    nt = T // tm
    kpt = tm // KB
    return pl.pallas_call(
        _dsa_prep_kernel,
        grid=(M // tm,),
        in_specs=[pl.BlockSpec((tm, Rq), lambda i: (i, ZB_CQ // Rq)),
                  pl.BlockSpec((tm, 2 * Dc), lambda i: (i, ZB_CKV // (2 * Dc))),
                  pl.BlockSpec((tm, LANES), lambda i: (i, ZF_IW // LANES)),
                  pl.BlockSpec((1, Rq), lambda i: (0, 0)),
                  pl.BlockSpec((1, Dc), lambda i: (0, 0)),
                  pl.BlockSpec((H, Rq, dh), lambda i: (0, 0, 0)),
                  pl.BlockSpec((Rq, HI * dI), lambda i: (0, 0)),
                  pl.BlockSpec((H, dh, Dc), lambda i: (0, 0, 0))],
        out_specs=[pl.BlockSpec((nb, Dc, H * QB), lambda i: (i, 0, 0)),
                   pl.BlockSpec((nb, dI, HI * QB), lambda i: (i, 0, 0)),
                   pl.BlockSpec((nb, 1, HI * QB), lambda i: (i, 0, 0)),
                   pl.BlockSpec((tm, Dc), lambda i: (i, 0)),
                   pl.BlockSpec((1, kpt, Dc, KB), lambda i: (i // nt, i % nt, 0, 0)),
                   pl.BlockSpec((tm, dI), lambda i: (i, 0))],
        out_shape=[jax.ShapeDtypeStruct((M // QB, Dc, H * QB), BF16),
                   jax.ShapeDtypeStruct((M // QB, dI, HI * QB), BF16),
                   jax.ShapeDtypeStruct((M // QB, 1, HI * QB), F32),
                   jax.ShapeDtypeStruct((M, Dc), BF16),
                   jax.ShapeDtypeStruct((B, nt * kpt, Dc, KB), BF16),
                   jax.ShapeDtypeStruct((M, dI), BF16)],
        compiler_params=_params("parallel"),
        name="dsa_prep",
    )(zb, zb, zf, qn, kn, wq, wqi, wuk)


def _sortable_key(s):
    s = jnp.where(s == 0.0, 0.0, s)
    bits = pltpu.bitcast(s, I32)
    return bits ^ ((bits >> 31) & 0x7FFFFFFF)


def _dsa_attn_kernel(topk, n_keys, neg_key, qlt_ref, qit_ref, w_ref, ckv_ref, ckvt_ref, ik_ref, bias_ref,
                     wuvt_ref, o_ref, key_ref, z_ref, acc_ref, tie_ref):
    H, HI = DSA_HEADS, IDX_HEADS
    QB = o_ref.shape[0]
    KB = key_ref.shape[1]
    q0 = pl.program_id(1) * QB
    nkb = (q0 + QB - 1) // KB + 1
    n_skip = n_keys - nkb * KB

    s_row = lax.broadcasted_iota(I32, (KB, QB), 0)
    t_row = q0 + lax.broadcasted_iota(I32, (1, QB), 1)

    qit = qit_ref[0]
    w_row = w_ref[0]

    def score_blk(kb, carry):
        k0 = pl.multiple_of(kb * KB, KB)
        p = _dot(ik_ref[pl.ds(k0, KB), :], qit)
        p = jnp.maximum(p, 0.0) * w_row
        s = p[:, 0:QB]
        for h in range(1, HI):
            s = s + p[:, h * QB:(h + 1) * QB]
        s = jnp.where(k0 + s_row <= t_row, s, NEG_BIG)
        key_ref[kb] = _sortable_key(s)
        return carry

    lax.fori_loop(0, nkb, score_blk, 0)

    def count(hit_fn):
        def body(kb, acc):
            return acc + _group_sum(jnp.where(hit_fn(kb, key_ref[kb]), 1, 0))

        acc = lax.fori_loop(0, nkb, body, jnp.zeros((SUBLANES, QB), I32))
        return jnp.sum(acc, axis=0, keepdims=True)

    def count_ge(cand):
        return count(lambda kb, kk: kk >= cand) + jnp.where(cand <= neg_key, n_skip, 0)

    zero = jnp.zeros((1, QB), I32)
    thr = jnp.where(count_ge(zero) >= topk, zero, jnp.full((1, QB), INT_MIN, I32))

    def bisect(it, thr):
        cand = thr | jnp.left_shift(jnp.int32(1), 30 - it)
        return jnp.where(count_ge(cand) >= topk, cand, thr)

    thr = lax.fori_loop(0, 31, bisect, thr)

    n_gt = count(lambda kb, kk: kk > thr) + jnp.where(thr < neg_key, n_skip, 0)
    n_eq = count(lambda kb, kk: kk == thr)
    need = topk - n_gt
    nbits = max(1, (n_keys - 1).bit_length())
    tie_ref[...] = jnp.full(tie_ref.shape, (1 << nbits) - 1, I32)

    @pl.when(jnp.max(jnp.where(need < n_eq, 1, 0)) > 0)
    def _():
        def ibisect(it, p):
            cand = p | jnp.left_shift(jnp.int32(1), nbits - 1 - it)
            below = count(lambda kb, kk: (kk == thr) & (kb * KB + s_row < cand))
            return jnp.where(below < need, cand, p)

        p = lax.fori_loop(0, nbits, ibisect, jnp.zeros((1, QB), I32))
        tie_ref[...] = jnp.broadcast_to(p, tie_ref.shape)

    tie = tie_ref[0:1, :]

    qlt = qlt_ref[0]

    def logits_blk(kb, m8):
        k0 = pl.multiple_of(kb * KB, KB)
        kk = key_ref[kb]
        s_glob = k0 + s_row
        sel = ((kk > thr) | ((kk == thr) & (s_glob <= tie))) & (s_glob <= t_row)
        z = _dot(ckv_ref[pl.ds(k0, KB), :], qlt) + bias_ref[jnp.minimum((q0 - k0) // QB, 3)]
        tops = []
        for h in range(H):
            zh = jnp.where(sel, z[:, h * QB:(h + 1) * QB], NEG_BIG)
            z_ref[kb, :, h * QB:(h + 1) * QB] = zh
            tops.append(_group_max(zh))
        return jnp.maximum(m8, jnp.concatenate(tops, axis=1))

    m8 = lax.fori_loop(0, nkb, logits_blk, jnp.full((SUBLANES, H * QB), NEG_BIG, F32))
    m = jnp.max(m8, axis=0, keepdims=True)
    acc_ref[...] = jnp.zeros(acc_ref.shape, F32)

    def pv_blk(kb, l8):
        p = jnp.exp(z_ref[kb] - m)
        acc_ref[...] += _dot(ckvt_ref[0, kb], p.astype(BF16))
        return l8 + _group_sum(p)

    l8 = lax.fori_loop(0, nkb, pv_blk, jnp.zeros((SUBLANES, H * QB), F32))
    inv_l = 1.0 / jnp.sum(l8, axis=0, keepdims=True)
    o_lat_t = (acc_ref[...] * inv_l).astype(BF16)
    y_t = jnp.concatenate([_dot(wuvt_ref[h], o_lat_t[:, h * QB:(h + 1) * QB]) for h in range(H)], axis=0)
    o_ref[...] = y_t.T.astype(BF16)


def _dsa_attn(qlt, qit, w_row, ckvn, ckvt, idxk, bias_tiles, wuvt, B, T):
    H, HI, dI, Dc, dh = DSA_HEADS, IDX_HEADS, IDX_DIM, DSA_KV_RANK, DSA_DH
    QB = min(Q_BLOCK, T)
    KB = KEY_BLOCK
    nq = T // QB
    nk = T // KB
    topk = min(DSA_TOPK_MAX, T // 4)
    neg_key = int(np.array(NEG_BIG, np.float32).view(np.int32))
    neg_key = neg_key ^ ((neg_key >> 31) & 0x7FFFFFFF)
    return pl.pallas_call(
        functools.partial(_dsa_attn_kernel, topk, T, neg_key),
        grid=(B, nq),
        in_specs=[pl.BlockSpec((1, Dc, H * QB), lambda b, i: (b * nq + i, 0, 0)),
                  pl.BlockSpec((1, dI, HI * QB), lambda b, i: (b * nq + i, 0, 0)),
                  pl.BlockSpec((1, 1, HI * QB), lambda b, i: (b * nq + i, 0, 0)),
                  pl.BlockSpec((T, Dc), lambda b, i: (b, 0)),
                  pl.BlockSpec((1, nk, Dc, KB), lambda b, i: (b, 0, 0, 0)),
                  pl.BlockSpec((T, dI), lambda b, i: (b, 0)),
                  pl.BlockSpec((4, KB, H * QB), lambda b, i: (0, 0, 0), pipeline_mode=pl.Buffered(1)),
                  pl.BlockSpec((H, dh, Dc), lambda b, i: (0, 0, 0))],
        out_specs=pl.BlockSpec((QB, DSA_W), lambda b, i: (b * nq + i, 0)),
        out_shape=jax.ShapeDtypeStruct((B * T, DSA_W), BF16),
        scratch_shapes=[pltpu.VMEM((nk, KB, QB), I32),
                        pltpu.VMEM((nk, KB, H * QB), F32),
                        pltpu.VMEM((Dc, H * QB), F32),
                        pltpu.VMEM((SUBLANES, QB), I32)],
        compiler_params=_params("parallel", "arbitrary"),
        name="dsa_attn",
    )(qlt, qit, w_row, ckvn, ckvt, idxk, bias_tiles, wuvt)


def _rel_bias_tiles(rel_bias, QB, KB):
    max_exact = REL_BUCKETS // 2
    n_far = 3 * QB
    assert n_far - (KB - 1) > REL_MAX_DIST
    n = jnp.arange(n_far, dtype=jnp.int32)
    nf = jnp.maximum(n, max_exact).astype(F32)
    large = max_exact + (jnp.log(nf / max_exact) / math.log(REL_MAX_DIST / max_exact)
                         * (REL_BUCKETS - max_exact)).astype(jnp.int32)
    large = jnp.minimum(large, REL_BUCKETS - 1)
    bucket = jnp.where(n < max_exact, n, large)
    H = rel_bias.shape[1]
    tab = jnp.take(rel_bias, bucket, axis=0).T
    P = KB + QB
    tiles = []
    for delta in (0, QB, 2 * QB):
        d = np.concatenate([np.arange(QB + 1), np.arange(-(KB - 1), 0)])
        g = tab[:, np.clip(delta + d, 0, n_far - 1)]
        rows = jnp.tile(g, (1, KB))[:, :KB * (P - 1)].reshape(H, KB, P - 1)[:, :, :QB]
        tiles.append(rows.transpose(1, 0, 2).reshape(KB, H * QB))
    tiles.append(jnp.broadcast_to(jnp.repeat(rel_bias[REL_BUCKETS - 1], QB)[None, :], (KB, H * QB)))
    return jnp.stack(tiles).astype(F32)


def _merge_kernel(x_ref, yr_ref, yd_ref, yh_ref, lnpre_ref, wg_ref, wr_ref, wd_ref, wh_ref, wo_ref,
                  ln_ref, o_ref):
    D = x_ref.shape[1]
    x = x_ref[...]
    h = _rms(x, lnpre_ref[...]).astype(BF16)
    m = None
    for k, (y_ref, w_ref) in enumerate(((yr_ref, wr_ref), (yd_ref, wd_ref), (yh_ref, wh_ref))):
        gate = _sigmoid(_dot(h, wg_ref[:, k * D:(k + 1) * D]))
        term = gate * _dot(y_ref[...], w_ref[...])
        m = term if m is None else m + term
    u = _dot(m.astype(BF16), wo_ref[...])
    o_ref[...] = x + _rms(u, ln_ref[...])


def _merge(x2, y_ret, y_dsa, y_hg, lnpre, wg, wr, wd, wh, wo, ln):
    M = x2.shape[0]
    tm = min(1024, M)
    D = D_MODEL
    row = lambda w: pl.BlockSpec((tm, w), lambda i: (i, 0))
    const = lambda r, c: pl.BlockSpec((r, c), lambda i: (0, 0))
    return pl.pallas_call(
        _merge_kernel,
        grid=(M // tm,),
        in_specs=[row(D), row(RET_W), row(DSA_W), row(HGRN_W), const(1, D), const(D, N_BRANCH * D),
                  const(RET_W, D), const(DSA_W, D), const(HGRN_W, D), const(D, D), const(1, D)],
        out_specs=row(D),
        out_shape=jax.ShapeDtypeStruct((M, D), F32),
        compiler_params=_params("parallel"),
        name="merge",
    )(x2, y_ret, y_dsa, y_hg, lnpre, wg, wr, wd, wh, wo, ln)


def _gelu_tanh(x):
    return 0.5 * x * (1.0 + jnp.tanh(math.sqrt(2.0 / math.pi) * (x + 0.044715 * (x * x * x))))


def _ffn_kernel(tiles_per_seq, x_ref, lnpre_ref, wup_ref, cw_ref, cb_ref, wdn_ref, lnpost_ref,
                o_ref, buf_ref, prev_ref):
    tm = x_ref.shape[0]
    HALO = SUBLANES
    fc = buf_ref.shape[1]
    n_pass = D_FF // fc
    first = (pl.program_id(0) % tiles_per_seq) == 0

    @pl.when(first)
    def _():
        prev_ref[...] = jnp.zeros_like(prev_ref)

    x = x_ref[...]
    h = _rms(x, lnpre_ref[...]).astype(BF16)

    def conv(part, c):
        col = part * D_FF + c * fc
        up = _dot(h, wup_ref[:, col:col + fc])
        slot = part * n_pass + c
        buf_ref[0:HALO, :] = prev_ref[slot]
        buf_ref[HALO:HALO + tm, :] = up
        prev_ref[slot] = up[tm - HALO:tm, :]
        w = cw_ref[:, col:col + fc]
        y = (up * w[2:3] + buf_ref[HALO - 1:HALO - 1 + tm, :] * w[1:2]
             + buf_ref[HALO - 2:HALO - 2 + tm, :] * w[0:1])
        return y + cb_ref[:, col:col + fc]

    acc = None
    for c in range(n_pass):
        a = conv(0, c)
        u = conv(1, c)
        act = (_gelu_tanh(a) * u).astype(BF16)
        d = _dot(act, wdn_ref[c * fc:(c + 1) * fc, :])
        acc = d if acc is None else acc + d
    o_ref[...] = x + _rms(acc, lnpost_ref[...])


def _ffn(x2, lnpre, wup, cw, cb, wdn, lnpost, T):
    M = x2.shape[0]
    D = D_MODEL
    tm = min(512, T)
    fc = D_FF
    const = lambda r, c: pl.BlockSpec((r, c), lambda i: (0, 0))
    return pl.pallas_call(
        functools.partial(_ffn_kernel, T // tm),
        grid=(M // tm,),
        in_specs=[pl.BlockSpec((tm, D), lambda i: (i, 0)), const(1, D), const(D, 2 * D_FF),
                  const(CONV_WIDTH, 2 * D_FF), const(1, 2 * D_FF), const(D_FF, D), const(1, D)],
        out_specs=pl.BlockSpec((tm, D), lambda i: (i, 0)),
        out_shape=jax.ShapeDtypeStruct((M, D), F32),
        scratch_shapes=[pltpu.VMEM((tm + SUBLANES, fc), F32),
                        pltpu.VMEM((2 * (D_FF // fc), SUBLANES, fc), F32)],
        compiler_params=_params("arbitrary"),
        name="conv_ffn",
    )(x2, lnpre, wup, cw, cb, wdn, lnpost)


def _split_w_in(w):
    widths = (256, 256, 512, 512, 256, 128, 64, 8, 512, 512, 512, 512, N_BRANCH * D_MODEL)
    offs = np.concatenate([[0], np.cumsum(widths)])
    rq, rk, rv, rg, cq, ckv, ik, iw, hq, hf, hv, hg, gt = [w[..., offs[k]:offs[k + 1]] for k in range(len(widths))]
    zeros = lambda n: jnp.zeros(w.shape[:-1] + (n,), w.dtype)
    wb = jnp.concatenate([rq, rk, rv, cq, ckv, ik, zeros(2 * DSA_KV_RANK - DSA_KV_RANK - IDX_DIM), hq, hv], axis=-1)
    wf = jnp.concatenate([rg, hf, hg, iw, zeros(LANES - IDX_HEADS)], axis=-1)
    assert wb.shape[-1] == ZB_W and wf.shape[-1] == ZF_W
    return wb.astype(BF16), wf.astype(BF16), gt.astype(BF16)


def kernel(x, rel_bias, hgrn_lb, ln_mix_pre, ln_mix_post, ln_ffn_pre, ln_ffn_post, w_in, dsa_q_norm, dsa_kv_norm, dsa_w_uq, dsa_w_uk, dsa_w_uv, hgrn_norm, w_br_ret, w_br_dsa, w_br_hgrn, w_out, ffn_w_up, ffn_conv_w, ffn_conv_b, ffn_w_down):
    B, T, D = x.shape
    depth = w_in.shape[0]
    assert T % KEY_BLOCK == 0 and D == D_MODEL
    H, dh, Rq = DSA_HEADS, DSA_DH, DSA_Q_RANK
    bias_tiles = _rel_bias_tiles(rel_bias, min(Q_BLOCK, T), KEY_BLOCK)
    wq = dsa_w_uq[..., :H * dh].reshape(depth, Rq, H, dh).transpose(0, 2, 1, 3).astype(BF16)
    wqi = dsa_w_uq[..., H * dh:].astype(BF16)
    wuk = dsa_w_uk.astype(BF16)
    wuvt = dsa_w_uv.transpose(0, 1, 3, 2).astype(BF16)
    x2 = x.reshape(B * T, D)
    row = lambda v: v.reshape(1, -1)
    for l in range(depth):
        wb, wf, wg = _split_w_in(w_in[l])
        zb, zf = _inproj(x2, row(ln_mix_pre[l]), wb, wf)
        y_ret = _retention(zb, zf, B, T)
        y_hg = _hgrn(zb, zf, hgrn_lb, row(hgrn_norm[l]), l, B, T)
        qlt, qit, w_row, ckvn, ckvt, idxk = _dsa_prep(zb, zf, row(dsa_q_norm[l]), row(dsa_kv_norm[l]),
                                                      wq[l], wqi[l], wuk[l], B, T)
        y_dsa = _dsa_attn(qlt, qit, w_row, ckvn, ckvt, idxk, bias_tiles, wuvt[l], B, T)
        x2 = _merge(x2, y_ret, y_dsa, y_hg, row(ln_mix_pre[l]), wg, w_br_ret[l].astype(BF16),
                    w_br_dsa[l].astype(BF16), w_br_hgrn[l].astype(BF16), w_out[l].astype(BF16),
                    row(ln_mix_post[l]))
        x2 = _ffn(x2, row(ln_ffn_pre[l]), ffn_w_up[l].astype(BF16), ffn_conv_w[l], row(ffn_conv_b[l]),
                  ffn_w_down[l].astype(BF16), row(ln_ffn_post[l]), T)
    return x2.reshape(B, T, D)
```

```python
import functools
import math

import jax
import jax.numpy as jnp
import numpy as np
from jax import lax
from jax.experimental import pallas as pl
from jax.experimental.pallas import tpu as pltpu

F32 = jnp.float32
BF16 = jnp.bfloat16
I32 = jnp.int32

D_MODEL = 1024
RET_HEADS, RET_DK, RET_DV, RET_CHUNK = 4, 64, 128, 128
DSA_HEADS, DSA_DH, DSA_Q_RANK, DSA_KV_RANK = 8, 64, 256, 128
IDX_HEADS, IDX_DIM, DSA_TOPK_MAX, Q_BLOCK = 8, 64, 256, 256
HGRN_HEADS, HGRN_EXPAND, HGRN_DV = 4, 128, 128
F_FLOOR = 1e-6
REL_BUCKETS, REL_MAX_DIST = 32, 128
D_FF = 2816
CONV_WIDTH = 3
N_BRANCH = 3
EPS = 1e-6
NEG_BIG = -1e30

RET_W = RET_HEADS * RET_DV
DSA_W = DSA_HEADS * DSA_DH
HGRN_KW = HGRN_HEADS * HGRN_EXPAND
HGRN_W = HGRN_HEADS * HGRN_DV

ZB_RQ, ZB_RK, ZB_RV, ZB_CQ = 0, 256, 512, 1024
ZB_CKV = 1280
ZB_HQ, ZB_HV = 1536, 2048
ZB_W = 2560
ZF_RG, ZF_HF, ZF_HG, ZF_IW = 0, 512, 1024, 1536
ZF_W = 1664

VMEM_LIMIT_BYTES = 56 * 1024 * 1024
SUBLANES = 8
LANES = 128

KEY_BLOCK = 256
DSA_PREP_ROWS = 512
HGRN_PAIR_CHUNK = 128
INT_MIN = -(2 ** 31)


def _params(*sem):
    return pltpu.CompilerParams(dimension_semantics=sem, vmem_limit_bytes=VMEM_LIMIT_BYTES)


def _dot(a, b):
    return jnp.dot(a, b, preferred_element_type=F32)


def _dot_nt(a, b):
    return lax.dot_general(a, b, (((1,), (1,)), ((), ())), preferred_element_type=F32)


def _dot_tn(a, b):
    return lax.dot_general(a, b, (((0,), (0,)), ((), ())), preferred_element_type=F32)


def _sigmoid(x):
    return 1.0 / (1.0 + jnp.exp(-x))


def _sigmoid_pair(x):
    t = jnp.exp(-jnp.abs(x))
    big = 1.0 / (1.0 + t)
    small = t * big
    pos = x >= 0.0
    return jnp.where(pos, big, small), jnp.where(pos, small, big)


def _rms(x, g):
    return x * lax.rsqrt(jnp.mean(x * x, axis=-1, keepdims=True) + EPS) * g


def _group_sum(x):
    return jnp.sum(x.reshape(x.shape[0] // SUBLANES, SUBLANES, x.shape[1]), axis=0)


def _group_max(x):
    return jnp.max(x.reshape(x.shape[0] // SUBLANES, SUBLANES, x.shape[1]), axis=0)


def _inproj_kernel(x_ref, g_ref, wb_ref, wf_ref, zb_ref, zf_ref):
    h = _rms(x_ref[...], g_ref[...]).astype(BF16)
    zb_ref[...] = _dot(h, wb_ref[...]).astype(BF16)
    zf_ref[...] = _dot(h, wf_ref[...])


def _inproj(x2, g, wb, wf):
    M = x2.shape[0]
    tm = min(1024, M)
    const = lambda r, c: pl.BlockSpec((r, c), lambda i: (0, 0))
    return pl.pallas_call(
        _inproj_kernel,
        grid=(M // tm,),
        in_specs=[pl.BlockSpec((tm, D_MODEL), lambda i: (i, 0)), const(1, D_MODEL),
                  const(D_MODEL, ZB_W), const(D_MODEL, ZF_W)],
        out_specs=[pl.BlockSpec((tm, ZB_W), lambda i: (i, 0)), pl.BlockSpec((tm, ZF_W), lambda i: (i, 0))],
        out_shape=[jax.ShapeDtypeStruct((M, ZB_W), BF16), jax.ShapeDtypeStruct((M, ZF_W), F32)],
        compiler_params=_params("parallel"),
        name="inproj",
    )(x2, g, wb, wf)


def _ret_kernel(gam_ref, q_ref, k_ref, v_ref, g_ref, cos_ref, sin_ref, dm_ref, xi_ref, zeta_ref,
                o_ref, r_ref):
    C = q_ref.shape[0]
    H, dk, dv = RET_HEADS, RET_DK, RET_DV

    @pl.when(pl.program_id(1) == 0)
    def _():
        r_ref[...] = jnp.zeros_like(r_ref)

    cos = cos_ref[...]
    sin = sin_ref[...]
    lane = lax.broadcasted_iota(I32, (C, H * dk), 1)
    first_half = (lane % dk) < (dk // 2)

    def rot(x):
        swapped = jnp.where(first_half, pltpu.roll(x, H * dk - dk // 2, 1), pltpu.roll(x, dk // 2, 1))
        return x * cos + swapped * sin

    q = rot(q_ref[...].astype(F32))
    k = rot(k_ref[...].astype(F32)) * dk ** -0.5
    for h in range(H):
        qh = q[:, h * dk:(h + 1) * dk].astype(BF16)
        kh = k[:, h * dk:(h + 1) * dk]
        vh = v_ref[:, h * dv:(h + 1) * dv]
        s = _dot_nt(qh, kh.astype(BF16)) * dm_ref[h]
        inner = _dot(s.astype(BF16), vh)
        rh = r_ref[h]
        cross = _dot(qh, rh.astype(BF16)) * xi_ref[h]
        r_ref[h] = gam_ref[h] * rh + _dot_tn((kh * zeta_ref[h]).astype(BF16), vh)
        o = inner + cross
        mu = jnp.mean(o, axis=-1, keepdims=True)
        oc = o - mu
        var = jnp.mean(oc * oc, axis=-1, keepdims=True)
        gh = g_ref[:, h * dv:(h + 1) * dv]
        o_ref[:, h * dv:(h + 1) * dv] = (gh * _sigmoid(gh) * (oc * lax.rsqrt(var + EPS))).astype(BF16)


def _retention(zb, zf, B, T):
    H, dk, dv = RET_HEADS, RET_DK, RET_DV
    C = min(RET_CHUNK, T)
    n = T // C
    pos = np.arange(T, dtype=np.float64)
    half = dk // 2
    freq = 1.0 / (10000.0 ** np.linspace(0.0, 1.0, half))
    ang = pos[:, None] * freq[None, :]
    cos = jnp.asarray(np.tile(np.cos(ang), (1, 2 * H)), F32)
    sin = jnp.asarray(np.tile(np.concatenate([-np.sin(ang), np.sin(ang)], axis=1), (1, H)), F32)
    log_gamma = np.log1p(-(2.0 ** (-5.0 - np.arange(H, dtype=np.float64))))
    i = np.arange(C, dtype=np.float64)
    rel = i[:, None] - i[None, :]
    dmask = jnp.asarray(np.where(rel >= 0, np.exp(np.maximum(rel, 0.0)[None] * log_gamma[:, None, None]), 0.0), F32)
    xi = jnp.asarray(np.exp((i + 1.0)[None, :] * log_gamma[:, None])[:, :, None], F32)
    zeta = jnp.asarray(np.exp((C - 1.0 - i)[None, :] * log_gamma[:, None])[:, :, None], F32)
    gamma_c = jnp.asarray(np.exp(C * log_gamma), F32)

    wq = H * dk
    wv = H * dv
    full = lambda shape: pl.BlockSpec(shape, lambda b, c: (0,) * len(shape))
    return pl.pallas_call(
        _ret_kernel,
        grid=(B, n),
        in_specs=[pl.BlockSpec(memory_space=pltpu.SMEM),
                  pl.BlockSpec((C, wq), lambda b, c: (b * n + c, ZB_RQ // wq)),
                  pl.BlockSpec((C, wq), lambda b, c: (b * n + c, ZB_RK // wq)),
                  pl.BlockSpec((C, wv), lambda b, c: (b * n + c, ZB_RV // wv)),
                  pl.BlockSpec((C, wv), lambda b, c: (b * n + c, ZF_RG // wv)),
                  pl.BlockSpec((C, wq), lambda b, c: (c, 0)),
                  pl.BlockSpec((C, wq), lambda b, c: (c, 0)),
                  full((H, C, C)), full((H, C, 1)), full((H, C, 1))],
        out_specs=pl.BlockSpec((C, wv), lambda b, c: (b * n + c, 0)),
        out_shape=jax.ShapeDtypeStruct((B * T, wv), BF16),
        scratch_shapes=[pltpu.VMEM((H, dk, dv), F32)],
        compiler_params=_params("parallel", "arbitrary"),
        name="retention",
    )(gamma_c, zb, zb, zb, zf, cos, sin, dmask, xi, zeta)


def _hgrn_levels(C):
    ms, m = [], C // 2
    while m >= SUBLANES:
        ms.append(m)
        m //= 2
    return ms


def _cumsum_rows(tri, x):
    hi = x.astype(BF16)
    rest = x - hi.astype(F32)
    mid = rest.astype(BF16)
    lo = (rest - mid.astype(F32)).astype(BF16)
    return _dot(tri, hi) + (_dot(tri, mid) + _dot(tri, lo))


def _hgrn_kernel(layer, q_ref, f_ref, v_ref, g_ref, lbraw_ref, gain_ref, tri_ref, lmask_ref, o_ref, st_ref):
    TR = q_ref.shape[0]
    C = tri_ref.shape[0]
    H, dk, dv = HGRN_HEADS, HGRN_EXPAND, HGRN_DV
    SB = SUBLANES

    @pl.when(pl.program_id(1) == 0)
    def _():
        st_ref[...] = jnp.zeros_like(st_ref)

    raw = lbraw_ref[...]
    e = jnp.exp(raw - jnp.max(raw, axis=0, keepdims=True))
    soft = e / jnp.sum(e, axis=0, keepdims=True)
    cs = soft[0:1]
    for l in range(1, layer + 1):
        cs = cs + soft[l:l + 1]
    lb = jnp.clip(cs - soft[0:1], 0.0, 1.0)
    tri = tri_ref[...]
    row_in_blk = lax.broadcasted_iota(I32, (SB, dk), 0)
    lane_t = lax.broadcasted_iota(I32, (SB, C), 1)
    levels = _hgrn_levels(C)

    def chunk(c, carry):
        r0 = pl.multiple_of(c * C, C)
        sig_pos, sig_neg = _sigmoid_pair(f_ref[pl.ds(r0, C), :])
        f = lb + (1.0 - lb) * sig_pos
        log_f = jnp.log(jnp.maximum(f, F_FLOOR))
        kk = (1.0 - lb) * sig_neg
        b = _cumsum_rows(tri, log_f)
        qq = q_ref[pl.ds(r0, C), :].astype(F32)
        vb = v_ref[pl.ds(r0, C), :]
        eb = jnp.exp(b)
        b_last = b[C - 1:C, :]
        eb_last = eb[C - 1:C, :]
        q_dec = (qq * eb).astype(BF16)
        k_dec = (kk * jnp.exp(b_last - b)).astype(BF16)
        outs = []
        for h in range(H):
            sl = slice(h * dk, (h + 1) * dk)
            bh, qh, kh, vh = b[:, sl], qq[:, sl], kk[:, sl], vb[:, h * dv:(h + 1) * dv]
            diag = []
            for blk in range(C // SB):
                bs = bh[blk * SB:(blk + 1) * SB, :]
                ks = kh[blk * SB:(blk + 1) * SB, :]
                at = jnp.zeros((SB, C), F32)
                for tt in range(SB):
                    t = blk * SB + tt
                    diff = jnp.where(row_in_blk <= tt, bh[t:t + 1, :] - bs, NEG_BIG)
                    p = (qh[t:t + 1, :] * jnp.exp(diff)) * ks
                    at = jnp.where(lane_t == t, jnp.sum(p, axis=-1, keepdims=True), at)
                diag.append(at)
            a_t = jnp.concatenate(diag, axis=0)
            for lev, m in enumerate(levels):
                qs, ks = [], []
                for blk in range(C // m):
                    rows = slice(blk * m, (blk + 1) * m)
                    if blk % 2 == 1:
                        ref = bh[blk * m - 1:blk * m, :]
                        qs.append(qh[rows] * jnp.exp(bh[rows] - ref))
                        ks.append(jnp.zeros((m, dk), F32))
                    else:
                        ref = bh[(blk + 1) * m - 1:(blk + 1) * m, :]
                        ks.append(kh[rows] * jnp.exp(ref - bh[rows]))
                        qs.append(jnp.zeros((m, dk), F32))
                q_l = jnp.concatenate(qs, axis=0).astype(BF16)
                k_l = jnp.concatenate(ks, axis=0).astype(BF16)
                a_t = a_t + _dot_nt(k_l, q_l) * lmask_ref[lev]
            intra = _dot_tn(a_t.astype(BF16), vh)
            st = st_ref[h]
            outs.append(intra + _dot_nt(q_dec[:, sl], st.astype(BF16)))
            st_ref[h] = st * eb_last[:, sl] + _dot_tn(vh, k_dec[:, sl])
        o = _rms(jnp.concatenate(outs, axis=1), gain_ref[...])
        gg = g_ref[pl.ds(r0, C), :]
        o_ref[pl.ds(r0, C), :] = (gg * _sigmoid(gg) * o).astype(BF16)
        return carry

    lax.fori_loop(0, TR // C, chunk, 0)


def _hgrn(zb, zf, hgrn_lb, gain, layer, B, T):
    H, dk, dv = HGRN_HEADS, HGRN_EXPAND, HGRN_DV
    C = min(HGRN_PAIR_CHUNK, T)
    TR = min(512, T)
    n = T // TR
    w = H * dk
    L = hgrn_lb.shape[0]
    tri = jnp.tril(jnp.ones((C, C), BF16))
    idx = np.arange(C)
    lmask = np.stack([(((idx[None, :] // m) % 2 == 1) & (idx[:, None] // m == idx[None, :] // m - 1))
                      for m in _hgrn_levels(C)]).astype(np.float32)
    return pl.pallas_call(
        functools.partial(_hgrn_kernel, layer),
        grid=(B, n),
        in_specs=[pl.BlockSpec((TR, w), lambda b, c: (b * n + c, ZB_HQ // w)),
                  pl.BlockSpec((TR, w), lambda b, c: (b * n + c, ZF_HF // w)),
                  pl.BlockSpec((TR, w), lambda b, c: (b * n + c, ZB_HV // w)),
                  pl.BlockSpec((TR, w), lambda b, c: (b * n + c, ZF_HG // w)),
                  pl.BlockSpec((L, w), lambda b, c: (0, 0)),
                  pl.BlockSpec((1, w), lambda b, c: (0, 0)),
                  pl.BlockSpec((C, C), lambda b, c: (0, 0)),
                  pl.BlockSpec(lmask.shape, lambda b, c: (0, 0, 0))],
        out_specs=pl.BlockSpec((TR, w), lambda b, c: (b * n + c, 0)),
        out_shape=jax.ShapeDtypeStruct((B * T, w), BF16),
        scratch_shapes=[pltpu.VMEM((H, dv, dk), F32)],
        compiler_params=_params("parallel", "arbitrary"),
        name="hgrn2",
    )(zb, zf, zb, zf, hgrn_lb, gain, tri, jnp.asarray(lmask))


def _dsa_prep_kernel(cq_ref, kv_ref, iw_ref, qn_ref, kn_ref, wq_ref, wqi_ref, wukt_ref,
                     qlt_ref, qit_ref, w_ref, ckv_ref, ckvt_ref, ik_ref):
    tm = cq_ref.shape[0]
    H, dh, HI, dI, Dc = DSA_HEADS, DSA_DH, IDX_HEADS, IDX_DIM, DSA_KV_RANK
    QB = qlt_ref.shape[2] // H
    nb = tm // QB
    cq = _rms(cq_ref[...].astype(F32), qn_ref[...]).astype(BF16)
    ckv = _rms(kv_ref[:, :Dc].astype(F32), kn_ref[...])
    ckv_ref[...] = ckv.astype(BF16)
    KB = ckvt_ref.shape[3]
    for j in range(tm // KB):
        ckvt_ref[0, j] = ckv[j * KB:(j + 1) * KB].T.astype(BF16)
    ik_ref[...] = kv_ref[:, Dc:Dc + dI]
    w_t = iw_ref[...].T[0:HI, :] * (HI * dI) ** -0.5
    qi_all = _dot(cq, wqi_ref[...])
    q_all = _dot(cq, wq_ref[...])
    for j in range(nb):
        rows = slice(j * QB, (j + 1) * QB)
        qi_t = qi_all[rows].T
        q_t = q_all[rows].T.astype(BF16)
        for h in range(HI):
            qit_ref[j, :, h * QB:(h + 1) * QB] = qi_t[h * dI:(h + 1) * dI].astype(BF16)
            w_ref[j, :, h * QB:(h + 1) * QB] = w_t[h:h + 1, rows]
        for h in range(H):
            ql_t = _dot(wukt_ref[h], q_t[h * dh:(h + 1) * dh]) * dh ** -0.5
            qlt_ref[j, :, h * QB:(h + 1) * QB] = ql_t.astype(BF16)


def _dsa_prep(zb, zf, qn, kn, wq, wqi, wuk, B, T):
    M = B * T
    H, dh, HI, dI, Dc, Rq = DSA_HEADS, DSA_DH, IDX_HEADS, IDX_DIM, DSA_KV_RANK, DSA_Q_RANK
    QB = min(Q_BLOCK, T)
    KB = KEY_BLOCK
    tm = min(DSA_PREP_ROWS, T)
    nb = tm // QB
    nt = T // tm
    kpt = tm // KB
    return pl.pallas_call(
        _dsa_prep_kernel,
        grid=(M // tm,),
        in_specs=[pl.BlockSpec((tm, Rq), lambda i: (i, ZB_CQ // Rq)),
                  pl.BlockSpec((tm, 2 * Dc), lambda i: (i, ZB_CKV // (2 * Dc))),
                  pl.BlockSpec((tm, LANES), lambda i: (i, ZF_IW // LANES)),
                  pl.BlockSpec((1, Rq), lambda i: (0, 0)),
                  pl.BlockSpec((1, Dc), lambda i: (0, 0)),
                  pl.BlockSpec((Rq, H * dh), lambda i: (0, 0)),
                  pl.BlockSpec((Rq, HI * dI), lambda i: (0, 0)),
                  pl.BlockSpec((H, Dc, dh), lambda i: (0, 0, 0))],
        out_specs=[pl.BlockSpec((nb, Dc, H * QB), lambda i: (i, 0, 0)),
                   pl.BlockSpec((nb, dI, HI * QB), lambda i: (i, 0, 0)),
                   pl.BlockSpec((nb, 1, HI * QB), lambda i: (i, 0, 0)),
                   pl.BlockSpec((tm, Dc), lambda i: (i, 0)),
                   pl.BlockSpec((1, kpt, Dc, KB), lambda i: (i // nt, i % nt, 0, 0)),
                   pl.BlockSpec((tm, dI), lambda i: (i, 0))],
        out_shape=[jax.ShapeDtypeStruct((M // QB, Dc, H * QB), BF16),
                   jax.ShapeDtypeStruct((M // QB, dI, HI * QB), BF16),
                   jax.ShapeDtypeStruct((M // QB, 1, HI * QB), F32),
                   jax.ShapeDtypeStruct((M, Dc), BF16),
                   jax.ShapeDtypeStruct((B, nt * kpt, Dc, KB), BF16),
                   jax.ShapeDtypeStruct((M, dI), BF16)],
        compiler_params=_params("parallel"),
        name="dsa_prep",
    )(zb, zb, zf, qn, kn, wq, wqi, wuk)


def _sortable_key(s):
    s = jnp.where(s == 0.0, 0.0, s)
    bits = pltpu.bitcast(s, I32)
    return bits ^ ((bits >> 31) & 0x7FFFFFFF)


def _dsa_attn_kernel(topk, n_keys, neg_key, qlt_ref, qit_ref, w_ref, ckv_ref, ckvt_ref, ik_ref, bias_ref,
                     wuvt_ref, o_ref, key_ref, z_ref, acc_ref, tie_ref):
    H, HI = DSA_HEADS, IDX_HEADS
    QB = o_ref.shape[0]
    KB = key_ref.shape[1]
    q0 = pl.program_id(1) * QB
    nkb = (q0 + QB - 1) // KB + 1
    n_skip = n_keys - nkb * KB

    s_row = lax.broadcasted_iota(I32, (KB, QB), 0)
    t_row = q0 + lax.broadcasted_iota(I32, (1, QB), 1)

    qit = qit_ref[0]
    w_row = w_ref[0]

    def score_blk(kb, carry):
        k0 = pl.multiple_of(kb * KB, KB)
        p = _dot(ik_ref[pl.ds(k0, KB), :], qit)
        p = jnp.maximum(p, 0.0) * w_row
        s = p[:, 0:QB]
        for h in range(1, HI):
            s = s + p[:, h * QB:(h + 1) * QB]
        s = jnp.where(k0 + s_row <= t_row, s, NEG_BIG)
        key_ref[kb] = _sortable_key(s)
        return carry

    lax.fori_loop(0, nkb, score_blk, 0)

    def count(hit_fn):
        def body(kb, acc):
            return acc + _group_sum(jnp.where(hit_fn(kb, key_ref[kb]), 1, 0))

        acc = lax.fori_loop(0, nkb, body, jnp.zeros((SUBLANES, QB), I32))
        return jnp.sum(acc, axis=0, keepdims=True)

    def count_ge(cand):
        return count(lambda kb, kk: kk >= cand) + jnp.where(cand <= neg_key, n_skip, 0)

    zero = jnp.zeros((1, QB), I32)
    cnt0 = count_ge(zero)
    state = (jnp.where(cnt0 >= topk, zero, jnp.full((1, QB), INT_MIN, I32)),
             jnp.where(cnt0 >= topk, cnt0, n_keys))

    def bisect(it, state):
        thr, cnt = state
        cand = thr | jnp.left_shift(jnp.int32(1), 30 - it)
        c = count_ge(cand)
        return jnp.where(c >= topk, cand, thr), jnp.where(c >= topk, c, cnt)

    thr, cnt = lax.fori_loop(0, 31, bisect, state)

    nbits = max(1, (n_keys - 1).bit_length())
    tie_ref[...] = jnp.full(tie_ref.shape, (1 << nbits) - 1, I32)

    @pl.when(jnp.max(jnp.where(cnt > topk, 1, 0)) > 0)
    def _():
        n_gt = count(lambda kb, kk: kk > thr) + jnp.where(thr < neg_key, n_skip, 0)
        need = topk - n_gt

        def ibisect(it, p):
            cand = p | jnp.left_shift(jnp.int32(1), nbits - 1 - it)
            below = count(lambda kb, kk: (kk == thr) & (kb * KB + s_row < cand))
            return jnp.where(below < need, cand, p)

        p = lax.fori_loop(0, nbits, ibisect, jnp.zeros((1, QB), I32))
        tie_ref[...] = jnp.broadcast_to(p, tie_ref.shape)

    tie = tie_ref[0:1, :]

    qlt = qlt_ref[0]

    FAR_TILE = bias_ref.shape[0] - 1
    n_far = jnp.maximum((q0 - FAR_TILE * QB) // KB + 1, 0)
    far_bias = bias_ref[FAR_TILE, 0:1, :]

    def logits_blk(near, kb, m8):
        k0 = pl.multiple_of(kb * KB, KB)
        kk = key_ref[kb]
        s_glob = k0 + s_row
        sel = ((kk > thr) | ((kk == thr) & (s_glob <= tie))) & (s_glob <= t_row)
        z = _dot(ckv_ref[pl.ds(k0, KB), :], qlt)
        if near:
            z = z + bias_ref[(q0 - k0) // QB]
        tops = []
        for h in range(H):
            zh = jnp.where(sel, z[:, h * QB:(h + 1) * QB], NEG_BIG)
            z_ref[kb, :, h * QB:(h + 1) * QB] = zh
            tops.append(_group_max(zh))
        return jnp.maximum(m8, jnp.concatenate(tops, axis=1))

    neg8 = jnp.full((SUBLANES, H * QB), NEG_BIG, F32)
    m8_far = lax.fori_loop(0, n_far, functools.partial(logits_blk, False), neg8)
    m8_near = lax.fori_loop(n_far, nkb, functools.partial(logits_blk, True), neg8)
    m = jnp.maximum(jnp.max(m8_far, axis=0, keepdims=True) + far_bias,
                    jnp.max(m8_near, axis=0, keepdims=True))
    m_far = m - far_bias
    acc_ref[...] = jnp.zeros(acc_ref.shape, F32)

    def pv_blk(kb, l8):
        p = jnp.exp(z_ref[kb] - jnp.where(kb < n_far, m_far, m))
        acc_ref[...] += _dot(ckvt_ref[0, kb], p.astype(BF16))
        return l8 + _group_sum(p)

    l8 = lax.fori_loop(0, nkb, pv_blk, jnp.zeros((SUBLANES, H * QB), F32))
    inv_l = 1.0 / jnp.sum(l8, axis=0, keepdims=True)
    o_lat_t = (acc_ref[...] * inv_l).astype(BF16)
    y_t = jnp.concatenate([_dot(wuvt_ref[h], o_lat_t[:, h * QB:(h + 1) * QB]) for h in range(H)], axis=0)
    o_ref[...] = y_t.T.astype(BF16)


def _dsa_attn(qlt, qit, w_row, ckvn, ckvt, idxk, bias_tiles, wuvt, B, T):
    H, HI, dI, Dc, dh = DSA_HEADS, IDX_HEADS, IDX_DIM, DSA_KV_RANK, DSA_DH
    QB = min(Q_BLOCK, T)
    KB = KEY_BLOCK
    nq = T // QB
    nk = T // KB
    topk = min(DSA_TOPK_MAX, T // 4)
    neg_key = int(np.array(NEG_BIG, np.float32).view(np.int32))
    neg_key = neg_key ^ ((neg_key >> 31) & 0x7FFFFFFF)
    return pl.pallas_call(
        functools.partial(_dsa_attn_kernel, topk, T, neg_key),
        grid=(B, nq),
        in_specs=[pl.BlockSpec((1, Dc, H * QB), lambda b, i: (b * nq + i, 0, 0)),
                  pl.BlockSpec((1, dI, HI * QB), lambda b, i: (b * nq + i, 0, 0)),
                  pl.BlockSpec((1, 1, HI * QB), lambda b, i: (b * nq + i, 0, 0)),
                  pl.BlockSpec((T, Dc), lambda b, i: (b, 0)),
                  pl.BlockSpec((1, nk, Dc, KB), lambda b, i: (b, 0, 0, 0)),
                  pl.BlockSpec((T, dI), lambda b, i: (b, 0)),
                  pl.BlockSpec((4, KB, H * QB), lambda b, i: (0, 0, 0), pipeline_mode=pl.Buffered(1)),
                  pl.BlockSpec((H, dh, Dc), lambda b, i: (0, 0, 0))],
        out_specs=pl.BlockSpec((QB, DSA_W), lambda b, i: (b * nq + i, 0)),
        out_shape=jax.ShapeDtypeStruct((B * T, DSA_W), BF16),
        scratch_shapes=[pltpu.VMEM((nk, KB, QB), I32),
                        pltpu.VMEM((nk, KB, H * QB), F32),
                        pltpu.VMEM((Dc, H * QB), F32),
                        pltpu.VMEM((SUBLANES, QB), I32)],
        compiler_params=_params("parallel", "arbitrary"),
        name="dsa_attn",
    )(qlt, qit, w_row, ckvn, ckvt, idxk, bias_tiles, wuvt)


def _rel_bias_tiles(rel_bias, QB, KB):
    max_exact = REL_BUCKETS // 2
    n_far = 3 * QB
    assert n_far - (KB - 1) > REL_MAX_DIST
    n = jnp.arange(n_far, dtype=jnp.int32)
    nf = jnp.maximum(n, max_exact).astype(F32)
    large = max_exact + (jnp.log(nf / max_exact) / math.log(REL_MAX_DIST / max_exact)
                         * (REL_BUCKETS - max_exact)).astype(jnp.int32)
    large = jnp.minimum(large, REL_BUCKETS - 1)
    bucket = jnp.where(n < max_exact, n, large)
    H = rel_bias.shape[1]
    tab = jnp.take(rel_bias, bucket, axis=0).T
    P = KB + QB
    tiles = []
    for delta in (0, QB, 2 * QB):
        d = np.concatenate([np.arange(QB + 1), np.arange(-(KB - 1), 0)])
        g = tab[:, np.clip(delta + d, 0, n_far - 1)]
        rows = jnp.tile(g, (1, KB))[:, :KB * (P - 1)].reshape(H, KB, P - 1)[:, :, :QB]
        tiles.append(rows.transpose(1, 0, 2).reshape(KB, H * QB))
    tiles.append(jnp.broadcast_to(jnp.repeat(rel_bias[REL_BUCKETS - 1], QB)[None, :], (KB, H * QB)))
    return jnp.stack(tiles).astype(F32)


def _merge_kernel(x_ref, yr_ref, yd_ref, yh_ref, lnpre_ref, wg_ref, wr_ref, wd_ref, wh_ref, wo_ref,
                  ln_ref, o_ref):
    D = x_ref.shape[1]
    x = x_ref[...]
    h = _rms(x, lnpre_ref[...]).astype(BF16)
    m = None
    for k, (y_ref, w_ref) in enumerate(((yr_ref, wr_ref), (yd_ref, wd_ref), (yh_ref, wh_ref))):
        gate = _sigmoid(_dot(h, wg_ref[:, k * D:(k + 1) * D]))
        term = gate * _dot(y_ref[...], w_ref[...])
        m = term if m is None else m + term
    u = _dot(m.astype(BF16), wo_ref[...])
    o_ref[...] = x + _rms(u, ln_ref[...])


def _merge(x2, y_ret, y_dsa, y_hg, lnpre, wg, wr, wd, wh, wo, ln):
    M = x2.shape[0]
    tm = min(1024, M)
    D = D_MODEL
    row = lambda w: pl.BlockSpec((tm, w), lambda i: (i, 0))
    const = lambda r, c: pl.BlockSpec((r, c), lambda i: (0, 0))
    return pl.pallas_call(
        _merge_kernel,
        grid=(M // tm,),
        in_specs=[row(D), row(RET_W), row(DSA_W), row(HGRN_W), const(1, D), const(D, N_BRANCH * D),
                  const(RET_W, D), const(DSA_W, D), const(HGRN_W, D), const(D, D), const(1, D)],
        out_specs=row(D),
        out_shape=jax.ShapeDtypeStruct((M, D), F32),
        compiler_params=_params("parallel"),
        name="merge",
    )(x2, y_ret, y_dsa, y_hg, lnpre, wg, wr, wd, wh, wo, ln)


def _gelu_tanh(x):
    return 0.5 * x * (1.0 + jnp.tanh(math.sqrt(2.0 / math.pi) * (x + 0.044715 * (x * x * x))))


def _ffn_kernel(tiles_per_seq, x_ref, lnpre_ref, wup_ref, cw_ref, cb_ref, wdn_ref, lnpost_ref,
                o_ref, buf_ref, prev_ref):
    tm = x_ref.shape[0]
    HALO = SUBLANES
    fc = buf_ref.shape[1]
    n_pass = D_FF // fc
    first = (pl.program_id(0) % tiles_per_seq) == 0

    @pl.when(first)
    def _():
        prev_ref[...] = jnp.zeros_like(prev_ref)

    x = x_ref[...]
    h = _rms(x, lnpre_ref[...]).astype(BF16)

    def conv(part, c):
        col = part * D_FF + c * fc
        up = _dot(h, wup_ref[:, col:col + fc])
        slot = part * n_pass + c
        buf_ref[0:HALO, :] = prev_ref[slot]
        buf_ref[HALO:HALO + tm, :] = up
        prev_ref[slot] = up[tm - HALO:tm, :]
        w = cw_ref[:, col:col + fc]
        y = (up * w[2:3] + buf_ref[HALO - 1:HALO - 1 + tm, :] * w[1:2]
             + buf_ref[HALO - 2:HALO - 2 + tm, :] * w[0:1])
        return y + cb_ref[:, col:col + fc]

    acc = None
    for c in range(n_pass):
        a = conv(0, c)
        u = conv(1, c)
        act = (_gelu_tanh(a) * u).astype(BF16)
        d = _dot(act, wdn_ref[c * fc:(c + 1) * fc, :])
        acc = d if acc is None else acc + d
    o_ref[...] = x + _rms(acc, lnpost_ref[...])


def _ffn(x2, lnpre, wup, cw, cb, wdn, lnpost, T):
    M = x2.shape[0]
    D = D_MODEL
    tm = min(512, T)
    fc = D_FF
    const = lambda r, c: pl.BlockSpec((r, c), lambda i: (0, 0))
    return pl.pallas_call(
        functools.partial(_ffn_kernel, T // tm),
        grid=(M // tm,),
        in_specs=[pl.BlockSpec((tm, D), lambda i: (i, 0)), const(1, D), const(D, 2 * D_FF),
                  const(CONV_WIDTH, 2 * D_FF), const(1, 2 * D_FF), const(D_FF, D), const(1, D)],
        out_specs=pl.BlockSpec((tm, D), lambda i: (i, 0)),
        out_shape=jax.ShapeDtypeStruct((M, D), F32),
        scratch_shapes=[pltpu.VMEM((tm + SUBLANES, fc), F32),
                        pltpu.VMEM((2 * (D_FF // fc), SUBLANES, fc), F32)],
        compiler_params=_params("arbitrary"),
        name="conv_ffn",
    )(x2, lnpre, wup, cw, cb, wdn, lnpost)


def _split_w_in(w):
    widths = (256, 256, 512, 512, 256, 128, 64, 8, 512, 512, 512, 512, N_BRANCH * D_MODEL)
    offs = np.concatenate([[0], np.cumsum(widths)])
    rq, rk, rv, rg, cq, ckv, ik, iw, hq, hf, hv, hg, gt = [w[..., offs[k]:offs[k + 1]] for k in range(len(widths))]
    zeros = lambda n: jnp.zeros(w.shape[:-1] + (n,), w.dtype)
    wb = jnp.concatenate([rq, rk, rv, cq, ckv, ik, zeros(2 * DSA_KV_RANK - DSA_KV_RANK - IDX_DIM), hq, hv], axis=-1)
    wf = jnp.concatenate([rg, hf, hg, iw, zeros(LANES - IDX_HEADS)], axis=-1)
    assert wb.shape[-1] == ZB_W and wf.shape[-1] == ZF_W
    return wb.astype(BF16), wf.astype(BF16), gt.astype(BF16)


def kernel(x, rel_bias, hgrn_lb, ln_mix_pre, ln_mix_post, ln_ffn_pre, ln_ffn_post, w_in, dsa_q_norm, dsa_kv_norm, dsa_w_uq, dsa_w_uk, dsa_w_uv, hgrn_norm, w_br_ret, w_br_dsa, w_br_hgrn, w_out, ffn_w_up, ffn_conv_w, ffn_conv_b, ffn_w_down):
    B, T, D = x.shape
    depth = w_in.shape[0]
    assert T % KEY_BLOCK == 0 and D == D_MODEL
    H, dh, Rq = DSA_HEADS, DSA_DH, DSA_Q_RANK
    bias_tiles = _rel_bias_tiles(rel_bias, min(Q_BLOCK, T), KEY_BLOCK)
    wq = dsa_w_uq[..., :H * dh].astype(BF16)
    wqi = dsa_w_uq[..., H * dh:].astype(BF16)
    wuk = dsa_w_uk.transpose(0, 1, 3, 2).astype(BF16)
    wuvt = dsa_w_uv.transpose(0, 1, 3, 2).astype(BF16)
    x2 = x.reshape(B * T, D)
    row = lambda v: v.reshape(1, -1)
    for l in range(depth):
        wb, wf, wg = _split_w_in(w_in[l])
        zb, zf = _inproj(x2, row(ln_mix_pre[l]), wb, wf)
        y_ret = _retention(zb, zf, B, T)
        y_hg = _hgrn(zb, zf, hgrn_lb, row(hgrn_norm[l]), l, B, T)
        qlt, qit, w_row, ckvn, ckvt, idxk = _dsa_prep(zb, zf, row(dsa_q_norm[l]), row(dsa_kv_norm[l]),
                                                      wq[l], wqi[l], wuk[l], B, T)
        y_dsa = _dsa_attn(qlt, qit, w_row, ckvn, ckvt, idxk, bias_tiles, wuvt[l], B, T)
        x2 = _merge(x2, y_ret, y_dsa, y_hg, row(ln_mix_pre[l]), wg, w_br_ret[l].astype(BF16),
                    w_br_dsa[l].astype(BF16), w_br_hgrn[l].astype(BF16), w_out[l].astype(BF16),
                    row(ln_mix_post[l]))
        x2 = _ffn(x2, row(ln_ffn_pre[l]), ffn_w_up[l].astype(BF16), ffn_conv_w[l], row(ffn_conv_b[l]),
                  ffn_w_down[l].astype(BF16), row(ln_ffn_post[l]), T)
    return x2.reshape(B, T, D)
```

```python
import functools
import math

import jax
import jax.numpy as jnp
import numpy as np
from jax import lax
from jax.experimental import pallas as pl
from jax.experimental.pallas import tpu as pltpu

F32 = jnp.float32
BF16 = jnp.bfloat16
I32 = jnp.int32

D_MODEL = 1024
RET_HEADS, RET_DK, RET_DV, RET_CHUNK = 4, 64, 128, 128
DSA_HEADS, DSA_DH, DSA_Q_RANK, DSA_KV_RANK = 8, 64, 256, 128
IDX_HEADS, IDX_DIM, DSA_TOPK_MAX, Q_BLOCK = 8, 64, 256, 256
HGRN_HEADS, HGRN_EXPAND, HGRN_DV = 4, 128, 128
F_FLOOR = 1e-6
REL_BUCKETS, REL_MAX_DIST = 32, 128
D_FF = 2816
CONV_WIDTH = 3
N_BRANCH = 3
EPS = 1e-6
NEG_BIG = -1e30

RET_W = RET_HEADS * RET_DV
DSA_W = DSA_HEADS * DSA_DH
HGRN_KW = HGRN_HEADS * HGRN_EXPAND
HGRN_W = HGRN_HEADS * HGRN_DV

ZB_RQ, ZB_RK, ZB_RV, ZB_CQ = 0, 256, 512, 1024
ZB_CKV = 1280
ZB_HQ, ZB_HV = 1536, 2048
ZB_W = 2560
ZF_RG, ZF_HF, ZF_HG, ZF_IW = 0, 512, 1024, 1536
ZF_W = 1664

VMEM_LIMIT_BYTES = 56 * 1024 * 1024
SUBLANES = 8
LANES = 128

KEY_BLOCK = 256
DSA_PREP_ROWS = 512
HGRN_PAIR_CHUNK = 128
INT_MIN = -(2 ** 31)


def _params(*sem):
    return pltpu.CompilerParams(dimension_semantics=sem, vmem_limit_bytes=VMEM_LIMIT_BYTES)


def _dot(a, b):
    return jnp.dot(a, b, preferred_element_type=F32)


def _dot_nt(a, b):
    return lax.dot_general(a, b, (((1,), (1,)), ((), ())), preferred_element_type=F32)


def _dot_tn(a, b):
    return lax.dot_general(a, b, (((0,), (0,)), ((), ())), preferred_element_type=F32)


def _sigmoid(x):
    return 1.0 / (1.0 + jnp.exp(-x))


def _sigmoid_pair(x):
    t = jnp.exp(-jnp.abs(x))
    big = 1.0 / (1.0 + t)
    small = t * big
    pos = x >= 0.0
    return jnp.where(pos, big, small), jnp.where(pos, small, big)


def _rms(x, g):
    return x * lax.rsqrt(jnp.mean(x * x, axis=-1, keepdims=True) + EPS) * g


def _group_sum(x):
    return jnp.sum(x.reshape(x.shape[0] // SUBLANES, SUBLANES, x.shape[1]), axis=0)


def _group_max(x):
    return jnp.max(x.reshape(x.shape[0] // SUBLANES, SUBLANES, x.shape[1]), axis=0)


def _inproj_kernel(x_ref, g_ref, wb_ref, wf_ref, zb_ref, zf_ref):
    h = _rms(x_ref[...], g_ref[...]).astype(BF16)
    zb_ref[...] = _dot(h, wb_ref[...]).astype(BF16)
    zf_ref[...] = _dot(h, wf_ref[...])


def _inproj(x2, g, wb, wf):
    M = x2.shape[0]
    tm = min(1024, M)
    const = lambda r, c: pl.BlockSpec((r, c), lambda i: (0, 0))
    return pl.pallas_call(
        _inproj_kernel,
        grid=(M // tm,),
        in_specs=[pl.BlockSpec((tm, D_MODEL), lambda i: (i, 0)), const(1, D_MODEL),
                  const(D_MODEL, ZB_W), const(D_MODEL, ZF_W)],
        out_specs=[pl.BlockSpec((tm, ZB_W), lambda i: (i, 0)), pl.BlockSpec((tm, ZF_W), lambda i: (i, 0))],
        out_shape=[jax.ShapeDtypeStruct((M, ZB_W), BF16), jax.ShapeDtypeStruct((M, ZF_W), F32)],
        compiler_params=_params("parallel"),
        name="inproj",
    )(x2, g, wb, wf)


def _ret_kernel(gam_ref, q_ref, k_ref, v_ref, g_ref, cos_ref, sin_ref, dm_ref, xi_ref, zeta_ref,
                o_ref, r_ref):
    C = q_ref.shape[0]
    H, dk, dv = RET_HEADS, RET_DK, RET_DV

    @pl.when(pl.program_id(1) == 0)
    def _():
        r_ref[...] = jnp.zeros_like(r_ref)

    cos = cos_ref[...]
    sin = sin_ref[...]
    lane = lax.broadcasted_iota(I32, (C, H * dk), 1)
    first_half = (lane % dk) < (dk // 2)

    def rot(x):
        swapped = jnp.where(first_half, pltpu.roll(x, H * dk - dk // 2, 1), pltpu.roll(x, dk // 2, 1))
        return x * cos + swapped * sin

    q = rot(q_ref[...].astype(F32))
    k = rot(k_ref[...].astype(F32)) * dk ** -0.5
    for h in range(H):
        qh = q[:, h * dk:(h + 1) * dk].astype(BF16)
        kh = k[:, h * dk:(h + 1) * dk]
        vh = v_ref[:, h * dv:(h + 1) * dv]
        s = _dot_nt(qh, kh.astype(BF16)) * dm_ref[h]
        inner = _dot(s.astype(BF16), vh)
        rh = r_ref[h]
        cross = _dot(qh, rh.astype(BF16)) * xi_ref[h]
        r_ref[h] = gam_ref[h] * rh + _dot_tn((kh * zeta_ref[h]).astype(BF16), vh)
        o = inner + cross
        mu = jnp.mean(o, axis=-1, keepdims=True)
        oc = o - mu
        var = jnp.mean(oc * oc, axis=-1, keepdims=True)
        gh = g_ref[:, h * dv:(h + 1) * dv]
        o_ref[:, h * dv:(h + 1) * dv] = (gh * _sigmoid(gh) * (oc * lax.rsqrt(var + EPS))).astype(BF16)


def _retention(zb, zf, B, T):
    H, dk, dv = RET_HEADS, RET_DK, RET_DV
    C = min(RET_CHUNK, T)
    n = T // C
    pos = np.arange(T, dtype=np.float64)
    half = dk // 2
    freq = 1.0 / (10000.0 ** np.linspace(0.0, 1.0, half))
    ang = pos[:, None] * freq[None, :]
    cos = jnp.asarray(np.tile(np.cos(ang), (1, 2 * H)), F32)
    sin = jnp.asarray(np.tile(np.concatenate([-np.sin(ang), np.sin(ang)], axis=1), (1, H)), F32)
    log_gamma = np.log1p(-(2.0 ** (-5.0 - np.arange(H, dtype=np.float64))))
    i = np.arange(C, dtype=np.float64)
    rel = i[:, None] - i[None, :]
    dmask = jnp.asarray(np.where(rel >= 0, np.exp(np.maximum(rel, 0.0)[None] * log_gamma[:, None, None]), 0.0), F32)
    xi = jnp.asarray(np.exp((i + 1.0)[None, :] * log_gamma[:, None])[:, :, None], F32)
    zeta = jnp.asarray(np.exp((C - 1.0 - i)[None, :] * log_gamma[:, None])[:, :, None], F32)
    gamma_c = jnp.asarray(np.exp(C * log_gamma), F32)

    wq = H * dk
    wv = H * dv
    full = lambda shape: pl.BlockSpec(shape, lambda b, c: (0,) * len(shape))
    return pl.pallas_call(
        _ret_kernel,
        grid=(B, n),
        in_specs=[pl.BlockSpec(memory_space=pltpu.SMEM),
                  pl.BlockSpec((C, wq), lambda b, c: (b * n + c, ZB_RQ // wq)),
                  pl.BlockSpec((C, wq), lambda b, c: (b * n + c, ZB_RK // wq)),
                  pl.BlockSpec((C, wv), lambda b, c: (b * n + c, ZB_RV // wv)),
                  pl.BlockSpec((C, wv), lambda b, c: (b * n + c, ZF_RG // wv)),
                  pl.BlockSpec((C, wq), lambda b, c: (c, 0)),
                  pl.BlockSpec((C, wq), lambda b, c: (c, 0)),
                  full((H, C, C)), full((H, C, 1)), full((H, C, 1))],
        out_specs=pl.BlockSpec((C, wv), lambda b, c: (b * n + c, 0)),
        out_shape=jax.ShapeDtypeStruct((B * T, wv), BF16),
        scratch_shapes=[pltpu.VMEM((H, dk, dv), F32)],
        compiler_params=_params("parallel", "arbitrary"),
        name="retention",
    )(gamma_c, zb, zb, zb, zf, cos, sin, dmask, xi, zeta)


def _hgrn_levels(C):
    ms, m = [], C // 2
    while m >= SUBLANES:
        ms.append(m)
        m //= 2
    return ms


def _cumsum_rows(tri, x):
    hi = x.astype(BF16)
    rest = x - hi.astype(F32)
    mid = rest.astype(BF16)
    lo = (rest - mid.astype(F32)).astype(BF16)
    return _dot(tri, hi) + (_dot(tri, mid) + _dot(tri, lo))


def _hgrn_kernel(layer, q_ref, f_ref, v_ref, g_ref, lbraw_ref, gain_ref, tri_ref, lmask_ref, o_ref, st_ref):
    TR = q_ref.shape[0]
    C = tri_ref.shape[0]
    H, dk, dv = HGRN_HEADS, HGRN_EXPAND, HGRN_DV
    SB = SUBLANES

    @pl.when(pl.program_id(1) == 0)
    def _():
        st_ref[...] = jnp.zeros_like(st_ref)

    raw = lbraw_ref[...]
    e = jnp.exp(raw - jnp.max(raw, axis=0, keepdims=True))
    soft = e / jnp.sum(e, axis=0, keepdims=True)
    cs = soft[0:1]
    for l in range(1, layer + 1):
        cs = cs + soft[l:l + 1]
    lb = jnp.clip(cs - soft[0:1], 0.0, 1.0)
    tri = tri_ref[...]
    row_in_blk = lax.broadcasted_iota(I32, (SB, dk), 0)
    lane_t = lax.broadcasted_iota(I32, (SB, C), 1)
    levels = _hgrn_levels(C)

    def chunk(c, carry):
        r0 = pl.multiple_of(c * C, C)
        sig_pos, sig_neg = _sigmoid_pair(f_ref[pl.ds(r0, C), :])
        f = lb + (1.0 - lb) * sig_pos
        log_f = jnp.log(jnp.maximum(f, F_FLOOR))
        kk = (1.0 - lb) * sig_neg
        b = _cumsum_rows(tri, log_f)
        qq = q_ref[pl.ds(r0, C), :].astype(F32)
        vb = v_ref[pl.ds(r0, C), :]
        eb = jnp.exp(b)
        b_last = b[C - 1:C, :]
        eb_last = eb[C - 1:C, :]
        q_dec = (qq * eb).astype(BF16)
        k_dec = (kk * jnp.exp(b_last - b)).astype(BF16)
        outs = []
        for h in range(H):
            sl = slice(h * dk, (h + 1) * dk)
            bh, qh, kh, vh = b[:, sl], qq[:, sl], kk[:, sl], vb[:, h * dv:(h + 1) * dv]
            diag = []
            for blk in range(C // SB):
                bs = bh[blk * SB:(blk + 1) * SB, :]
                ks = kh[blk * SB:(blk + 1) * SB, :]
                at = jnp.zeros((SB, C), F32)
                for tt in range(SB):
                    t = blk * SB + tt
                    diff = jnp.where(row_in_blk <= tt, bh[t:t + 1, :] - bs, NEG_BIG)
                    p = (qh[t:t + 1, :] * jnp.exp(diff)) * ks
                    at = jnp.where(lane_t == t, jnp.sum(p, axis=-1, keepdims=True), at)
                diag.append(at)
            a_t = jnp.concatenate(diag, axis=0)
            for lev, m in enumerate(levels):
                qs, ks = [], []
                for blk in range(C // m):
                    rows = slice(blk * m, (blk + 1) * m)
                    if blk % 2 == 1:
                        ref = bh[blk * m - 1:blk * m, :]
                        qs.append(qh[rows] * jnp.exp(bh[rows] - ref))
                        ks.append(jnp.zeros((m, dk), F32))
                    else:
                        ref = bh[(blk + 1) * m - 1:(blk + 1) * m, :]
                        ks.append(kh[rows] * jnp.exp(ref - bh[rows]))
                        qs.append(jnp.zeros((m, dk), F32))
                q_l = jnp.concatenate(qs, axis=0).astype(BF16)
                k_l = jnp.concatenate(ks, axis=0).astype(BF16)
                a_t = a_t + _dot_nt(k_l, q_l) * lmask_ref[lev]
            intra = _dot_tn(a_t.astype(BF16), vh)
            st = st_ref[h]
            outs.append(intra + _dot_nt(q_dec[:, sl], st.astype(BF16)))
            st_ref[h] = st * eb_last[:, sl] + _dot_tn(vh, k_dec[:, sl])
        o = _rms(jnp.concatenate(outs, axis=1), gain_ref[...])
        gg = g_ref[pl.ds(r0, C), :]
        o_ref[pl.ds(r0, C), :] = (gg * _sigmoid(gg) * o).astype(BF16)
        return carry

    lax.fori_loop(0, TR // C, chunk, 0)


def _hgrn(zb, zf, hgrn_lb, gain, layer, B, T):
    H, dk, dv = HGRN_HEADS, HGRN_EXPAND, HGRN_DV
    C = min(HGRN_PAIR_CHUNK, T)
    TR = min(512, T)
    n = T // TR
    w = H * dk
    L = hgrn_lb.shape[0]
    tri = jnp.tril(jnp.ones((C, C), BF16))
    idx = np.arange(C)
    lmask = np.stack([(((idx[None, :] // m) % 2 == 1) & (idx[:, None] // m == idx[None, :] // m - 1))
                      for m in _hgrn_levels(C)]).astype(np.float32)
    return pl.pallas_call(
        functools.partial(_hgrn_kernel, layer),
        grid=(B, n),
        in_specs=[pl.BlockSpec((TR, w), lambda b, c: (b * n + c, ZB_HQ // w)),
                  pl.BlockSpec((TR, w), lambda b, c: (b * n + c, ZF_HF // w)),
                  pl.BlockSpec((TR, w), lambda b, c: (b * n + c, ZB_HV // w)),
                  pl.BlockSpec((TR, w), lambda b, c: (b * n + c, ZF_HG // w)),
                  pl.BlockSpec((L, w), lambda b, c: (0, 0)),
                  pl.BlockSpec((1, w), lambda b, c: (0, 0)),
                  pl.BlockSpec((C, C), lambda b, c: (0, 0)),
                  pl.BlockSpec(lmask.shape, lambda b, c: (0, 0, 0))],
        out_specs=pl.BlockSpec((TR, w), lambda b, c: (b * n + c, 0)),
        out_shape=jax.ShapeDtypeStruct((B * T, w), BF16),
        scratch_shapes=[pltpu.VMEM((H, dv, dk), F32)],
        compiler_params=_params("parallel", "arbitrary"),
        name="hgrn2",
    )(zb, zf, zb, zf, hgrn_lb, gain, tri, jnp.asarray(lmask))


def _dsa_prep_kernel(cq_ref, kv_ref, iw_ref, qn_ref, kn_ref, wq_ref, wqi_ref, wukt_ref,
                     qlt_ref, qit_ref, w_ref, ckv_ref, ckvt_ref, ik_ref):
    tm = cq_ref.shape[0]
    H, dh, HI, dI, Dc = DSA_HEADS, DSA_DH, IDX_HEADS, IDX_DIM, DSA_KV_RANK
    QB = qlt_ref.shape[2] // H
    nb = tm // QB
    cq = _rms(cq_ref[...].astype(F32), qn_ref[...]).astype(BF16)
    ckv = _rms(kv_ref[:, :Dc].astype(F32), kn_ref[...])
    ckv_ref[...] = ckv.astype(BF16)
    KB = ckvt_ref.shape[3]
    for j in range(tm // KB):
        ckvt_ref[0, j, 0:Dc, :] = ckv[j * KB:(j + 1) * KB].T.astype(BF16)
        ckvt_ref[0, j, Dc:Dc + SUBLANES, :] = jnp.ones((SUBLANES, KB), BF16)
    ik_ref[...] = kv_ref[:, Dc:Dc + dI]
    w_t = iw_ref[...].T[0:HI, :] * (HI * dI) ** -0.5
    qi_all = _dot(cq, wqi_ref[...])
    q_all = _dot(cq, wq_ref[...])
    for j in range(nb):
        rows = slice(j * QB, (j + 1) * QB)
        qi_t = qi_all[rows].T
        q_t = q_all[rows].T.astype(BF16)
        for h in range(HI):
            qit_ref[j, :, h * QB:(h + 1) * QB] = qi_t[h * dI:(h + 1) * dI].astype(BF16)
            w_ref[j, :, h * QB:(h + 1) * QB] = w_t[h:h + 1, rows]
        for h in range(H):
            ql_t = _dot(wukt_ref[h], q_t[h * dh:(h + 1) * dh]) * dh ** -0.5
            qlt_ref[j, :, h * QB:(h + 1) * QB] = ql_t.astype(BF16)


def _dsa_prep(zb, zf, qn, kn, wq, wqi, wuk, B, T):
    M = B * T
    H, dh, HI, dI, Dc, Rq = DSA_HEADS, DSA_DH, IDX_HEADS, IDX_DIM, DSA_KV_RANK, DSA_Q_RANK
    QB = min(Q_BLOCK, T)
    KB = KEY_BLOCK
    tm = min(DSA_PREP_ROWS, T)
    nb = tm // QB
    nt = T // tm
    kpt = tm // KB
    return pl.pallas_call(
        _dsa_prep_kernel,
        grid=(M // tm,),
        in_specs=[pl.BlockSpec((tm, Rq), lambda i: (i, ZB_CQ // Rq)),
                  pl.BlockSpec((tm, 2 * Dc), lambda i: (i, ZB_CKV // (2 * Dc))),
                  pl.BlockSpec((tm, LANES), lambda i: (i, ZF_IW // LANES)),
                  pl.BlockSpec((1, Rq), lambda i: (0, 0)),
                  pl.BlockSpec((1, Dc), lambda i: (0, 0)),
                  pl.BlockSpec((Rq, H * dh), lambda i: (0, 0)),
                  pl.BlockSpec((Rq, HI * dI), lambda i: (0, 0)),
                  pl.BlockSpec((H, Dc, dh), lambda i: (0, 0, 0))],
        out_specs=[pl.BlockSpec((nb, Dc, H * QB), lambda i: (i, 0, 0)),
                   pl.BlockSpec((nb, dI, HI * QB), lambda i: (i, 0, 0)),
                   pl.BlockSpec((nb, 1, HI * QB), lambda i: (i, 0, 0)),
                   pl.BlockSpec((tm, Dc), lambda i: (i, 0)),
                   pl.BlockSpec((1, kpt, Dc + SUBLANES, KB), lambda i: (i // nt, i % nt, 0, 0)),
                   pl.BlockSpec((tm, dI), lambda i: (i, 0))],
        out_shape=[jax.ShapeDtypeStruct((M // QB, Dc, H * QB), BF16),
                   jax.ShapeDtypeStruct((M // QB, dI, HI * QB), BF16),
                   jax.ShapeDtypeStruct((M // QB, 1, HI * QB), F32),
                   jax.ShapeDtypeStruct((M, Dc), BF16),
                   jax.ShapeDtypeStruct((B, nt * kpt, Dc + SUBLANES, KB), BF16),
                   jax.ShapeDtypeStruct((M, dI), BF16)],
        compiler_params=_params("parallel"),
        name="dsa_prep",
    )(zb, zb, zf, qn, kn, wq, wqi, wuk)


def _sortable_key(s):
    s = jnp.where(s == 0.0, 0.0, s)
    bits = pltpu.bitcast(s, I32)
    return bits ^ ((bits >> 31) & 0x7FFFFFFF)


def _dsa_attn_kernel(topk, n_keys, neg_key, qlt_ref, qit_ref, w_ref, ckv_ref, ckvt_ref, ik_ref, bias_ref,
                     wuvt_ref, o_ref, key_ref, z_ref, acc_ref, tie_ref):
    H, HI = DSA_HEADS, IDX_HEADS
    QB = o_ref.shape[0]
    KB = key_ref.shape[1]
    q0 = pl.program_id(1) * QB
    nkb = (q0 + QB - 1) // KB + 1
    n_skip = n_keys - nkb * KB

    s_row = lax.broadcasted_iota(I32, (KB, QB), 0)
    t_row = q0 + lax.broadcasted_iota(I32, (1, QB), 1)

    qit = qit_ref[0]
    w_row = w_ref[0]

    def score_blk(causal_edge, kb, carry):
        k0 = pl.multiple_of(kb * KB, KB)
        p = _dot(ik_ref[pl.ds(k0, KB), :], qit)
        p = jnp.maximum(p, 0.0) * w_row
        s = p[:, 0:QB]
        for h in range(1, HI):
            s = s + p[:, h * QB:(h + 1) * QB]
        if causal_edge:
            s = jnp.where(k0 + s_row <= t_row, s, NEG_BIG)
        key_ref[kb] = _sortable_key(s)
        return carry

    n_past = q0 // KB
    lax.fori_loop(0, n_past, functools.partial(score_blk, False), 0)
    lax.fori_loop(n_past, nkb, functools.partial(score_blk, True), 0)

    def count(hit_fn):
        def body(kb, acc):
            return acc + _group_sum(jnp.where(hit_fn(kb, key_ref[kb]), 1, 0))

        acc = lax.fori_loop(0, nkb, body, jnp.zeros((SUBLANES, QB), I32))
        return jnp.sum(acc, axis=0, keepdims=True)

    def count_ge(cand):
        return count(lambda kb, kk: kk >= cand) + jnp.where(cand <= neg_key, n_skip, 0)

    zero = jnp.zeros((1, QB), I32)
    cnt0 = count_ge(zero)
    state = (jnp.where(cnt0 >= topk, zero, jnp.full((1, QB), INT_MIN, I32)),
             jnp.where(cnt0 >= topk, cnt0, n_keys))

    def bisect(it, state):
        thr, cnt = state
        cand = thr | jnp.left_shift(jnp.int32(1), 30 - it)
        c = count_ge(cand)
        return jnp.where(c >= topk, cand, thr), jnp.where(c >= topk, c, cnt)

    thr, cnt = lax.fori_loop(0, 31, bisect, state)

    nbits = max(1, (n_keys - 1).bit_length())
    tie_ref[...] = jnp.full(tie_ref.shape, (1 << nbits) - 1, I32)

    @pl.when(jnp.max(jnp.where(cnt > topk, 1, 0)) > 0)
    def _():
        n_gt = count(lambda kb, kk: kk > thr) + jnp.where(thr < neg_key, n_skip, 0)
        need = topk - n_gt

        def ibisect(it, p):
            cand = p | jnp.left_shift(jnp.int32(1), nbits - 1 - it)
            below = count(lambda kb, kk: (kk == thr) & (kb * KB + s_row < cand))
            return jnp.where(below < need, cand, p)

        p = lax.fori_loop(0, nbits, ibisect, jnp.zeros((1, QB), I32))
        tie_ref[...] = jnp.broadcast_to(p, tie_ref.shape)

    tie = tie_ref[0:1, :]

    qlt = qlt_ref[0]

    FAR_TILE = bias_ref.shape[0] - 1
    n_far = jnp.maximum((q0 - FAR_TILE * QB) // KB + 1, 0)
    far_bias = bias_ref[FAR_TILE, 0:1, :]

    def logits_blk(near, kb, m8):
        k0 = pl.multiple_of(kb * KB, KB)
        kk = key_ref[kb]
        s_glob = k0 + s_row
        sel = (kk > thr) | ((kk == thr) & (s_glob <= tie))
        z = _dot(ckv_ref[pl.ds(k0, KB), :], qlt)
        if near:
            sel = sel & (s_glob <= t_row)
            z = z + bias_ref[(q0 - k0) // QB]
        tops = []
        for h in range(H):
            zh = jnp.where(sel, z[:, h * QB:(h + 1) * QB], NEG_BIG)
            z_ref[kb, :, h * QB:(h + 1) * QB] = zh
            tops.append(_group_max(zh))
        return jnp.maximum(m8, jnp.concatenate(tops, axis=1))

    neg8 = jnp.full((SUBLANES, H * QB), NEG_BIG, F32)
    m8_far = lax.fori_loop(0, n_far, functools.partial(logits_blk, False), neg8)
    m8_near = lax.fori_loop(n_far, nkb, functools.partial(logits_blk, True), neg8)
    m = jnp.maximum(jnp.max(m8_far, axis=0, keepdims=True) + far_bias,
                    jnp.max(m8_near, axis=0, keepdims=True))
    m_far = m - far_bias
    acc_ref[...] = jnp.zeros(acc_ref.shape, F32)

    def pv_blk(kb, carry):
        p = jnp.exp(z_ref[kb] - jnp.where(kb < n_far, m_far, m))
        acc_ref[...] += _dot(ckvt_ref[0, kb], p.astype(BF16))
        return carry

    lax.fori_loop(0, nkb, pv_blk, 0)
    Dc = acc_ref.shape[0] - SUBLANES
    inv_l = 1.0 / acc_ref[Dc:Dc + 1, :]
    o_lat_t = (acc_ref[0:Dc, :] * inv_l).astype(BF16)
    y_t = jnp.concatenate([_dot(wuvt_ref[h], o_lat_t[:, h * QB:(h + 1) * QB]) for h in range(H)], axis=0)
    o_ref[...] = y_t.T.astype(BF16)


def _dsa_attn(qlt, qit, w_row, ckvn, ckvt, idxk, bias_tiles, wuvt, B, T):
    H, HI, dI, Dc, dh = DSA_HEADS, IDX_HEADS, IDX_DIM, DSA_KV_RANK, DSA_DH
    QB = min(Q_BLOCK, T)
    KB = KEY_BLOCK
    nq = T // QB
    nk = T // KB
    topk = min(DSA_TOPK_MAX, T // 4)
    neg_key = int(np.array(NEG_BIG, np.float32).view(np.int32))
    neg_key = neg_key ^ ((neg_key >> 31) & 0x7FFFFFFF)
    return pl.pallas_call(
        functools.partial(_dsa_attn_kernel, topk, T, neg_key),
        grid=(B, nq),
        in_specs=[pl.BlockSpec((1, Dc, H * QB), lambda b, i: (b * nq + i, 0, 0)),
                  pl.BlockSpec((1, dI, HI * QB), lambda b, i: (b * nq + i, 0, 0)),
                  pl.BlockSpec((1, 1, HI * QB), lambda b, i: (b * nq + i, 0, 0)),
                  pl.BlockSpec((T, Dc), lambda b, i: (b, 0)),
                  pl.BlockSpec((1, nk, Dc + SUBLANES, KB), lambda b, i: (b, 0, 0, 0)),
                  pl.BlockSpec((T, dI), lambda b, i: (b, 0)),
                  pl.BlockSpec((4, KB, H * QB), lambda b, i: (0, 0, 0), pipeline_mode=pl.Buffered(1)),
                  pl.BlockSpec((H, dh, Dc), lambda b, i: (0, 0, 0))],
        out_specs=pl.BlockSpec((QB, DSA_W), lambda b, i: (b * nq + i, 0)),
        out_shape=jax.ShapeDtypeStruct((B * T, DSA_W), BF16),
        scratch_shapes=[pltpu.VMEM((nk, KB, QB), I32),
                        pltpu.VMEM((nk, KB, H * QB), F32),
                        pltpu.VMEM((Dc + SUBLANES, H * QB), F32),
                        pltpu.VMEM((SUBLANES, QB), I32)],
        compiler_params=_params("parallel", "arbitrary"),
        name="dsa_attn",
    )(qlt, qit, w_row, ckvn, ckvt, idxk, bias_tiles, wuvt)


def _rel_bias_tiles(rel_bias, QB, KB):
    max_exact = REL_BUCKETS // 2
    n_far = 3 * QB
    assert n_far - (KB - 1) > REL_MAX_DIST
    n = jnp.arange(n_far, dtype=jnp.int32)
    nf = jnp.maximum(n, max_exact).astype(F32)
    large = max_exact + (jnp.log(nf / max_exact) / math.log(REL_MAX_DIST / max_exact)
                         * (REL_BUCKETS - max_exact)).astype(jnp.int32)
    large = jnp.minimum(large, REL_BUCKETS - 1)
    bucket = jnp.where(n < max_exact, n, large)
    H = rel_bias.shape[1]
    tab = jnp.take(rel_bias, bucket, axis=0).T
    P = KB + QB
    tiles = []
    for delta in (0, QB, 2 * QB):
        d = np.concatenate([np.arange(QB + 1), np.arange(-(KB - 1), 0)])
        g = tab[:, np.clip(delta + d, 0, n_far - 1)]
        rows = jnp.tile(g, (1, KB))[:, :KB * (P - 1)].reshape(H, KB, P - 1)[:, :, :QB]
        tiles.append(rows.transpose(1, 0, 2).reshape(KB, H * QB))
    tiles.append(jnp.broadcast_to(jnp.repeat(rel_bias[REL_BUCKETS - 1], QB)[None, :], (KB, H * QB)))
    return jnp.stack(tiles).astype(F32)


def _merge_kernel(x_ref, yr_ref, yd_ref, yh_ref, lnpre_ref, wg_ref, wr_ref, wd_ref, wh_ref, wo_ref,
                  ln_ref, o_ref):
    D = x_ref.shape[1]
    x = x_ref[...]
    h = _rms(x, lnpre_ref[...]).astype(BF16)
    m = None
    for k, (y_ref, w_ref) in enumerate(((yr_ref, wr_ref), (yd_ref, wd_ref), (yh_ref, wh_ref))):
        gate = _sigmoid(_dot(h, wg_ref[:, k * D:(k + 1) * D]))
        term = gate * _dot(y_ref[...], w_ref[...])
        m = term if m is None else m + term
    u = _dot(m.astype(BF16), wo_ref[...])
    o_ref[...] = x + _rms(u, ln_ref[...])


def _merge(x2, y_ret, y_dsa, y_hg, lnpre, wg, wr, wd, wh, wo, ln):
    M = x2.shape[0]
    tm = min(1024, M)
    D = D_MODEL
    row = lambda w: pl.BlockSpec((tm, w), lambda i: (i, 0))
    const = lambda r, c: pl.BlockSpec((r, c), lambda i: (0, 0))
    return pl.pallas_call(
        _merge_kernel,
        grid=(M // tm,),
        in_specs=[row(D), row(RET_W), row(DSA_W), row(HGRN_W), const(1, D), const(D, N_BRANCH * D),
                  const(RET_W, D), const(DSA_W, D), const(HGRN_W, D), const(D, D), const(1, D)],
        out_specs=row(D),
        out_shape=jax.ShapeDtypeStruct((M, D), F32),
        compiler_params=_params("parallel"),
        name="merge",
    )(x2, y_ret, y_dsa, y_hg, lnpre, wg, wr, wd, wh, wo, ln)


def _gelu_tanh(x):
    return 0.5 * x * (1.0 + jnp.tanh(math.sqrt(2.0 / math.pi) * (x + 0.044715 * (x * x * x))))


def _ffn_kernel(tiles_per_seq, x_ref, lnpre_ref, wup_ref, cw_ref, cb_ref, wdn_ref, lnpost_ref,
                o_ref, buf_ref, prev_ref):
    tm = x_ref.shape[0]
    HALO = SUBLANES
    fc = buf_ref.shape[1]
    n_pass = D_FF // fc
    first = (pl.program_id(0) % tiles_per_seq) == 0

    @pl.when(first)
    def _():
        prev_ref[...] = jnp.zeros_like(prev_ref)

    x = x_ref[...]
    h = _rms(x, lnpre_ref[...]).astype(BF16)

    def conv(part, c):
        col = part * D_FF + c * fc
        up = _dot(h, wup_ref[:, col:col + fc])
        slot = part * n_pass + c
        buf_ref[0:HALO, :] = prev_ref[slot]
        buf_ref[HALO:HALO + tm, :] = up
        prev_ref[slot] = up[tm - HALO:tm, :]
        w = cw_ref[:, col:col + fc]
        y = (up * w[2:3] + buf_ref[HALO - 1:HALO - 1 + tm, :] * w[1:2]
             + buf_ref[HALO - 2:HALO - 2 + tm, :] * w[0:1])
        return y + cb_ref[:, col:col + fc]

    acc = None
    for c in range(n_pass):
        a = conv(0, c)
        u = conv(1, c)
        act = (_gelu_tanh(a) * u).astype(BF16)
        d = _dot(act, wdn_ref[c * fc:(c + 1) * fc, :])
        acc = d if acc is None else acc + d
    o_ref[...] = x + _rms(acc, lnpost_ref[...])


def _ffn(x2, lnpre, wup, cw, cb, wdn, lnpost, T):
    M = x2.shape[0]
    D = D_MODEL
    tm = min(512, T)
    fc = D_FF
    const = lambda r, c: pl.BlockSpec((r, c), lambda i: (0, 0))
    return pl.pallas_call(
        functools.partial(_ffn_kernel, T // tm),
        grid=(M // tm,),
        in_specs=[pl.BlockSpec((tm, D), lambda i: (i, 0)), const(1, D), const(D, 2 * D_FF),
                  const(CONV_WIDTH, 2 * D_FF), const(1, 2 * D_FF), const(D_FF, D), const(1, D)],
        out_specs=pl.BlockSpec((tm, D), lambda i: (i, 0)),
        out_shape=jax.ShapeDtypeStruct((M, D), F32),
        scratch_shapes=[pltpu.VMEM((tm + SUBLANES, fc), F32),
                        pltpu.VMEM((2 * (D_FF // fc), SUBLANES, fc), F32)],
        compiler_params=_params("arbitrary"),
        name="conv_ffn",
    )(x2, lnpre, wup, cw, cb, wdn, lnpost)


def _split_w_in(w):
    widths = (256, 256, 512, 512, 256, 128, 64, 8, 512, 512, 512, 512, N_BRANCH * D_MODEL)
    offs = np.concatenate([[0], np.cumsum(widths)])
    rq, rk, rv, rg, cq, ckv, ik, iw, hq, hf, hv, hg, gt = [w[..., offs[k]:offs[k + 1]] for k in range(len(widths))]
    zeros = lambda n: jnp.zeros(w.shape[:-1] + (n,), w.dtype)
    wb = jnp.concatenate([rq, rk, rv, cq, ckv, ik, zeros(2 * DSA_KV_RANK - DSA_KV_RANK - IDX_DIM), hq, hv], axis=-1)
    wf = jnp.concatenate([rg, hf, hg, iw, zeros(LANES - IDX_HEADS)], axis=-1)
    assert wb.shape[-1] == ZB_W and wf.shape[-1] == ZF_W
    return wb.astype(BF16), wf.astype(BF16), gt.astype(BF16)


def kernel(x, rel_bias, hgrn_lb, ln_mix_pre, ln_mix_post, ln_ffn_pre, ln_ffn_post, w_in, dsa_q_norm, dsa_kv_norm, dsa_w_uq, dsa_w_uk, dsa_w_uv, hgrn_norm, w_br_ret, w_br_dsa, w_br_hgrn, w_out, ffn_w_up, ffn_conv_w, ffn_conv_b, ffn_w_down):
    B, T, D = x.shape
    depth = w_in.shape[0]
    assert T % KEY_BLOCK == 0 and D == D_MODEL
    H, dh, Rq = DSA_HEADS, DSA_DH, DSA_Q_RANK
    bias_tiles = _rel_bias_tiles(rel_bias, min(Q_BLOCK, T), KEY_BLOCK)
    wq = dsa_w_uq[..., :H * dh].astype(BF16)
    wqi = dsa_w_uq[..., H * dh:].astype(BF16)
    wuk = dsa_w_uk.transpose(0, 1, 3, 2).astype(BF16)
    wuvt = dsa_w_uv.transpose(0, 1, 3, 2).astype(BF16)
    x2 = x.reshape(B * T, D)
    row = lambda v: v.reshape(1, -1)
    for l in range(depth):
        wb, wf, wg = _split_w_in(w_in[l])
        zb, zf = _inproj(x2, row(ln_mix_pre[l]), wb, wf)
        y_ret = _retention(zb, zf, B, T)
        y_hg = _hgrn(zb, zf, hgrn_lb, row(hgrn_norm[l]), l, B, T)
        qlt, qit, w_row, ckvn, ckvt, idxk = _dsa_prep(zb, zf, row(dsa_q_norm[l]), row(dsa_kv_norm[l]),
                                                      wq[l], wqi[l], wuk[l], B, T)
        y_dsa = _dsa_attn(qlt, qit, w_row, ckvn, ckvt, idxk, bias_tiles, wuvt[l], B, T)
        x2 = _merge(x2, y_ret, y_dsa, y_hg, row(ln_mix_pre[l]), wg, w_br_ret[l].astype(BF16),
                    w_br_dsa[l].astype(BF16), w_br_hgrn[l].astype(BF16), w_out[l].astype(BF16),
                    row(ln_mix_post[l]))
        x2 = _ffn(x2, row(ln_ffn_pre[l]), ffn_w_up[l].astype(BF16), ffn_conv_w[l], row(ffn_conv_b[l]),
                  ffn_w_down[l].astype(BF16), row(ln_ffn_post[l]), T)
    return x2.reshape(B, T, D)
```

```python
import functools
import math

import jax
import jax.numpy as jnp
import numpy as np
from jax import lax
from jax.experimental import pallas as pl
from jax.experimental.pallas import tpu as pltpu

F32 = jnp.float32
BF16 = jnp.bfloat16
I32 = jnp.int32

D_MODEL = 1024
RET_HEADS, RET_DK, RET_DV, RET_CHUNK = 4, 64, 128, 128
DSA_HEADS, DSA_DH, DSA_Q_RANK, DSA_KV_RANK = 8, 64, 256, 128
IDX_HEADS, IDX_DIM, DSA_TOPK_MAX = 8, 64, 256
Q_BLOCK = 256
HGRN_HEADS, HGRN_EXPAND, HGRN_DV = 4, 128, 128
F_FLOOR = 1e-6
REL_BUCKETS, REL_MAX_DIST = 32, 128
D_FF = 2816
CONV_WIDTH = 3
N_BRANCH = 3
EPS = 1e-6
NEG_BIG = -1e30

RET_W = RET_HEADS * RET_DV
DSA_W = DSA_HEADS * DSA_DH
HGRN_KW = HGRN_HEADS * HGRN_EXPAND
HGRN_W = HGRN_HEADS * HGRN_DV

ZB_RQ, ZB_RK, ZB_RV, ZB_CQ = 0, 256, 512, 1024
ZB_CKV = 1280
ZB_HQ, ZB_HV = 1536, 2048
ZB_W = 2560
ZF_RG, ZF_HF, ZF_HG, ZF_IW = 0, 512, 1024, 1536
ZF_W = 1664

VMEM_LIMIT_BYTES = 56 * 1024 * 1024
SUBLANES = 8
LANES = 128

KEY_BLOCK = 256
DSA_PREP_ROWS = 512
PROJ_ROWS = 1024
FFN_ROWS = 512
HGRN_ROWS = 512
HGRN_PAIR_CHUNK = 256
INT_MIN = -(2 ** 31)


def _params(*sem):
    return pltpu.CompilerParams(dimension_semantics=sem, vmem_limit_bytes=VMEM_LIMIT_BYTES)


def _dot(a, b):
    return jnp.dot(a, b, preferred_element_type=F32)


def _dot_nt(a, b):
    return lax.dot_general(a, b, (((1,), (1,)), ((), ())), preferred_element_type=F32)


def _dot_tn(a, b):
    return lax.dot_general(a, b, (((0,), (0,)), ((), ())), preferred_element_type=F32)


def _sigmoid(x):
    return 1.0 / (1.0 + jnp.exp(-x))


def _sigmoid_pair(x):
    t = jnp.exp(-jnp.abs(x))
    big = 1.0 / (1.0 + t)
    small = t * big
    pos = x >= 0.0
    return jnp.where(pos, big, small), jnp.where(pos, small, big)


def _rms(x, g):
    return x * lax.rsqrt(jnp.mean(x * x, axis=-1, keepdims=True) + EPS) * g


def _group_sum(x):
    return jnp.sum(x.reshape(x.shape[0] // SUBLANES, SUBLANES, x.shape[1]), axis=0)


def _group_max(x):
    return jnp.max(x.reshape(x.shape[0] // SUBLANES, SUBLANES, x.shape[1]), axis=0)


def _inproj_kernel(x_ref, g_ref, wb_ref, wf_ref, zb_ref, zf_ref):
    h = _rms(x_ref[...], g_ref[...]).astype(BF16)
    zb_ref[...] = _dot(h, wb_ref[...]).astype(BF16)
    zf_ref[...] = _dot(h, wf_ref[...])


def _inproj(x2, g, wb, wf):
    M = x2.shape[0]
    tm = min(PROJ_ROWS, M)
    const = lambda r, c: pl.BlockSpec((r, c), lambda i: (0, 0))
    return pl.pallas_call(
        _inproj_kernel,
        grid=(M // tm,),
        in_specs=[pl.BlockSpec((tm, D_MODEL), lambda i: (i, 0)), const(1, D_MODEL),
                  const(D_MODEL, ZB_W), const(D_MODEL, ZF_W)],
        out_specs=[pl.BlockSpec((tm, ZB_W), lambda i: (i, 0)), pl.BlockSpec((tm, ZF_W), lambda i: (i, 0))],
        out_shape=[jax.ShapeDtypeStruct((M, ZB_W), BF16), jax.ShapeDtypeStruct((M, ZF_W), F32)],
        compiler_params=_params("parallel"),
        name="inproj",
    )(x2, g, wb, wf)


def _ret_kernel(gam_ref, q_ref, k_ref, v_ref, g_ref, cos_ref, sin_ref, dm_ref, xi_ref, zeta_ref,
                o_ref, r_ref):
    C = q_ref.shape[0]
    H, dk, dv = RET_HEADS, RET_DK, RET_DV

    @pl.when(pl.program_id(1) == 0)
    def _():
        r_ref[...] = jnp.zeros_like(r_ref)

    cos = cos_ref[...]
    sin = sin_ref[...]
    lane = lax.broadcasted_iota(I32, (C, H * dk), 1)
    first_half = (lane % dk) < (dk // 2)

    def rot(x):
        swapped = jnp.where(first_half, pltpu.roll(x, H * dk - dk // 2, 1), pltpu.roll(x, dk // 2, 1))
        return x * cos + swapped * sin

    q = rot(q_ref[...].astype(F32))
    k = rot(k_ref[...].astype(F32)) * dk ** -0.5
    for h in range(H):
        qh = q[:, h * dk:(h + 1) * dk].astype(BF16)
        kh = k[:, h * dk:(h + 1) * dk]
        vh = v_ref[:, h * dv:(h + 1) * dv]
        s = _dot_nt(qh, kh.astype(BF16)) * dm_ref[h]
        inner = _dot(s.astype(BF16), vh)
        rh = r_ref[h]
        cross = _dot(qh, rh.astype(BF16)) * xi_ref[h]
        r_ref[h] = gam_ref[h] * rh + _dot_tn((kh * zeta_ref[h]).astype(BF16), vh)
        o = inner + cross
        mu = jnp.mean(o, axis=-1, keepdims=True)
        oc = o - mu
        var = jnp.mean(oc * oc, axis=-1, keepdims=True)
        gh = g_ref[:, h * dv:(h + 1) * dv]
        o_ref[:, h * dv:(h + 1) * dv] = (gh * _sigmoid(gh) * (oc * lax.rsqrt(var + EPS))).astype(BF16)


def _retention(zb, zf, B, T):
    H, dk, dv = RET_HEADS, RET_DK, RET_DV
    C = min(RET_CHUNK, T)
    n = T // C
    pos = np.arange(T, dtype=np.float64)
    half = dk // 2
    freq = 1.0 / (10000.0 ** np.linspace(0.0, 1.0, half))
    ang = pos[:, None] * freq[None, :]
    cos = jnp.asarray(np.tile(np.cos(ang), (1, 2 * H)), F32)
    sin = jnp.asarray(np.tile(np.concatenate([-np.sin(ang), np.sin(ang)], axis=1), (1, H)), F32)
    log_gamma = np.log1p(-(2.0 ** (-5.0 - np.arange(H, dtype=np.float64))))
    i = np.arange(C, dtype=np.float64)
    rel = i[:, None] - i[None, :]
    dmask = jnp.asarray(np.where(rel >= 0, np.exp(np.maximum(rel, 0.0)[None] * log_gamma[:, None, None]), 0.0), F32)
    xi = jnp.asarray(np.exp((i + 1.0)[None, :] * log_gamma[:, None])[:, :, None], F32)
    zeta = jnp.asarray(np.exp((C - 1.0 - i)[None, :] * log_gamma[:, None])[:, :, None], F32)
    gamma_c = jnp.asarray(np.exp(C * log_gamma), F32)

    wq = H * dk
    wv = H * dv
    full = lambda shape: pl.BlockSpec(shape, lambda b, c: (0,) * len(shape))
    return pl.pallas_call(
        _ret_kernel,
        grid=(B, n),
        in_specs=[pl.BlockSpec(memory_space=pltpu.SMEM),
                  pl.BlockSpec((C, wq), lambda b, c: (b * n + c, ZB_RQ // wq)),
                  pl.BlockSpec((C, wq), lambda b, c: (b * n + c, ZB_RK // wq)),
                  pl.BlockSpec((C, wv), lambda b, c: (b * n + c, ZB_RV // wv)),
                  pl.BlockSpec((C, wv), lambda b, c: (b * n + c, ZF_RG // wv)),
                  pl.BlockSpec((C, wq), lambda b, c: (c, 0)),
                  pl.BlockSpec((C, wq), lambda b, c: (c, 0)),
                  full((H, C, C)), full((H, C, 1)), full((H, C, 1))],
        out_specs=pl.BlockSpec((C, wv), lambda b, c: (b * n + c, 0)),
        out_shape=jax.ShapeDtypeStruct((B * T, wv), BF16),
        scratch_shapes=[pltpu.VMEM((H, dk, dv), F32)],
        compiler_params=_params("parallel", "arbitrary"),
        name="retention",
    )(gamma_c, zb, zb, zb, zf, cos, sin, dmask, xi, zeta)


def _hgrn_levels(C):
    ms, m = [], C // 2
    while m >= SUBLANES:
        ms.append(m)
        m //= 2
    return ms


def _cumsum_rows(tri, x):
    hi = x.astype(BF16)
    rest = x - hi.astype(F32)
    mid = rest.astype(BF16)
    lo = (rest - mid.astype(F32)).astype(BF16)
    return _dot(tri, hi) + (_dot(tri, mid) + _dot(tri, lo))


def _hgrn_kernel(layer, q_ref, f_ref, v_ref, g_ref, lbraw_ref, gain_ref, tri_ref, lmask_ref, o_ref, st_ref):
    TR = q_ref.shape[0]
    C = tri_ref.shape[0]
    H, dk, dv = HGRN_HEADS, HGRN_EXPAND, HGRN_DV
    SB = SUBLANES

    @pl.when(pl.program_id(1) == 0)
    def _():
        st_ref[...] = jnp.zeros_like(st_ref)

    raw = lbraw_ref[...]
    e = jnp.exp(raw - jnp.max(raw, axis=0, keepdims=True))
    soft = e / jnp.sum(e, axis=0, keepdims=True)
    cs = soft[0:1]
    for l in range(1, layer + 1):
        cs = cs + soft[l:l + 1]
    lb = jnp.clip(cs - soft[0:1], 0.0, 1.0)
    tri = tri_ref[...]
    row_in_blk = lax.broadcasted_iota(I32, (SB, dk), 0)
    lane_t = lax.broadcasted_iota(I32, (SB, C), 1)
    levels = _hgrn_levels(C)

    def chunk(c, carry):
        r0 = pl.multiple_of(c * C, C)
        sig_pos, sig_neg = _sigmoid_pair(f_ref[pl.ds(r0, C), :])
        f = lb + (1.0 - lb) * sig_pos
        log_f = jnp.log(jnp.maximum(f, F_FLOOR))
        kk = (1.0 - lb) * sig_neg
        b = _cumsum_rows(tri, log_f)
        qq = q_ref[pl.ds(r0, C), :].astype(F32)
        vb = v_ref[pl.ds(r0, C), :]
        eb = jnp.exp(b)
        b_last = b[C - 1:C, :]
        eb_last = eb[C - 1:C, :]
        q_dec = (qq * eb).astype(BF16)
        k_dec = (kk * jnp.exp(b_last - b)).astype(BF16)
        outs = []
        for h in range(H):
            sl = slice(h * dk, (h + 1) * dk)
            bh, qh, kh, vh = b[:, sl], qq[:, sl], kk[:, sl], vb[:, h * dv:(h + 1) * dv]
            diag = []
            for blk in range(C // SB):
                bs = bh[blk * SB:(blk + 1) * SB, :]
                ks = kh[blk * SB:(blk + 1) * SB, :]
                at = jnp.zeros((SB, C), F32)
                for tt in range(SB):
                    t = blk * SB + tt
                    diff = jnp.where(row_in_blk <= tt, bh[t:t + 1, :] - bs, NEG_BIG)
                    p = (qh[t:t + 1, :] * jnp.exp(diff)) * ks
                    at = jnp.where(lane_t == t, jnp.sum(p, axis=-1, keepdims=True), at)
                diag.append(at)
            a_t = jnp.concatenate(diag, axis=0)
            for lev, m in enumerate(levels):
                qs, ks = [], []
                for blk in range(C // m):
                    rows = slice(blk * m, (blk + 1) * m)
                    if blk % 2 == 1:
                        ref = bh[blk * m - 1:blk * m, :]
                        qs.append(qh[rows] * jnp.exp(bh[rows] - ref))
                        ks.append(jnp.zeros((m, dk), F32))
                    else:
                        ref = bh[(blk + 1) * m - 1:(blk + 1) * m, :]
                        ks.append(kh[rows] * jnp.exp(ref - bh[rows]))
                        qs.append(jnp.zeros((m, dk), F32))
                q_l = jnp.concatenate(qs, axis=0).astype(BF16)
                k_l = jnp.concatenate(ks, axis=0).astype(BF16)
                a_t = a_t + _dot_nt(k_l, q_l) * lmask_ref[lev]
            intra = _dot_tn(a_t.astype(BF16), vh)
            st = st_ref[h]
            outs.append(intra + _dot_nt(q_dec[:, sl], st.astype(BF16)))
            st_ref[h] = st * eb_last[:, sl] + _dot_tn(vh, k_dec[:, sl])
        o = _rms(jnp.concatenate(outs, axis=1), gain_ref[...])
        gg = g_ref[pl.ds(r0, C), :]
        o_ref[pl.ds(r0, C), :] = (gg * _sigmoid(gg) * o).astype(BF16)
        return carry

    lax.fori_loop(0, TR // C, chunk, 0)


def _hgrn(zb, zf, hgrn_lb, gain, layer, B, T):
    H, dk, dv = HGRN_HEADS, HGRN_EXPAND, HGRN_DV
    C = min(HGRN_PAIR_CHUNK, T)
    TR = min(HGRN_ROWS, T)
    n = T // TR
    w = H * dk
    L = hgrn_lb.shape[0]
    tri = jnp.tril(jnp.ones((C, C), BF16))
    idx = np.arange(C)
    lmask = np.stack([(((idx[None, :] // m) % 2 == 1) & (idx[:, None] // m == idx[None, :] // m - 1))
                      for m in _hgrn_levels(C)]).astype(np.float32)
    return pl.pallas_call(
        functools.partial(_hgrn_kernel, layer),
        grid=(B, n),
        in_specs=[pl.BlockSpec((TR, w), lambda b, c: (b * n + c, ZB_HQ // w)),
                  pl.BlockSpec((TR, w), lambda b, c: (b * n + c, ZF_HF // w)),
                  pl.BlockSpec((TR, w), lambda b, c: (b * n + c, ZB_HV // w)),
                  pl.BlockSpec((TR, w), lambda b, c: (b * n + c, ZF_HG // w)),
                  pl.BlockSpec((L, w), lambda b, c: (0, 0)),
                  pl.BlockSpec((1, w), lambda b, c: (0, 0)),
                  pl.BlockSpec((C, C), lambda b, c: (0, 0)),
                  pl.BlockSpec(lmask.shape, lambda b, c: (0, 0, 0))],
        out_specs=pl.BlockSpec((TR, w), lambda b, c: (b * n + c, 0)),
        out_shape=jax.ShapeDtypeStruct((B * T, w), BF16),
        scratch_shapes=[pltpu.VMEM((H, dv, dk), F32)],
        compiler_params=_params("parallel", "arbitrary"),
        name="hgrn2",
    )(zb, zf, zb, zf, hgrn_lb, gain, tri, jnp.asarray(lmask))


def _dsa_prep_kernel(cq_ref, kv_ref, iw_ref, qn_ref, kn_ref, wq_ref, wqi_ref, wukt_ref,
                     qlt_ref, qit_ref, w_ref, ckv_ref, ckvt_ref, ik_ref):
    tm = cq_ref.shape[0]
    H, dh, HI, dI, Dc = DSA_HEADS, DSA_DH, IDX_HEADS, IDX_DIM, DSA_KV_RANK
    QB = qlt_ref.shape[2] // H
    nb = tm // QB
    cq = _rms(cq_ref[...].astype(F32), qn_ref[...]).astype(BF16)
    ckv = _rms(kv_ref[:, :Dc].astype(F32), kn_ref[...])
    ckv_ref[...] = ckv.astype(BF16)
    KB = ckvt_ref.shape[3]
    for j in range(tm // KB):
        ckvt_ref[0, j, 0:Dc, :] = ckv[j * KB:(j + 1) * KB].T.astype(BF16)
        ckvt_ref[0, j, Dc:Dc + SUBLANES, :] = jnp.ones((SUBLANES, KB), BF16)
    ik_ref[...] = kv_ref[:, Dc:Dc + dI]
    w_t = iw_ref[...].T[0:HI, :] * (HI * dI) ** -0.5
    qi_all = _dot(cq, wqi_ref[...])
    q_all = _dot(cq, wq_ref[...])
    for j in range(nb):
        rows = slice(j * QB, (j + 1) * QB)
        qi_t = qi_all[rows].T
        q_t = q_all[rows].T.astype(BF16)
        for h in range(HI):
            qit_ref[j, :, h * QB:(h + 1) * QB] = qi_t[h * dI:(h + 1) * dI].astype(BF16)
            w_ref[j, :, h * QB:(h + 1) * QB] = w_t[h:h + 1, rows]
        for h in range(H):
            ql_t = _dot(wukt_ref[h], q_t[h * dh:(h + 1) * dh]) * dh ** -0.5
            qlt_ref[j, :, h * QB:(h + 1) * QB] = ql_t.astype(BF16)


def _dsa_prep(zb, zf, qn, kn, wq, wqi, wuk, B, T):
    M = B * T
    H, dh, HI, dI, Dc, Rq = DSA_HEADS, DSA_DH, IDX_HEADS, IDX_DIM, DSA_KV_RANK, DSA_Q_RANK
    QB = min(Q_BLOCK, T)
    KB = KEY_BLOCK
    tm = min(DSA_PREP_ROWS, T)
    nb = tm // QB
    nt = T // tm
    kpt = tm // KB
    return pl.pallas_call(
        _dsa_prep_kernel,
        grid=(M // tm,),
        in_specs=[pl.BlockSpec((tm, Rq), lambda i: (i, ZB_CQ // Rq)),
                  pl.BlockSpec((tm, 2 * Dc), lambda i: (i, ZB_CKV // (2 * Dc))),
                  pl.BlockSpec((tm, LANES), lambda i: (i, ZF_IW // LANES)),
                  pl.BlockSpec((1, Rq), lambda i: (0, 0)),
                  pl.BlockSpec((1, Dc), lambda i: (0, 0)),
                  pl.BlockSpec((Rq, H * dh), lambda i: (0, 0)),
                  pl.BlockSpec((Rq, HI * dI), lambda i: (0, 0)),
                  pl.BlockSpec((H, Dc, dh), lambda i: (0, 0, 0))],
        out_specs=[pl.BlockSpec((nb, Dc, H * QB), lambda i: (i, 0, 0)),
                   pl.BlockSpec((nb, dI, HI * QB), lambda i: (i, 0, 0)),
                   pl.BlockSpec((nb, 1, HI * QB), lambda i: (i, 0, 0)),
                   pl.BlockSpec((tm, Dc), lambda i: (i, 0)),
                   pl.BlockSpec((1, kpt, Dc + SUBLANES, KB), lambda i: (i // nt, i % nt, 0, 0)),
                   pl.BlockSpec((tm, dI), lambda i: (i, 0))],
        out_shape=[jax.ShapeDtypeStruct((M // QB, Dc, H * QB), BF16),
                   jax.ShapeDtypeStruct((M // QB, dI, HI * QB), BF16),
                   jax.ShapeDtypeStruct((M // QB, 1, HI * QB), F32),
                   jax.ShapeDtypeStruct((M, Dc), BF16),
                   jax.ShapeDtypeStruct((B, nt * kpt, Dc + SUBLANES, KB), BF16),
                   jax.ShapeDtypeStruct((M, dI), BF16)],
        compiler_params=_params("parallel"),
        name="dsa_prep",
    )(zb, zb, zf, qn, kn, wq, wqi, wuk)


def _sortable_key(s):
    s = jnp.where(s == 0.0, 0.0, s)
    bits = pltpu.bitcast(s, I32)
    return bits ^ ((bits >> 31) & 0x7FFFFFFF)


def _dsa_attn_kernel(topk, n_keys, neg_key, qlt_ref, qit_ref, w_ref, ckv_ref, ckvt_ref, ik_ref, bias_ref,
                     wuvt_ref, o_ref, key_ref, z_ref, acc_ref, tie_ref):
    H, HI = DSA_HEADS, IDX_HEADS
    QB = o_ref.shape[0]
    KB = key_ref.shape[1]
    q0 = pl.program_id(1) * QB
    nkb = (q0 + QB - 1) // KB + 1
    n_skip = n_keys - nkb * KB

    s_row = lax.broadcasted_iota(I32, (KB, QB), 0)
    t_row = q0 + lax.broadcasted_iota(I32, (1, QB), 1)

    qit = qit_ref[0]
    w_row = w_ref[0]

    def score_blk(causal_edge, kb, carry):
        k0 = pl.multiple_of(kb * KB, KB)
        p = _dot(ik_ref[pl.ds(k0, KB), :], qit)
        p = jnp.maximum(p, 0.0) * w_row
        s = p[:, 0:QB]
        for h in range(1, HI):
            s = s + p[:, h * QB:(h + 1) * QB]
        if causal_edge:
            s = jnp.where(k0 + s_row <= t_row, s, NEG_BIG)
        key_ref[kb] = _sortable_key(s)
        return carry

    n_past = q0 // KB
    lax.fori_loop(0, n_past, functools.partial(score_blk, False), 0)
    lax.fori_loop(n_past, nkb, functools.partial(score_blk, True), 0)

    def count(hit_fn):
        def body(kb, acc):
            return acc + _group_sum(jnp.where(hit_fn(kb, key_ref[kb]), 1, 0))

        acc = lax.fori_loop(0, nkb, body, jnp.zeros((SUBLANES, QB), I32))
        return jnp.sum(acc, axis=0, keepdims=True)

    def count_ge(cand):
        return count(lambda kb, kk: kk >= cand) + jnp.where(cand <= neg_key, n_skip, 0)

    zero = jnp.zeros((1, QB), I32)
    cnt0 = count_ge(zero)
    state = (jnp.where(cnt0 >= topk, zero, jnp.full((1, QB), INT_MIN, I32)),
             jnp.where(cnt0 >= topk, cnt0, n_keys))

    def bisect(it, state):
        thr, cnt = state
        cand = thr | jnp.left_shift(jnp.int32(1), 30 - it)
        c = count_ge(cand)
        return jnp.where(c >= topk, cand, thr), jnp.where(c >= topk, c, cnt)

    thr, cnt = lax.fori_loop(0, 31, bisect, state)

    nbits = max(1, (n_keys - 1).bit_length())
    tie_ref[...] = jnp.full(tie_ref.shape, (1 << nbits) - 1, I32)

    @pl.when(jnp.max(jnp.where(cnt > topk, 1, 0)) > 0)
    def _():
        n_gt = count(lambda kb, kk: kk > thr) + jnp.where(thr < neg_key, n_skip, 0)
        need = topk - n_gt

        def ibisect(it, p):
            cand = p | jnp.left_shift(jnp.int32(1), nbits - 1 - it)
            below = count(lambda kb, kk: (kk == thr) & (kb * KB + s_row < cand))
            return jnp.where(below < need, cand, p)

        p = lax.fori_loop(0, nbits, ibisect, jnp.zeros((1, QB), I32))
        tie_ref[...] = jnp.broadcast_to(p, tie_ref.shape)

    tie = tie_ref[0:1, :]

    qlt = qlt_ref[0]

    FAR_TILE = bias_ref.shape[0] - 1
    n_far = jnp.maximum((q0 - FAR_TILE * QB) // KB + 1, 0)
    far_bias = bias_ref[FAR_TILE, 0:1, :]

    def logits_blk(near, kb, m8):
        k0 = pl.multiple_of(kb * KB, KB)
        kk = key_ref[kb]
        s_glob = k0 + s_row
        sel = (kk > thr) | ((kk == thr) & (s_glob <= tie))
        z = _dot(ckv_ref[pl.ds(k0, KB), :], qlt)
        if near:
            sel = sel & (s_glob <= t_row)
            z = z + bias_ref[(q0 - k0) // QB]
        tops = []
        for h in range(H):
            zh = jnp.where(sel, z[:, h * QB:(h + 1) * QB], NEG_BIG)
            z_ref[kb, :, h * QB:(h + 1) * QB] = zh
            tops.append(_group_max(zh))
        return jnp.maximum(m8, jnp.concatenate(tops, axis=1))

    neg8 = jnp.full((SUBLANES, H * QB), NEG_BIG, F32)
    m8_far = lax.fori_loop(0, n_far, functools.partial(logits_blk, False), neg8)
    m8_near = lax.fori_loop(n_far, nkb, functools.partial(logits_blk, True), neg8)
    m = jnp.maximum(jnp.max(m8_far, axis=0, keepdims=True) + far_bias,
                    jnp.max(m8_near, axis=0, keepdims=True))
    m_far = m - far_bias
    acc_ref[...] = jnp.zeros(acc_ref.shape, F32)

    def pv_blk(kb, carry):
        p = jnp.exp((z_ref[kb] - jnp.where(kb < n_far, m_far, m)).astype(BF16))
        acc_ref[...] += _dot(ckvt_ref[0, kb], p)
        return carry

    lax.fori_loop(0, nkb, pv_blk, 0)
    Dc = acc_ref.shape[0] - SUBLANES
    inv_l = 1.0 / acc_ref[Dc:Dc + 1, :]
    o_lat_t = (acc_ref[0:Dc, :] * inv_l).astype(BF16)
    y_t = jnp.concatenate([_dot(wuvt_ref[h], o_lat_t[:, h * QB:(h + 1) * QB]) for h in range(H)], axis=0)
    o_ref[...] = y_t.T.astype(BF16)


def _dsa_attn(qlt, qit, w_row, ckvn, ckvt, idxk, bias_tiles, wuvt, B, T):
    H, HI, dI, Dc, dh = DSA_HEADS, IDX_HEADS, IDX_DIM, DSA_KV_RANK, DSA_DH
    QB = min(Q_BLOCK, T)
    KB = KEY_BLOCK
    nq = T // QB
    nk = T // KB
    topk = min(DSA_TOPK_MAX, T // 4)
    neg_key = int(np.array(NEG_BIG, np.float32).view(np.int32))
    neg_key = neg_key ^ ((neg_key >> 31) & 0x7FFFFFFF)
    return pl.pallas_call(
        functools.partial(_dsa_attn_kernel, topk, T, neg_key),
        grid=(B, nq),
        in_specs=[pl.BlockSpec((1, Dc, H * QB), lambda b, i: (b * nq + i, 0, 0)),
                  pl.BlockSpec((1, dI, HI * QB), lambda b, i: (b * nq + i, 0, 0)),
                  pl.BlockSpec((1, 1, HI * QB), lambda b, i: (b * nq + i, 0, 0)),
                  pl.BlockSpec((T, Dc), lambda b, i: (b, 0)),
                  pl.BlockSpec((1, nk, Dc + SUBLANES, KB), lambda b, i: (b, 0, 0, 0)),
                  pl.BlockSpec((T, dI), lambda b, i: (b, 0)),
                  pl.BlockSpec((4, KB, H * QB), lambda b, i: (0, 0, 0), pipeline_mode=pl.Buffered(1)),
                  pl.BlockSpec((H, dh, Dc), lambda b, i: (0, 0, 0))],
        out_specs=pl.BlockSpec((QB, DSA_W), lambda b, i: (b * nq + i, 0)),
        out_shape=jax.ShapeDtypeStruct((B * T, DSA_W), BF16),
        scratch_shapes=[pltpu.VMEM((nk, KB, QB), I32),
                        pltpu.VMEM((nk, KB, H * QB), F32),
                        pltpu.VMEM((Dc + SUBLANES, H * QB), F32),
                        pltpu.VMEM((SUBLANES, QB), I32)],
        compiler_params=_params("parallel", "arbitrary"),
        name="dsa_attn",
    )(qlt, qit, w_row, ckvn, ckvt, idxk, bias_tiles, wuvt)


def _rel_bias_tiles(rel_bias, QB, KB):
    max_exact = REL_BUCKETS // 2
    n_far = 3 * QB
    assert n_far - (KB - 1) > REL_MAX_DIST
    n = jnp.arange(n_far, dtype=jnp.int32)
    nf = jnp.maximum(n, max_exact).astype(F32)
    large = max_exact + (jnp.log(nf / max_exact) / math.log(REL_MAX_DIST / max_exact)
                         * (REL_BUCKETS - max_exact)).astype(jnp.int32)
    large = jnp.minimum(large, REL_BUCKETS - 1)
    bucket = jnp.where(n < max_exact, n, large)
    H = rel_bias.shape[1]
    tab = jnp.take(rel_bias, bucket, axis=0).T
    P = KB + QB
    tiles = []
    for delta in (0, QB, 2 * QB):
        d = np.concatenate([np.arange(QB + 1), np.arange(-(KB - 1), 0)])
        g = tab[:, np.clip(delta + d, 0, n_far - 1)]
        rows = jnp.tile(g, (1, KB))[:, :KB * (P - 1)].reshape(H, KB, P - 1)[:, :, :QB]
        tiles.append(rows.transpose(1, 0, 2).reshape(KB, H * QB))
    tiles.append(jnp.broadcast_to(jnp.repeat(rel_bias[REL_BUCKETS - 1], QB)[None, :], (KB, H * QB)))
    return jnp.stack(tiles).astype(F32)


def _merge_kernel(x_ref, yr_ref, yd_ref, yh_ref, lnpre_ref, wg_ref, wr_ref, wd_ref, wh_ref, wo_ref,
                  ln_ref, o_ref):
    D = x_ref.shape[1]
    x = x_ref[...]
    h = _rms(x, lnpre_ref[...]).astype(BF16)
    m = None
    for k, (y_ref, w_ref) in enumerate(((yr_ref, wr_ref), (yd_ref, wd_ref), (yh_ref, wh_ref))):
        gate = _sigmoid(_dot(h, wg_ref[:, k * D:(k + 1) * D]))
        term = gate * _dot(y_ref[...], w_ref[...])
        m = term if m is None else m + term
    u = _dot(m.astype(BF16), wo_ref[...])
    o_ref[...] = x + _rms(u, ln_ref[...])


def _merge(x2, y_ret, y_dsa, y_hg, lnpre, wg, wr, wd, wh, wo, ln):
    M = x2.shape[0]
    tm = min(PROJ_ROWS, M)
    D = D_MODEL
    row = lambda w: pl.BlockSpec((tm, w), lambda i: (i, 0))
    const = lambda r, c: pl.BlockSpec((r, c), lambda i: (0, 0))
    return pl.pallas_call(
        _merge_kernel,
        grid=(M // tm,),
        in_specs=[row(D), row(RET_W), row(DSA_W), row(HGRN_W), const(1, D), const(D, N_BRANCH * D),
                  const(RET_W, D), const(DSA_W, D), const(HGRN_W, D), const(D, D), const(1, D)],
        out_specs=row(D),
        out_shape=jax.ShapeDtypeStruct((M, D), F32),
        compiler_params=_params("parallel"),
        name="merge",
    )(x2, y_ret, y_dsa, y_hg, lnpre, wg, wr, wd, wh, wo, ln)


def _gelu_tanh(x):
    return 0.5 * x * (1.0 + jnp.tanh(math.sqrt(2.0 / math.pi) * (x + 0.044715 * (x * x * x))))


def _ffn_kernel(tiles_per_seq, x_ref, lnpre_ref, wup_ref, cw_ref, cb_ref, wdn_ref, lnpost_ref,
                o_ref, buf_ref, prev_ref):
    tm = x_ref.shape[0]
    HALO = SUBLANES
    fc = buf_ref.shape[1]
    n_pass = D_FF // fc
    first = (pl.program_id(0) % tiles_per_seq) == 0

    @pl.when(first)
    def _():
        prev_ref[...] = jnp.zeros_like(prev_ref)

    x = x_ref[...]
    h = _rms(x, lnpre_ref[...]).astype(BF16)

    def conv(part, c):
        col = part * D_FF + c * fc
        up = _dot(h, wup_ref[:, col:col + fc])
        slot = part * n_pass + c
        buf_ref[0:HALO, :] = prev_ref[slot]
        buf_ref[HALO:HALO + tm, :] = up
        prev_ref[slot] = up[tm - HALO:tm, :]
        w = cw_ref[:, col:col + fc]
        y = (up * w[2:3] + buf_ref[HALO - 1:HALO - 1 + tm, :] * w[1:2]
             + buf_ref[HALO - 2:HALO - 2 + tm, :] * w[0:1])
        return y + cb_ref[:, col:col + fc]

    acc = None
    for c in range(n_pass):
        a = conv(0, c)
        u = conv(1, c)
        act = (_gelu_tanh(a) * u).astype(BF16)
        d = _dot(act, wdn_ref[c * fc:(c + 1) * fc, :])
        acc = d if acc is None else acc + d
    o_ref[...] = x + _rms(acc, lnpost_ref[...])


def _ffn(x2, lnpre, wup, cw, cb, wdn, lnpost, T):
    M = x2.shape[0]
    D = D_MODEL
    tm = min(FFN_ROWS, T)
    fc = D_FF
    const = lambda r, c: pl.BlockSpec((r, c), lambda i: (0, 0))
    return pl.pallas_call(
        functools.partial(_ffn_kernel, T // tm),
        grid=(M // tm,),
        in_specs=[pl.BlockSpec((tm, D), lambda i: (i, 0)), const(1, D), const(D, 2 * D_FF),
                  const(CONV_WIDTH, 2 * D_FF), const(1, 2 * D_FF), const(D_FF, D), const(1, D)],
        out_specs=pl.BlockSpec((tm, D), lambda i: (i, 0)),
        out_shape=jax.ShapeDtypeStruct((M, D), F32),
        scratch_shapes=[pltpu.VMEM((tm + SUBLANES, fc), F32),
                        pltpu.VMEM((2 * (D_FF // fc), SUBLANES, fc), F32)],
        compiler_params=_params("arbitrary"),
        name="conv_ffn",
    )(x2, lnpre, wup, cw, cb, wdn, lnpost)


def _split_w_in(w):
    widths = (256, 256, 512, 512, 256, 128, 64, 8, 512, 512, 512, 512, N_BRANCH * D_MODEL)
    offs = np.concatenate([[0], np.cumsum(widths)])
    rq, rk, rv, rg, cq, ckv, ik, iw, hq, hf, hv, hg, gt = [w[..., offs[k]:offs[k + 1]] for k in range(len(widths))]
    zeros = lambda n: jnp.zeros(w.shape[:-1] + (n,), w.dtype)
    wb = jnp.concatenate([rq, rk, rv, cq, ckv, ik, zeros(2 * DSA_KV_RANK - DSA_KV_RANK - IDX_DIM), hq, hv], axis=-1)
    wf = jnp.concatenate([rg, hf, hg, iw, zeros(LANES - IDX_HEADS)], axis=-1)
    assert wb.shape[-1] == ZB_W and wf.shape[-1] == ZF_W
    return wb.astype(BF16), wf.astype(BF16), gt.astype(BF16)


def kernel(x, rel_bias, hgrn_lb, ln_mix_pre, ln_mix_post, ln_ffn_pre, ln_ffn_post, w_in, dsa_q_norm, dsa_kv_norm, dsa_w_uq, dsa_w_uk, dsa_w_uv, hgrn_norm, w_br_ret, w_br_dsa, w_br_hgrn, w_out, ffn_w_up, ffn_conv_w, ffn_conv_b, ffn_w_down):
    B, T, D = x.shape
    depth = w_in.shape[0]
    assert T % KEY_BLOCK == 0 and D == D_MODEL
    H, dh, Rq = DSA_HEADS, DSA_DH, DSA_Q_RANK
    bias_tiles = _rel_bias_tiles(rel_bias, min(Q_BLOCK, T), KEY_BLOCK)
    wq = dsa_w_uq[..., :H * dh].astype(BF16)
    wqi = dsa_w_uq[..., H * dh:].astype(BF16)
    wuk = dsa_w_uk.transpose(0, 1, 3, 2).astype(BF16)
    wuvt = dsa_w_uv.transpose(0, 1, 3, 2).astype(BF16)
    x2 = x.reshape(B * T, D)
    row = lambda v: v.reshape(1, -1)
    for l in range(depth):
        wb, wf, wg = _split_w_in(w_in[l])
        zb, zf = _inproj(x2, row(ln_mix_pre[l]), wb, wf)
        y_ret = _retention(zb, zf, B, T)
        y_hg = _hgrn(zb, zf, hgrn_lb, row(hgrn_norm[l]), l, B, T)
        qlt, qit, w_row, ckvn, ckvt, idxk = _dsa_prep(zb, zf, row(dsa_q_norm[l]), row(dsa_kv_norm[l]),
                                                      wq[l], wqi[l], wuk[l], B, T)
        y_dsa = _dsa_attn(qlt, qit, w_row, ckvn, ckvt, idxk, bias_tiles, wuvt[l], B, T)
        x2 = _merge(x2, y_ret, y_dsa, y_hg, row(ln_mix_pre[l]), wg, w_br_ret[l].astype(BF16),
                    w_br_dsa[l].astype(BF16), w_br_hgrn[l].astype(BF16), w_out[l].astype(BF16),
                    row(ln_mix_post[l]))
        x2 = _ffn(x2, row(ln_ffn_pre[l]), ffn_w_up[l].astype(BF16), ffn_conv_w[l], row(ffn_conv_b[l]),
                  ffn_w_down[l].astype(BF16), row(ln_ffn_post[l]), T)
    return x2.reshape(B, T, D)
```

```python
import functools
import math

import jax
import jax.numpy as jnp
import numpy as np
from jax import lax
from jax.experimental import pallas as pl
from jax.experimental.pallas import tpu as pltpu

F32 = jnp.float32
BF16 = jnp.bfloat16
I32 = jnp.int32

D_MODEL = 1024
RET_HEADS, RET_DK, RET_DV, RET_CHUNK = 4, 64, 128, 128
DSA_HEADS, DSA_DH, DSA_Q_RANK, DSA_KV_RANK = 8, 64, 256, 128
IDX_HEADS, IDX_DIM, DSA_TOPK_MAX = 8, 64, 256
Q_BLOCK = 256
HGRN_HEADS, HGRN_EXPAND, HGRN_DV = 4, 128, 128
F_FLOOR = 1e-6
REL_BUCKETS, REL_MAX_DIST = 32, 128
D_FF = 2816
CONV_WIDTH = 3
N_BRANCH = 3
EPS = 1e-6
NEG_BIG = -1e30

RET_W = RET_HEADS * RET_DV
DSA_W = DSA_HEADS * DSA_DH
HGRN_KW = HGRN_HEADS * HGRN_EXPAND
HGRN_W = HGRN_HEADS * HGRN_DV

ZB_RQ, ZB_RK, ZB_RV, ZB_CQ = 0, 256, 512, 1024
ZB_CKV = 1280
ZB_HQ, ZB_HV = 1536, 2048
ZB_W = 2560
ZF_RG, ZF_HF, ZF_HG, ZF_IW = 0, 512, 1024, 1536
ZF_W = 1664

VMEM_LIMIT_BYTES = 56 * 1024 * 1024
SUBLANES = 8
LANES = 128

KEY_BLOCK = 256
DSA_PREP_ROWS = 512
PROJ_ROWS = 1024
FFN_ROWS = 512
HGRN_ROWS = 512
HGRN_PAIR_CHUNK = 256
INT_MIN = -(2 ** 31)


def _params(*sem):
    return pltpu.CompilerParams(dimension_semantics=sem, vmem_limit_bytes=VMEM_LIMIT_BYTES)


def _dot(a, b):
    return jnp.dot(a, b, preferred_element_type=F32)


def _dot_nt(a, b):
    return lax.dot_general(a, b, (((1,), (1,)), ((), ())), preferred_element_type=F32)


def _dot_tn(a, b):
    return lax.dot_general(a, b, (((0,), (0,)), ((), ())), preferred_element_type=F32)


def _sigmoid(x):
    return 1.0 / (1.0 + jnp.exp(-x))


def _sigmoid_pair(x):
    t = jnp.exp(-jnp.abs(x))
    big = 1.0 / (1.0 + t)
    small = t * big
    pos = x >= 0.0
    return jnp.where(pos, big, small), jnp.where(pos, small, big)


def _rms(x, g):
    return x * lax.rsqrt(jnp.mean(x * x, axis=-1, keepdims=True) + EPS) * g


def _group_sum(x):
    return jnp.sum(x.reshape(x.shape[0] // SUBLANES, SUBLANES, x.shape[1]), axis=0)


def _group_max(x):
    return jnp.max(x.reshape(x.shape[0] // SUBLANES, SUBLANES, x.shape[1]), axis=0)


def _inproj_kernel(x_ref, g_ref, wb_ref, wf_ref, zb_ref, zf_ref):
    h = _rms(x_ref[...], g_ref[...]).astype(BF16)
    zb_ref[...] = _dot(h, wb_ref[...]).astype(BF16)
    zf_ref[...] = _dot(h, wf_ref[...])


def _inproj(x2, g, wb, wf):
    M = x2.shape[0]
    tm = min(PROJ_ROWS, M)
    const = lambda r, c: pl.BlockSpec((r, c), lambda i: (0, 0))
    return pl.pallas_call(
        _inproj_kernel,
        grid=(M // tm,),
        in_specs=[pl.BlockSpec((tm, D_MODEL), lambda i: (i, 0)), const(1, D_MODEL),
                  const(D_MODEL, ZB_W), const(D_MODEL, ZF_W)],
        out_specs=[pl.BlockSpec((tm, ZB_W), lambda i: (i, 0)), pl.BlockSpec((tm, ZF_W), lambda i: (i, 0))],
        out_shape=[jax.ShapeDtypeStruct((M, ZB_W), BF16), jax.ShapeDtypeStruct((M, ZF_W), F32)],
        compiler_params=_params("parallel"),
        name="inproj",
    )(x2, g, wb, wf)


def _ret_kernel(gam_ref, q_ref, k_ref, v_ref, g_ref, cos_ref, sin_ref, dm_ref, xi_ref, zeta_ref,
                o_ref, r_ref):
    C = q_ref.shape[0]
    H, dk, dv = RET_HEADS, RET_DK, RET_DV

    @pl.when(pl.program_id(1) == 0)
    def _():
        r_ref[...] = jnp.zeros_like(r_ref)

    cos = cos_ref[...]
    sin = sin_ref[...]
    lane = lax.broadcasted_iota(I32, (C, H * dk), 1)
    first_half = (lane % dk) < (dk // 2)

    def rot(x):
        swapped = jnp.where(first_half, pltpu.roll(x, H * dk - dk // 2, 1), pltpu.roll(x, dk // 2, 1))
        return x * cos + swapped * sin

    q = rot(q_ref[...].astype(F32))
    k = rot(k_ref[...].astype(F32)) * dk ** -0.5
    for h in range(H):
        qh = q[:, h * dk:(h + 1) * dk].astype(BF16)
        kh = k[:, h * dk:(h + 1) * dk]
        vh = v_ref[:, h * dv:(h + 1) * dv]
        s = _dot_nt(qh, kh.astype(BF16)) * dm_ref[h]
        inner = _dot(s.astype(BF16), vh)
        rh = r_ref[h]
        cross = _dot(qh, rh.astype(BF16)) * xi_ref[h]
        r_ref[h] = gam_ref[h] * rh + _dot_tn((kh * zeta_ref[h]).astype(BF16), vh)
        o = inner + cross
        mu = jnp.mean(o, axis=-1, keepdims=True)
        oc = o - mu
        var = jnp.mean(oc * oc, axis=-1, keepdims=True)
        gh = g_ref[:, h * dv:(h + 1) * dv]
        o_ref[:, h * dv:(h + 1) * dv] = (gh * _sigmoid(gh) * (oc * lax.rsqrt(var + EPS))).astype(BF16)


def _retention(zb, zf, B, T):
    H, dk, dv = RET_HEADS, RET_DK, RET_DV
    C = min(RET_CHUNK, T)
    n = T // C
    pos = np.arange(T, dtype=np.float64)
    half = dk // 2
    freq = 1.0 / (10000.0 ** np.linspace(0.0, 1.0, half))
    ang = pos[:, None] * freq[None, :]
    cos = jnp.asarray(np.tile(np.cos(ang), (1, 2 * H)), F32)
    sin = jnp.asarray(np.tile(np.concatenate([-np.sin(ang), np.sin(ang)], axis=1), (1, H)), F32)
    log_gamma = np.log1p(-(2.0 ** (-5.0 - np.arange(H, dtype=np.float64))))
    i = np.arange(C, dtype=np.float64)
    rel = i[:, None] - i[None, :]
    dmask = jnp.asarray(np.where(rel >= 0, np.exp(np.maximum(rel, 0.0)[None] * log_gamma[:, None, None]), 0.0), F32)
    xi = jnp.asarray(np.exp((i + 1.0)[None, :] * log_gamma[:, None])[:, :, None], F32)
    zeta = jnp.asarray(np.exp((C - 1.0 - i)[None, :] * log_gamma[:, None])[:, :, None], F32)
    gamma_c = jnp.asarray(np.exp(C * log_gamma), F32)

    wq = H * dk
    wv = H * dv
    full = lambda shape: pl.BlockSpec(shape, lambda b, c: (0,) * len(shape))
    return pl.pallas_call(
        _ret_kernel,
        grid=(B, n),
        in_specs=[pl.BlockSpec(memory_space=pltpu.SMEM),
                  pl.BlockSpec((C, wq), lambda b, c: (b * n + c, ZB_RQ // wq)),
                  pl.BlockSpec((C, wq), lambda b, c: (b * n + c, ZB_RK // wq)),
                  pl.BlockSpec((C, wv), lambda b, c: (b * n + c, ZB_RV // wv)),
                  pl.BlockSpec((C, wv), lambda b, c: (b * n + c, ZF_RG // wv)),
                  pl.BlockSpec((C, wq), lambda b, c: (c, 0)),
                  pl.BlockSpec((C, wq), lambda b, c: (c, 0)),
                  full((H, C, C)), full((H, C, 1)), full((H, C, 1))],
        out_specs=pl.BlockSpec((C, wv), lambda b, c: (b * n + c, 0)),
        out_shape=jax.ShapeDtypeStruct((B * T, wv), BF16),
        scratch_shapes=[pltpu.VMEM((H, dk, dv), F32)],
        compiler_params=_params("parallel", "arbitrary"),
        name="retention",
    )(gamma_c, zb, zb, zb, zf, cos, sin, dmask, xi, zeta)


def _hgrn_levels(C):
    ms, m = [], C // 2
    while m >= SUBLANES:
        ms.append(m)
        m //= 2
    return ms


def _cumsum_rows(tri, x):
    hi = x.astype(BF16)
    rest = x - hi.astype(F32)
    mid = rest.astype(BF16)
    lo = (rest - mid.astype(F32)).astype(BF16)
    return _dot(tri, hi) + (_dot(tri, mid) + _dot(tri, lo))


def _hgrn_kernel(layer, q_ref, f_ref, v_ref, g_ref, lbraw_ref, gain_ref, tri_ref, lmask_ref, o_ref, st_ref):
    TR = q_ref.shape[0]
    C = tri_ref.shape[0]
    H, dk, dv = HGRN_HEADS, HGRN_EXPAND, HGRN_DV
    SB = SUBLANES

    @pl.when(pl.program_id(1) == 0)
    def _():
        st_ref[...] = jnp.zeros_like(st_ref)

    raw = lbraw_ref[...]
    e = jnp.exp(raw - jnp.max(raw, axis=0, keepdims=True))
    soft = e / jnp.sum(e, axis=0, keepdims=True)
    cs = soft[0:1]
    for l in range(1, layer + 1):
        cs = cs + soft[l:l + 1]
    lb = jnp.clip(cs - soft[0:1], 0.0, 1.0)
    tri = tri_ref[...]
    row_in_blk = lax.broadcasted_iota(I32, (SB, dk), 0)
    lane_t = lax.broadcasted_iota(I32, (SB, C), 1)
    levels = _hgrn_levels(C)

    def chunk(c, carry):
        r0 = pl.multiple_of(c * C, C)
        sig_pos, sig_neg = _sigmoid_pair(f_ref[pl.ds(r0, C), :])
        f = lb + (1.0 - lb) * sig_pos
        log_f = jnp.log(jnp.maximum(f, F_FLOOR))
        kk = (1.0 - lb) * sig_neg
        b = _cumsum_rows(tri, log_f)
        qq = q_ref[pl.ds(r0, C), :].astype(F32)
        vb = v_ref[pl.ds(r0, C), :]
        eb = jnp.exp(b)
        b_last = b[C - 1:C, :]
        eb_last = eb[C - 1:C, :]
        q_dec = (qq * eb).astype(BF16)
        k_dec = (kk * jnp.exp(b_last - b)).astype(BF16)
        outs = []
        for h in range(H):
            sl = slice(h * dk, (h + 1) * dk)
            bh, qh, kh, vh = b[:, sl], qq[:, sl], kk[:, sl], vb[:, h * dv:(h + 1) * dv]
            diag = []
            for blk in range(C // SB):
                bs = bh[blk * SB:(blk + 1) * SB, :]
                ks = kh[blk * SB:(blk + 1) * SB, :]
                at = jnp.zeros((SB, C), F32)
                for tt in range(SB):
                    t = blk * SB + tt
                    diff = jnp.where(row_in_blk <= tt, bh[t:t + 1, :] - bs, NEG_BIG)
                    p = (qh[t:t + 1, :] * jnp.exp(diff)) * ks
                    at = jnp.where(lane_t == t, jnp.sum(p, axis=-1, keepdims=True), at)
                diag.append(at)
            a_t = jnp.concatenate(diag, axis=0)
            for lev, m in enumerate(levels):
                qs, ks = [], []
                for blk in range(C // m):
                    rows = slice(blk * m, (blk + 1) * m)
                    if blk % 2 == 1:
                        ref = bh[blk * m - 1:blk * m, :]
                        qs.append(qh[rows] * jnp.exp(bh[rows] - ref))
                        ks.append(jnp.zeros((m, dk), F32))
                    else:
                        ref = bh[(blk + 1) * m - 1:(blk + 1) * m, :]
                        ks.append(kh[rows] * jnp.exp(ref - bh[rows]))
                        qs.append(jnp.zeros((m, dk), F32))
                q_l = jnp.concatenate(qs, axis=0).astype(BF16)
                k_l = jnp.concatenate(ks, axis=0).astype(BF16)
                a_t = a_t + _dot_nt(k_l, q_l) * lmask_ref[lev]
            intra = _dot_tn(a_t.astype(BF16), vh)
            st = st_ref[h]
            outs.append(intra + _dot_nt(q_dec[:, sl], st.astype(BF16)))
            st_ref[h] = st * eb_last[:, sl] + _dot_tn(vh, k_dec[:, sl])
        o = _rms(jnp.concatenate(outs, axis=1), gain_ref[...])
        gg = g_ref[pl.ds(r0, C), :]
        o_ref[pl.ds(r0, C), :] = (gg * _sigmoid(gg) * o).astype(BF16)
        return carry

    lax.fori_loop(0, TR // C, chunk, 0)


def _hgrn(zb, zf, hgrn_lb, gain, layer, B, T):
    H, dk, dv = HGRN_HEADS, HGRN_EXPAND, HGRN_DV
    C = min(HGRN_PAIR_CHUNK, T)
    TR = min(HGRN_ROWS, T)
    n = T // TR
    w = H * dk
    L = hgrn_lb.shape[0]
    tri = jnp.tril(jnp.ones((C, C), BF16))
    idx = np.arange(C)
    lmask = np.stack([(((idx[None, :] // m) % 2 == 1) & (idx[:, None] // m == idx[None, :] // m - 1))
                      for m in _hgrn_levels(C)]).astype(np.float32)
    return pl.pallas_call(
        functools.partial(_hgrn_kernel, layer),
        grid=(B, n),
        in_specs=[pl.BlockSpec((TR, w), lambda b, c: (b * n + c, ZB_HQ // w)),
                  pl.BlockSpec((TR, w), lambda b, c: (b * n + c, ZF_HF // w)),
                  pl.BlockSpec((TR, w), lambda b, c: (b * n + c, ZB_HV // w)),
                  pl.BlockSpec((TR, w), lambda b, c: (b * n + c, ZF_HG // w)),
                  pl.BlockSpec((L, w), lambda b, c: (0, 0)),
                  pl.BlockSpec((1, w), lambda b, c: (0, 0)),
                  pl.BlockSpec((C, C), lambda b, c: (0, 0)),
                  pl.BlockSpec(lmask.shape, lambda b, c: (0, 0, 0))],
        out_specs=pl.BlockSpec((TR, w), lambda b, c: (b * n + c, 0)),
        out_shape=jax.ShapeDtypeStruct((B * T, w), BF16),
        scratch_shapes=[pltpu.VMEM((H, dv, dk), F32)],
        compiler_params=_params("parallel", "arbitrary"),
        name="hgrn2",
    )(zb, zf, zb, zf, hgrn_lb, gain, tri, jnp.asarray(lmask))


def _dsa_prep_kernel(cq_ref, kv_ref, iw_ref, qn_ref, kn_ref, wq_ref, wqi_ref, wukt_ref,
                     qlt_ref, qit_ref, w_ref, ckv_ref, ckvt_ref, ik_ref):
    tm = cq_ref.shape[0]
    H, dh, HI, dI, Dc = DSA_HEADS, DSA_DH, IDX_HEADS, IDX_DIM, DSA_KV_RANK
    QB = qlt_ref.shape[2] // H
    nb = tm // QB
    cq = _rms(cq_ref[...].astype(F32), qn_ref[...]).astype(BF16)
    ckv = _rms(kv_ref[:, :Dc].astype(F32), kn_ref[...])
    ckv_ref[...] = ckv.astype(BF16)
    KB = ckvt_ref.shape[3]
    for j in range(tm // KB):
        ckvt_ref[0, j, 0:Dc, :] = ckv[j * KB:(j + 1) * KB].T.astype(BF16)
        ckvt_ref[0, j, Dc:Dc + SUBLANES, :] = jnp.ones((SUBLANES, KB), BF16)
    ik_ref[...] = kv_ref[:, Dc:Dc + dI]
    w_t = iw_ref[...].T[0:HI, :] * (HI * dI) ** -0.5
    qi_all = _dot(cq, wqi_ref[...])
    q_all = _dot(cq, wq_ref[...])
    for j in range(nb):
        rows = slice(j * QB, (j + 1) * QB)
        qi_t = qi_all[rows].T
        q_t = q_all[rows].T.astype(BF16)
        for h in range(HI):
            qit_ref[j, :, h * QB:(h + 1) * QB] = qi_t[h * dI:(h + 1) * dI].astype(BF16)
            w_ref[j, :, h * QB:(h + 1) * QB] = w_t[h:h + 1, rows]
        for h in range(H):
            ql_t = _dot(wukt_ref[h], q_t[h * dh:(h + 1) * dh]) * dh ** -0.5
            qlt_ref[j, :, h * QB:(h + 1) * QB] = ql_t.astype(BF16)


def _dsa_prep(zb, zf, qn, kn, wq, wqi, wuk, B, T):
    M = B * T
    H, dh, HI, dI, Dc, Rq = DSA_HEADS, DSA_DH, IDX_HEADS, IDX_DIM, DSA_KV_RANK, DSA_Q_RANK
    QB = min(Q_BLOCK, T)
    KB = KEY_BLOCK
    tm = min(DSA_PREP_ROWS, T)
    nb = tm // QB
    nt = T // tm
    kpt = tm // KB
    return pl.pallas_call(
        _dsa_prep_kernel,
        grid=(M // tm,),
        in_specs=[pl.BlockSpec((tm, Rq), lambda i: (i, ZB_CQ // Rq)),
                  pl.BlockSpec((tm, 2 * Dc), lambda i: (i, ZB_CKV // (2 * Dc))),
                  pl.BlockSpec((tm, LANES), lambda i: (i, ZF_IW // LANES)),
                  pl.BlockSpec((1, Rq), lambda i: (0, 0)),
                  pl.BlockSpec((1, Dc), lambda i: (0, 0)),
                  pl.BlockSpec((Rq, H * dh), lambda i: (0, 0)),
                  pl.BlockSpec((Rq, HI * dI), lambda i: (0, 0)),
                  pl.BlockSpec((H, Dc, dh), lambda i: (0, 0, 0))],
        out_specs=[pl.BlockSpec((nb, Dc, H * QB), lambda i: (i, 0, 0)),
                   pl.BlockSpec((nb, dI, HI * QB), lambda i: (i, 0, 0)),
                   pl.BlockSpec((nb, 1, HI * QB), lambda i: (i, 0, 0)),
                   pl.BlockSpec((tm, Dc), lambda i: (i, 0)),
                   pl.BlockSpec((1, kpt, Dc + SUBLANES, KB), lambda i: (i // nt, i % nt, 0, 0)),
                   pl.BlockSpec((tm, dI), lambda i: (i, 0))],
        out_shape=[jax.ShapeDtypeStruct((M // QB, Dc, H * QB), BF16),
                   jax.ShapeDtypeStruct((M // QB, dI, HI * QB), BF16),
                   jax.ShapeDtypeStruct((M // QB, 1, HI * QB), F32),
                   jax.ShapeDtypeStruct((M, Dc), BF16),
                   jax.ShapeDtypeStruct((B, nt * kpt, Dc + SUBLANES, KB), BF16),
                   jax.ShapeDtypeStruct((M, dI), BF16)],
        compiler_params=_params("parallel"),
        name="dsa_prep",
    )(zb, zb, zf, qn, kn, wq, wqi, wuk)


def _sortable_key(s):
    s = jnp.where(s == 0.0, 0.0, s)
    bits = pltpu.bitcast(s, I32)
    return bits ^ ((bits >> 31) & 0x7FFFFFFF)


def _two_per_trip(lo, hi, body, init):
    def pair(j, carry):
        return body(lo + 2 * j + 1, body(lo + 2 * j, carry))

    carry = lax.fori_loop(0, (hi - lo) // 2, pair, init)
    return lax.cond((hi - lo) % 2 == 1, lambda c: body(hi - 1, c), lambda c: c, carry)


def _dsa_attn_kernel(topk, n_keys, neg_key, qlt_ref, qit_ref, w_ref, ckv_ref, ckvt_ref, ik_ref, bias_ref,
                     wuvt_ref, o_ref, key_ref, z_ref, acc_ref, tie_ref):
    H, HI = DSA_HEADS, IDX_HEADS
    QB = o_ref.shape[0]
    KB = key_ref.shape[1]
    q0 = pl.program_id(1) * QB
    nkb = (q0 + QB - 1) // KB + 1
    n_skip = n_keys - nkb * KB

    s_row = lax.broadcasted_iota(I32, (KB, QB), 0)
    t_row = q0 + lax.broadcasted_iota(I32, (1, QB), 1)

    qit = qit_ref[0]
    w_row = w_ref[0]

    def score_blk(causal_edge, kb, carry):
        k0 = pl.multiple_of(kb * KB, KB)
        p = _dot(ik_ref[pl.ds(k0, KB), :], qit)
        p = jnp.maximum(p, 0.0) * w_row
        s = p[:, 0:QB]
        for h in range(1, HI):
            s = s + p[:, h * QB:(h + 1) * QB]
        if causal_edge:
            s = jnp.where(k0 + s_row <= t_row, s, NEG_BIG)
        key_ref[kb] = _sortable_key(s)
        return carry

    n_past = q0 // KB
    _two_per_trip(0, n_past, functools.partial(score_blk, False), 0)
    lax.fori_loop(n_past, nkb, functools.partial(score_blk, True), 0)

    def count(hit_fn):
        def body(kb, acc):
            return acc + _group_sum(jnp.where(hit_fn(kb, key_ref[kb]), 1, 0))

        acc = _two_per_trip(0, nkb, body, jnp.zeros((SUBLANES, QB), I32))
        return jnp.sum(acc, axis=0, keepdims=True)

    def count_ge(cand):
        return count(lambda kb, kk: kk >= cand) + jnp.where(cand <= neg_key, n_skip, 0)

    zero = jnp.zeros((1, QB), I32)
    cnt0 = count_ge(zero)
    state = (jnp.where(cnt0 >= topk, zero, jnp.full((1, QB), INT_MIN, I32)),
             jnp.where(cnt0 >= topk, cnt0, n_keys))

    def bisect(it, state):
        thr, cnt = state
        cand = thr | jnp.left_shift(jnp.int32(1), 30 - it)
        c = count_ge(cand)
        return jnp.where(c >= topk, cand, thr), jnp.where(c >= topk, c, cnt)

    thr, cnt = lax.fori_loop(0, 31, bisect, state)

    nbits = max(1, (n_keys - 1).bit_length())
    tie_ref[...] = jnp.full(tie_ref.shape, (1 << nbits) - 1, I32)

    @pl.when(jnp.max(jnp.where(cnt > topk, 1, 0)) > 0)
    def _():
        n_gt = count(lambda kb, kk: kk > thr) + jnp.where(thr < neg_key, n_skip, 0)
        need = topk - n_gt

        def ibisect(it, p):
            cand = p | jnp.left_shift(jnp.int32(1), nbits - 1 - it)
            below = count(lambda kb, kk: (kk == thr) & (kb * KB + s_row < cand))
            return jnp.where(below < need, cand, p)

        p = lax.fori_loop(0, nbits, ibisect, jnp.zeros((1, QB), I32))
        tie_ref[...] = jnp.broadcast_to(p, tie_ref.shape)

    tie = tie_ref[0:1, :]

    qlt = qlt_ref[0]

    FAR_TILE = bias_ref.shape[0] - 1
    n_far = jnp.maximum((q0 - FAR_TILE * QB) // KB + 1, 0)
    far_bias = bias_ref[FAR_TILE, 0:1, :]

    def logits_blk(near, kb, m8):
        k0 = pl.multiple_of(kb * KB, KB)
        kk = key_ref[kb]
        s_glob = k0 + s_row
        sel = (kk > thr) | ((kk == thr) & (s_glob <= tie))
        z = _dot(ckv_ref[pl.ds(k0, KB), :], qlt)
        if near:
            sel = sel & (s_glob <= t_row)
            z = z + bias_ref[(q0 - k0) // QB]
        tops = []
        for h in range(H):
            zh = jnp.where(sel, z[:, h * QB:(h + 1) * QB], NEG_BIG)
            z_ref[kb, :, h * QB:(h + 1) * QB] = zh
            tops.append(_group_max(zh))
        return jnp.maximum(m8, jnp.concatenate(tops, axis=1))

    neg8 = jnp.full((SUBLANES, H * QB), NEG_BIG, F32)
    m8_far = _two_per_trip(0, n_far, functools.partial(logits_blk, False), neg8)
    m8_near = _two_per_trip(n_far, nkb, functools.partial(logits_blk, True), neg8)
    m = jnp.maximum(jnp.max(m8_far, axis=0, keepdims=True) + far_bias,
                    jnp.max(m8_near, axis=0, keepdims=True))
    m_far = m - far_bias
    acc_ref[...] = jnp.zeros(acc_ref.shape, F32)

    def pv_blk(kb):
        p = jnp.exp((z_ref[kb] - jnp.where(kb < n_far, m_far, m)).astype(BF16))
        return _dot(ckvt_ref[0, kb], p)

    def pv_pair(j, carry):
        acc_ref[...] += pv_blk(2 * j) + pv_blk(2 * j + 1)
        return carry

    lax.fori_loop(0, nkb // 2, pv_pair, 0)

    @pl.when(nkb % 2 == 1)
    def _():
        acc_ref[...] += pv_blk(nkb - 1)

    Dc = acc_ref.shape[0] - SUBLANES
    inv_l = 1.0 / acc_ref[Dc:Dc + 1, :]
    o_lat_t = (acc_ref[0:Dc, :] * inv_l).astype(BF16)
    y_t = jnp.concatenate([_dot(wuvt_ref[h], o_lat_t[:, h * QB:(h + 1) * QB]) for h in range(H)], axis=0)
    o_ref[...] = y_t.T.astype(BF16)


def _dsa_attn(qlt, qit, w_row, ckvn, ckvt, idxk, bias_tiles, wuvt, B, T):
    H, HI, dI, Dc, dh = DSA_HEADS, IDX_HEADS, IDX_DIM, DSA_KV_RANK, DSA_DH
    QB = min(Q_BLOCK, T)
    KB = KEY_BLOCK
    nq = T // QB
    nk = T // KB
    topk = min(DSA_TOPK_MAX, T // 4)
    neg_key = int(np.array(NEG_BIG, np.float32).view(np.int32))
    neg_key = neg_key ^ ((neg_key >> 31) & 0x7FFFFFFF)
    return pl.pallas_call(
        functools.partial(_dsa_attn_kernel, topk, T, neg_key),
        grid=(B, nq),
        in_specs=[pl.BlockSpec((1, Dc, H * QB), lambda b, i: (b * nq + i, 0, 0)),
                  pl.BlockSpec((1, dI, HI * QB), lambda b, i: (b * nq + i, 0, 0)),
                  pl.BlockSpec((1, 1, HI * QB), lambda b, i: (b * nq + i, 0, 0)),
                  pl.BlockSpec((T, Dc), lambda b, i: (b, 0)),
                  pl.BlockSpec((1, nk, Dc + SUBLANES, KB), lambda b, i: (b, 0, 0, 0)),
                  pl.BlockSpec((T, dI), lambda b, i: (b, 0)),
                  pl.BlockSpec((4, KB, H * QB), lambda b, i: (0, 0, 0), pipeline_mode=pl.Buffered(1)),
                  pl.BlockSpec((H, dh, Dc), lambda b, i: (0, 0, 0))],
        out_specs=pl.BlockSpec((QB, DSA_W), lambda b, i: (b * nq + i, 0)),
        out_shape=jax.ShapeDtypeStruct((B * T, DSA_W), BF16),
        scratch_shapes=[pltpu.VMEM((nk, KB, QB), I32),
                        pltpu.VMEM((nk, KB, H * QB), F32),
                        pltpu.VMEM((Dc + SUBLANES, H * QB), F32),
                        pltpu.VMEM((SUBLANES, QB), I32)],
        compiler_params=_params("parallel", "arbitrary"),
        name="dsa_attn",
    )(qlt, qit, w_row, ckvn, ckvt, idxk, bias_tiles, wuvt)


def _rel_bias_tiles(rel_bias, QB, KB):
    max_exact = REL_BUCKETS // 2
    n_far = 3 * QB
    assert n_far - (KB - 1) > REL_MAX_DIST
    n = jnp.arange(n_far, dtype=jnp.int32)
    nf = jnp.maximum(n, max_exact).astype(F32)
    large = max_exact + (jnp.log(nf / max_exact) / math.log(REL_MAX_DIST / max_exact)
                         * (REL_BUCKETS - max_exact)).astype(jnp.int32)
    large = jnp.minimum(large, REL_BUCKETS - 1)
    bucket = jnp.where(n < max_exact, n, large)
    H = rel_bias.shape[1]
    tab = jnp.take(rel_bias, bucket, axis=0).T
    P = KB + QB
    tiles = []
    for delta in (0, QB, 2 * QB):
        d = np.concatenate([np.arange(QB + 1), np.arange(-(KB - 1), 0)])
        g = tab[:, np.clip(delta + d, 0, n_far - 1)]
        rows = jnp.tile(g, (1, KB))[:, :KB * (P - 1)].reshape(H, KB, P - 1)[:, :, :QB]
        tiles.append(rows.transpose(1, 0, 2).reshape(KB, H * QB))
    tiles.append(jnp.broadcast_to(jnp.repeat(rel_bias[REL_BUCKETS - 1], QB)[None, :], (KB, H * QB)))
    return jnp.stack(tiles).astype(F32)


def _merge_kernel(x_ref, yr_ref, yd_ref, yh_ref, lnpre_ref, wg_ref, wr_ref, wd_ref, wh_ref, wo_ref,
                  ln_ref, o_ref):
    D = x_ref.shape[1]
    x = x_ref[...]
    h = _rms(x, lnpre_ref[...]).astype(BF16)
    m = None
    for k, (y_ref, w_ref) in enumerate(((yr_ref, wr_ref), (yd_ref, wd_ref), (yh_ref, wh_ref))):
        gate = _sigmoid(_dot(h, wg_ref[:, k * D:(k + 1) * D]))
        term = gate * _dot(y_ref[...], w_ref[...])
        m = term if m is None else m + term
    u = _dot(m.astype(BF16), wo_ref[...])
    o_ref[...] = x + _rms(u, ln_ref[...])


def _merge(x2, y_ret, y_dsa, y_hg, lnpre, wg, wr, wd, wh, wo, ln):
    M = x2.shape[0]
    tm = min(PROJ_ROWS, M)
    D = D_MODEL
    row = lambda w: pl.BlockSpec((tm, w), lambda i: (i, 0))
    const = lambda r, c: pl.BlockSpec((r, c), lambda i: (0, 0))
    return pl.pallas_call(
        _merge_kernel,
        grid=(M // tm,),
        in_specs=[row(D), row(RET_W), row(DSA_W), row(HGRN_W), const(1, D), const(D, N_BRANCH * D),
                  const(RET_W, D), const(DSA_W, D), const(HGRN_W, D), const(D, D), const(1, D)],
        out_specs=row(D),
        out_shape=jax.ShapeDtypeStruct((M, D), F32),
        compiler_params=_params("parallel"),
        name="merge",
    )(x2, y_ret, y_dsa, y_hg, lnpre, wg, wr, wd, wh, wo, ln)


def _gelu_tanh(x):
    return 0.5 * x * (1.0 + jnp.tanh(math.sqrt(2.0 / math.pi) * (x + 0.044715 * (x * x * x))))


def _ffn_kernel(tiles_per_seq, x_ref, lnpre_ref, wup_ref, cw_ref, cb_ref, wdn_ref, lnpost_ref,
                o_ref, buf_ref, prev_ref):
    tm = x_ref.shape[0]
    HALO = SUBLANES
    fc = buf_ref.shape[1]
    n_pass = D_FF // fc
    first = (pl.program_id(0) % tiles_per_seq) == 0

    @pl.when(first)
    def _():
        prev_ref[...] = jnp.zeros_like(prev_ref)

    x = x_ref[...]
    h = _rms(x, lnpre_ref[...]).astype(BF16)

    def conv(part, c):
        col = part * D_FF + c * fc
        up = _dot(h, wup_ref[:, col:col + fc])
        slot = part * n_pass + c
        buf_ref[0:HALO, :] = prev_ref[slot]
        buf_ref[HALO:HALO + tm, :] = up
        prev_ref[slot] = up[tm - HALO:tm, :]
        w = cw_ref[:, col:col + fc]
        y = (up * w[2:3] + buf_ref[HALO - 1:HALO - 1 + tm, :] * w[1:2]
             + buf_ref[HALO - 2:HALO - 2 + tm, :] * w[0:1])
        return y + cb_ref[:, col:col + fc]

    acc = None
    for c in range(n_pass):
        a = conv(0, c)
        u = conv(1, c)
        act = (_gelu_tanh(a) * u).astype(BF16)
        d = _dot(act, wdn_ref[c * fc:(c + 1) * fc, :])
        acc = d if acc is None else acc + d
    o_ref[...] = x + _rms(acc, lnpost_ref[...])


def _ffn(x2, lnpre, wup, cw, cb, wdn, lnpost, T):
    M = x2.shape[0]
    D = D_MODEL
    tm = min(FFN_ROWS, T)
    fc = D_FF
    const = lambda r, c: pl.BlockSpec((r, c), lambda i: (0, 0))
    return pl.pallas_call(
        functools.partial(_ffn_kernel, T // tm),
        grid=(M // tm,),
        in_specs=[pl.BlockSpec((tm, D), lambda i: (i, 0)), const(1, D), const(D, 2 * D_FF),
                  const(CONV_WIDTH, 2 * D_FF), const(1, 2 * D_FF), const(D_FF, D), const(1, D)],
        out_specs=pl.BlockSpec((tm, D), lambda i: (i, 0)),
        out_shape=jax.ShapeDtypeStruct((M, D), F32),
        scratch_shapes=[pltpu.VMEM((tm + SUBLANES, fc), F32),
                        pltpu.VMEM((2 * (D_FF // fc), SUBLANES, fc), F32)],
        compiler_params=_params("arbitrary"),
        name="conv_ffn",
    )(x2, lnpre, wup, cw, cb, wdn, lnpost)


def _split_w_in(w):
    widths = (256, 256, 512, 512, 256, 128, 64, 8, 512, 512, 512, 512, N_BRANCH * D_MODEL)
    offs = np.concatenate([[0], np.cumsum(widths)])
    rq, rk, rv, rg, cq, ckv, ik, iw, hq, hf, hv, hg, gt = [w[..., offs[k]:offs[k + 1]] for k in range(len(widths))]
    zeros = lambda n: jnp.zeros(w.shape[:-1] + (n,), w.dtype)
    wb = jnp.concatenate([rq, rk, rv, cq, ckv, ik, zeros(2 * DSA_KV_RANK - DSA_KV_RANK - IDX_DIM), hq, hv], axis=-1)
    wf = jnp.concatenate([rg, hf, hg, iw, zeros(LANES - IDX_HEADS)], axis=-1)
    assert wb.shape[-1] == ZB_W and wf.shape[-1] == ZF_W
    return wb.astype(BF16), wf.astype(BF16), gt.astype(BF16)


def kernel(x, rel_bias, hgrn_lb, ln_mix_pre, ln_mix_post, ln_ffn_pre, ln_ffn_post, w_in, dsa_q_norm, dsa_kv_norm, dsa_w_uq, dsa_w_uk, dsa_w_uv, hgrn_norm, w_br_ret, w_br_dsa, w_br_hgrn, w_out, ffn_w_up, ffn_conv_w, ffn_conv_b, ffn_w_down):
    B, T, D = x.shape
    depth = w_in.shape[0]
    assert T % KEY_BLOCK == 0 and D == D_MODEL
    H, dh, Rq = DSA_HEADS, DSA_DH, DSA_Q_RANK
    bias_tiles = _rel_bias_tiles(rel_bias, min(Q_BLOCK, T), KEY_BLOCK)
    wq = dsa_w_uq[..., :H * dh].astype(BF16)
    wqi = dsa_w_uq[..., H * dh:].astype(BF16)
    wuk = dsa_w_uk.transpose(0, 1, 3, 2).astype(BF16)
    wuvt = dsa_w_uv.transpose(0, 1, 3, 2).astype(BF16)
    x2 = x.reshape(B * T, D)
    row = lambda v: v.reshape(1, -1)
    for l in range(depth):
        wb, wf, wg = _split_w_in(w_in[l])
        zb, zf = _inproj(x2, row(ln_mix_pre[l]), wb, wf)
        y_ret = _retention(zb, zf, B, T)
        y_hg = _hgrn(zb, zf, hgrn_lb, row(hgrn_norm[l]), l, B, T)
        qlt, qit, w_row, ckvn, ckvt, idxk = _dsa_prep(zb, zf, row(dsa_q_norm[l]), row(dsa_kv_norm[l]),
                                                      wq[l], wqi[l], wuk[l], B, T)
        y_dsa = _dsa_attn(qlt, qit, w_row, ckvn, ckvt, idxk, bias_tiles, wuvt[l], B, T)
        x2 = _merge(x2, y_ret, y_dsa, y_hg, row(ln_mix_pre[l]), wg, w_br_ret[l].astype(BF16),
                    w_br_dsa[l].astype(BF16), w_br_hgrn[l].astype(BF16), w_out[l].astype(BF16),
                    row(ln_mix_post[l]))
        x2 = _ffn(x2, row(ln_ffn_pre[l]), ffn_w_up[l].astype(BF16), ffn_conv_w[l], row(ffn_conv_b[l]),
                  ffn_w_down[l].astype(BF16), row(ln_ffn_post[l]), T)
    return x2.reshape(B, T, D)
```

```python
import functools
import math

import jax
import jax.numpy as jnp
import numpy as np
from jax import lax
from jax.experimental import pallas as pl
from jax.experimental.pallas import tpu as pltpu

F32 = jnp.float32
BF16 = jnp.bfloat16
I32 = jnp.int32

D_MODEL = 1024
RET_HEADS, RET_DK, RET_DV, RET_CHUNK = 4, 64, 128, 128
DSA_HEADS, DSA_DH, DSA_Q_RANK, DSA_KV_RANK = 8, 64, 256, 128
IDX_HEADS, IDX_DIM, DSA_TOPK_MAX = 8, 64, 256
Q_BLOCK = 256
HGRN_HEADS, HGRN_EXPAND, HGRN_DV = 4, 128, 128
F_FLOOR = 1e-6
REL_BUCKETS, REL_MAX_DIST = 32, 128
D_FF = 2816
CONV_WIDTH = 3
N_BRANCH = 3
EPS = 1e-6
NEG_BIG = -1e30

RET_W = RET_HEADS * RET_DV
DSA_W = DSA_HEADS * DSA_DH
HGRN_KW = HGRN_HEADS * HGRN_EXPAND
HGRN_W = HGRN_HEADS * HGRN_DV

ZB_RQ, ZB_RK, ZB_RV, ZB_CQ = 0, 256, 512, 1024
ZB_CKV = 1280
ZB_HQ, ZB_HV = 1536, 2048
ZB_W = 2560
ZF_RG, ZF_HF, ZF_HG, ZF_IW = 0, 512, 1024, 1536
ZF_W = 1664

VMEM_LIMIT_BYTES = 56 * 1024 * 1024
SUBLANES = 8
LANES = 128

KEY_BLOCK = 256
DSA_PREP_ROWS = 512
PROJ_ROWS = 1024
FFN_ROWS = 512
HGRN_ROWS = 512
RET_ROWS = 256
HGRN_PAIR_CHUNK = 256
INT_MIN = -(2 ** 31)


def _params(*sem):
    return pltpu.CompilerParams(dimension_semantics=sem, vmem_limit_bytes=VMEM_LIMIT_BYTES)


def _dot(a, b):
    return jnp.dot(a, b, preferred_element_type=F32)


def _dot_nt(a, b):
    return lax.dot_general(a, b, (((1,), (1,)), ((), ())), preferred_element_type=F32)


def _dot_tn(a, b):
    return lax.dot_general(a, b, (((0,), (0,)), ((), ())), preferred_element_type=F32)


def _sigmoid(x):
    return 1.0 / (1.0 + jnp.exp(-x))


def _sigmoid_pair(x):
    t = jnp.exp(-jnp.abs(x))
    big = 1.0 / (1.0 + t)
    small = t * big
    pos = x >= 0.0
    return jnp.where(pos, big, small), jnp.where(pos, small, big)


def _rms(x, g):
    return x * lax.rsqrt(jnp.mean(x * x, axis=-1, keepdims=True) + EPS) * g


def _group_sum(x):
    return jnp.sum(x.reshape(x.shape[0] // SUBLANES, SUBLANES, x.shape[1]), axis=0)


def _group_max(x):
    return jnp.max(x.reshape(x.shape[0] // SUBLANES, SUBLANES, x.shape[1]), axis=0)


def _inproj_kernel(x_ref, g_ref, wb_ref, wf_ref, zb_ref, zf_ref):
    h = _rms(x_ref[...], g_ref[...]).astype(BF16)
    zb_ref[...] = _dot(h, wb_ref[...]).astype(BF16)
    zf_ref[...] = _dot(h, wf_ref[...])


def _inproj(x2, g, wb, wf):
    M = x2.shape[0]
    tm = min(PROJ_ROWS, M)
    const = lambda r, c: pl.BlockSpec((r, c), lambda i: (0, 0))
    return pl.pallas_call(
        _inproj_kernel,
        grid=(M // tm,),
        in_specs=[pl.BlockSpec((tm, D_MODEL), lambda i: (i, 0)), const(1, D_MODEL),
                  const(D_MODEL, ZB_W), const(D_MODEL, ZF_W)],
        out_specs=[pl.BlockSpec((tm, ZB_W), lambda i: (i, 0)), pl.BlockSpec((tm, ZF_W), lambda i: (i, 0))],
        out_shape=[jax.ShapeDtypeStruct((M, ZB_W), BF16), jax.ShapeDtypeStruct((M, ZF_W), F32)],
        compiler_params=_params("parallel"),
        name="inproj",
    )(x2, g, wb, wf)


def _ret_kernel(gam_ref, q_ref, k_ref, v_ref, g_ref, cos_ref, sin_ref, dm_ref, xi_ref, zeta_ref,
                o_ref, r_ref):
    rows = q_ref.shape[0]
    C = dm_ref.shape[1]
    H, dk, dv = RET_HEADS, RET_DK, RET_DV

    @pl.when(pl.program_id(1) == 0)
    def _():
        r_ref[...] = jnp.zeros_like(r_ref)

    cos = cos_ref[...]
    sin = sin_ref[...]
    lane = lax.broadcasted_iota(I32, (rows, H * dk), 1)
    first_half = (lane % dk) < (dk // 2)

    def rot(x):
        swapped = jnp.where(first_half, pltpu.roll(x, H * dk - dk // 2, 1), pltpu.roll(x, dk // 2, 1))
        return x * cos + swapped * sin

    q = rot(q_ref[...].astype(F32))
    k = rot(k_ref[...].astype(F32)) * dk ** -0.5
    for c in range(rows // C):
        rs = slice(c * C, (c + 1) * C)
        for h in range(H):
            qh = q[rs, h * dk:(h + 1) * dk].astype(BF16)
            kh = k[rs, h * dk:(h + 1) * dk]
            vh = v_ref[rs, h * dv:(h + 1) * dv]
            s = _dot_nt(qh, kh.astype(BF16)) * dm_ref[h]
            inner = _dot(s.astype(BF16), vh)
            rh = r_ref[h]
            cross = _dot(qh, rh.astype(BF16)) * xi_ref[h]
            r_ref[h] = gam_ref[h] * rh + _dot_tn((kh * zeta_ref[h]).astype(BF16), vh)
            o = inner + cross
            mu = jnp.mean(o, axis=-1, keepdims=True)
            oc = o - mu
            var = jnp.mean(oc * oc, axis=-1, keepdims=True)
            gh = g_ref[rs, h * dv:(h + 1) * dv]
            o_ref[rs, h * dv:(h + 1) * dv] = (gh * _sigmoid(gh) * (oc * lax.rsqrt(var + EPS))).astype(BF16)


def _retention(zb, zf, B, T):
    H, dk, dv = RET_HEADS, RET_DK, RET_DV
    C = min(RET_CHUNK, T)
    n = T // C
    pos = np.arange(T, dtype=np.float64)
    half = dk // 2
    freq = 1.0 / (10000.0 ** np.linspace(0.0, 1.0, half))
    ang = pos[:, None] * freq[None, :]
    cos = jnp.asarray(np.tile(np.cos(ang), (1, 2 * H)), F32)
    sin = jnp.asarray(np.tile(np.concatenate([-np.sin(ang), np.sin(ang)], axis=1), (1, H)), F32)
    log_gamma = np.log1p(-(2.0 ** (-5.0 - np.arange(H, dtype=np.float64))))
    i = np.arange(C, dtype=np.float64)
    rel = i[:, None] - i[None, :]
    dmask = jnp.asarray(np.where(rel >= 0, np.exp(np.maximum(rel, 0.0)[None] * log_gamma[:, None, None]), 0.0), F32)
    xi = jnp.asarray(np.exp((i + 1.0)[None, :] * log_gamma[:, None])[:, :, None], F32)
    zeta = jnp.asarray(np.exp((C - 1.0 - i)[None, :] * log_gamma[:, None])[:, :, None], F32)
    gamma_c = jnp.asarray(np.exp(C * log_gamma), F32)

    wq = H * dk
    wv = H * dv
    R = min(RET_ROWS, T)
    n = T // R
    full = lambda shape: pl.BlockSpec(shape, lambda b, c: (0,) * len(shape))
    return pl.pallas_call(
        _ret_kernel,
        grid=(B, n),
        in_specs=[pl.BlockSpec(memory_space=pltpu.SMEM),
                  pl.BlockSpec((R, wq), lambda b, c: (b * n + c, ZB_RQ // wq)),
                  pl.BlockSpec((R, wq), lambda b, c: (b * n + c, ZB_RK // wq)),
                  pl.BlockSpec((R, wv), lambda b, c: (b * n + c, ZB_RV // wv)),
                  pl.BlockSpec((R, wv), lambda b, c: (b * n + c, ZF_RG // wv)),
                  pl.BlockSpec((R, wq), lambda b, c: (c, 0)),
                  pl.BlockSpec((R, wq), lambda b, c: (c, 0)),
                  full((H, C, C)), full((H, C, 1)), full((H, C, 1))],
        out_specs=pl.BlockSpec((R, wv), lambda b, c: (b * n + c, 0)),
        out_shape=jax.ShapeDtypeStruct((B * T, wv), BF16),
        scratch_shapes=[pltpu.VMEM((H, dk, dv), F32)],
        compiler_params=_params("parallel", "arbitrary"),
        name="retention",
    )(gamma_c, zb, zb, zb, zf, cos, sin, dmask, xi, zeta)


def _hgrn_levels(C):
    ms, m = [], C // 2
    while m >= SUBLANES:
        ms.append(m)
        m //= 2
    return ms


def _cumsum_rows(tri, x):
    hi = x.astype(BF16)
    rest = x - hi.astype(F32)
    mid = rest.astype(BF16)
    lo = (rest - mid.astype(F32)).astype(BF16)
    return _dot(tri, hi) + (_dot(tri, mid) + _dot(tri, lo))


def _hgrn_kernel(layer, q_ref, f_ref, v_ref, g_ref, lbraw_ref, gain_ref, tri_ref, lmask_ref, o_ref, st_ref):
    TR = q_ref.shape[0]
    C = tri_ref.shape[0]
    H, dk, dv = HGRN_HEADS, HGRN_EXPAND, HGRN_DV
    SB = SUBLANES

    @pl.when(pl.program_id(1) == 0)
    def _():
        st_ref[...] = jnp.zeros_like(st_ref)

    raw = lbraw_ref[...]
    e = jnp.exp(raw - jnp.max(raw, axis=0, keepdims=True))
    soft = e / jnp.sum(e, axis=0, keepdims=True)
    cs = soft[0:1]
    for l in range(1, layer + 1):
        cs = cs + soft[l:l + 1]
    lb = jnp.clip(cs - soft[0:1], 0.0, 1.0)
    tri = tri_ref[...]
    row_in_blk = lax.broadcasted_iota(I32, (SB, dk), 0)
    lane_t = lax.broadcasted_iota(I32, (SB, C), 1)
    levels = _hgrn_levels(C)

    def chunk(c, carry):
        r0 = c * C
        sig_pos, sig_neg = _sigmoid_pair(f_ref[pl.ds(r0, C), :])
        f = lb + (1.0 - lb) * sig_pos
        log_f = jnp.log(jnp.maximum(f, F_FLOOR))
        kk = (1.0 - lb) * sig_neg
        b = _cumsum_rows(tri, log_f)
        qq = q_ref[pl.ds(r0, C), :].astype(F32)
        vb = v_ref[pl.ds(r0, C), :]
        eb = jnp.exp(b)
        b_last = b[C - 1:C, :]
        eb_last = eb[C - 1:C, :]
        q_dec = (qq * eb).astype(BF16)
        k_dec = (kk * jnp.exp(b_last - b)).astype(BF16)
        outs = []
        for h in range(H):
            sl = slice(h * dk, (h + 1) * dk)
            bh, qh, kh, vh = b[:, sl], qq[:, sl], kk[:, sl], vb[:, h * dv:(h + 1) * dv]
            diag = []
            for blk in range(C // SB):
                bs = bh[blk * SB:(blk + 1) * SB, :]
                ks = kh[blk * SB:(blk + 1) * SB, :]
                at = jnp.zeros((SB, C), F32)
                for tt in range(SB):
                    t = blk * SB + tt
                    diff = jnp.where(row_in_blk <= tt, bh[t:t + 1, :] - bs, NEG_BIG)
                    p = (qh[t:t + 1, :] * jnp.exp(diff)) * ks
                    at = jnp.where(lane_t == t, jnp.sum(p, axis=-1, keepdims=True), at)
                diag.append(at)
            a_t = jnp.concatenate(diag, axis=0)
            for lev, m in enumerate(levels):
                qs, ks = [], []
                for blk in range(C // m):
                    rows = slice(blk * m, (blk + 1) * m)
                    if blk % 2 == 1:
                        ref = bh[blk * m - 1:blk * m, :]
                        qs.append(qh[rows] * jnp.exp(bh[rows] - ref))
                        ks.append(jnp.zeros((m, dk), F32))
                    else:
                        ref = bh[(blk + 1) * m - 1:(blk + 1) * m, :]
                        ks.append(kh[rows] * jnp.exp(ref - bh[rows]))
                        qs.append(jnp.zeros((m, dk), F32))
                q_l = jnp.concatenate(qs, axis=0).astype(BF16)
                k_l = jnp.concatenate(ks, axis=0).astype(BF16)
                a_t = a_t + _dot_nt(k_l, q_l) * lmask_ref[lev]
            intra = _dot_tn(a_t.astype(BF16), vh)
            st = st_ref[h]
            outs.append(intra + _dot_nt(q_dec[:, sl], st.astype(BF16)))
            st_ref[h] = st * eb_last[:, sl] + _dot_tn(vh, k_dec[:, sl])
        o = _rms(jnp.concatenate(outs, axis=1), gain_ref[...])
        gg = g_ref[pl.ds(r0, C), :]
        o_ref[pl.ds(r0, C), :] = (gg * _sigmoid(gg) * o).astype(BF16)
        return carry

    for c in range(TR // C):
        chunk(c, 0)


def _hgrn(zb, zf, hgrn_lb, gain, layer, B, T):
    H, dk, dv = HGRN_HEADS, HGRN_EXPAND, HGRN_DV
    C = min(HGRN_PAIR_CHUNK, T)
    TR = min(HGRN_ROWS, T)
    n = T // TR
    w = H * dk
    L = hgrn_lb.shape[0]
    tri = jnp.tril(jnp.ones((C, C), BF16))
    idx = np.arange(C)
    lmask = np.stack([(((idx[None, :] // m) % 2 == 1) & (idx[:, None] // m == idx[None, :] // m - 1))
                      for m in _hgrn_levels(C)]).astype(np.float32)
    return pl.pallas_call(
        functools.partial(_hgrn_kernel, layer),
        grid=(B, n),
        in_specs=[pl.BlockSpec((TR, w), lambda b, c: (b * n + c, ZB_HQ // w)),
                  pl.BlockSpec((TR, w), lambda b, c: (b * n + c, ZF_HF // w)),
                  pl.BlockSpec((TR, w), lambda b, c: (b * n + c, ZB_HV // w)),
                  pl.BlockSpec((TR, w), lambda b, c: (b * n + c, ZF_HG // w)),
                  pl.BlockSpec((L, w), lambda b, c: (0, 0)),
                  pl.BlockSpec((1, w), lambda b, c: (0, 0)),
                  pl.BlockSpec((C, C), lambda b, c: (0, 0)),
                  pl.BlockSpec(lmask.shape, lambda b, c: (0, 0, 0))],
        out_specs=pl.BlockSpec((TR, w), lambda b, c: (b * n + c, 0)),
        out_shape=jax.ShapeDtypeStruct((B * T, w), BF16),
        scratch_shapes=[pltpu.VMEM((H, dv, dk), F32)],
        compiler_params=_params("parallel", "arbitrary"),
        name="hgrn2",
    )(zb, zf, zb, zf, hgrn_lb, gain, tri, jnp.asarray(lmask))


def _dsa_prep_kernel(cq_ref, kv_ref, iw_ref, qn_ref, kn_ref, wq_ref, wqi_ref, wukt_ref,
                     qlt_ref, qit_ref, w_ref, ckv_ref, ckvt_ref, ik_ref):
    tm = cq_ref.shape[0]
    H, dh, HI, dI, Dc = DSA_HEADS, DSA_DH, IDX_HEADS, IDX_DIM, DSA_KV_RANK
    QB = qlt_ref.shape[2] // H
    nb = tm // QB
    cq = _rms(cq_ref[...].astype(F32), qn_ref[...]).astype(BF16)
    ckv = _rms(kv_ref[:, :Dc].astype(F32), kn_ref[...])
    ckv_ref[...] = ckv.astype(BF16)
    KB = ckvt_ref.shape[3]
    for j in range(tm // KB):
        ckvt_ref[0, j, 0:Dc, :] = ckv[j * KB:(j + 1) * KB].T.astype(BF16)
        ckvt_ref[0, j, Dc:Dc + SUBLANES, :] = jnp.ones((SUBLANES, KB), BF16)
    ik_ref[...] = kv_ref[:, Dc:Dc + dI]
    w_t = iw_ref[...].T[0:HI, :] * (HI * dI) ** -0.5
    qi_all = _dot(cq, wqi_ref[...])
    q_all = _dot(cq, wq_ref[...])
    for j in range(nb):
        rows = slice(j * QB, (j + 1) * QB)
        qi_t = qi_all[rows].T
        q_t = q_all[rows].T.astype(BF16)
        for h in range(HI):
            qit_ref[j, :, h * QB:(h + 1) * QB] = qi_t[h * dI:(h + 1) * dI].astype(BF16)
            w_ref[j, :, h * QB:(h + 1) * QB] = w_t[h:h + 1, rows]
        for h in range(H):
            ql_t = _dot(wukt_ref[h], q_t[h * dh:(h + 1) * dh]) * dh ** -0.5
            qlt_ref[j, :, h * QB:(h + 1) * QB] = ql_t.astype(BF16)


def _dsa_prep(zb, zf, qn, kn, wq, wqi, wuk, B, T):
    M = B * T
    H, dh, HI, dI, Dc, Rq = DSA_HEADS, DSA_DH, IDX_HEADS, IDX_DIM, DSA_KV_RANK, DSA_Q_RANK
    QB = min(Q_BLOCK, T)
    KB = KEY_BLOCK
    tm = min(DSA_PREP_ROWS, T)
    nb = tm // QB
    nt = T // tm
    kpt = tm // KB
    return pl.pallas_call(
        _dsa_prep_kernel,
        grid=(M // tm,),
        in_specs=[pl.BlockSpec((tm, Rq), lambda i: (i, ZB_CQ // Rq)),
                  pl.BlockSpec((tm, 2 * Dc), lambda i: (i, ZB_CKV // (2 * Dc))),
                  pl.BlockSpec((tm, LANES), lambda i: (i, ZF_IW // LANES)),
                  pl.BlockSpec((1, Rq), lambda i: (0, 0)),
                  pl.BlockSpec((1, Dc), lambda i: (0, 0)),
                  pl.BlockSpec((Rq, H * dh), lambda i: (0, 0)),
                  pl.BlockSpec((Rq, HI * dI), lambda i: (0, 0)),
                  pl.BlockSpec((H, Dc, dh), lambda i: (0, 0, 0))],
        out_specs=[pl.BlockSpec((nb, Dc, H * QB), lambda i: (i, 0, 0)),
                   pl.BlockSpec((nb, dI, HI * QB), lambda i: (i, 0, 0)),
                   pl.BlockSpec((nb, 1, HI * QB), lambda i: (i, 0, 0)),
                   pl.BlockSpec((tm, Dc), lambda i: (i, 0)),
                   pl.BlockSpec((1, kpt, Dc + SUBLANES, KB), lambda i: (i // nt, i % nt, 0, 0)),
                   pl.BlockSpec((tm, dI), lambda i: (i, 0))],
        out_shape=[jax.ShapeDtypeStruct((M // QB, Dc, H * QB), BF16),
                   jax.ShapeDtypeStruct((M // QB, dI, HI * QB), BF16),
                   jax.ShapeDtypeStruct((M // QB, 1, HI * QB), F32),
                   jax.ShapeDtypeStruct((M, Dc), BF16),
                   jax.ShapeDtypeStruct((B, nt * kpt, Dc + SUBLANES, KB), BF16),
                   jax.ShapeDtypeStruct((M, dI), BF16)],
        compiler_params=_params("parallel"),
        name="dsa_prep",
    )(zb, zb, zf, qn, kn, wq, wqi, wuk)


def _sortable_key(s):
    s = jnp.where(s == 0.0, 0.0, s)
    bits = pltpu.bitcast(s, I32)
    return bits ^ ((bits >> 31) & 0x7FFFFFFF)


def _two_per_trip(lo, hi, body, init):
    def pair(j, carry):
        return body(lo + 2 * j + 1, body(lo + 2 * j, carry))

    carry = lax.fori_loop(0, (hi - lo) // 2, pair, init)
    return lax.cond((hi - lo) % 2 == 1, lambda c: body(hi - 1, c), lambda c: c, carry)


def _dsa_attn_kernel(topk, n_keys, neg_key, qlt_ref, qit_ref, w_ref, ckv_ref, ckvt_ref, ik_ref, bias_ref,
                     wuvt_ref, o_ref, key_ref, z_ref, acc_ref, tie_ref):
    H, HI = DSA_HEADS, IDX_HEADS
    QB = o_ref.shape[0]
    KB = key_ref.shape[1]
    q0 = pl.program_id(1) * QB
    nkb = (q0 + QB - 1) // KB + 1
    n_skip = n_keys - nkb * KB

    s_row = lax.broadcasted_iota(I32, (KB, QB), 0)
    t_row = q0 + lax.broadcasted_iota(I32, (1, QB), 1)

    qit = qit_ref[0]
    w_row = w_ref[0]

    def score_blk(causal_edge, kb, carry):
        k0 = pl.multiple_of(kb * KB, KB)
        p = _dot(ik_ref[pl.ds(k0, KB), :], qit)
        p = jnp.maximum(p, 0.0) * w_row
        s = p[:, 0:QB]
        for h in range(1, HI):
            s = s + p[:, h * QB:(h + 1) * QB]
        if causal_edge:
            s = jnp.where(k0 + s_row <= t_row, s, NEG_BIG)
        key_ref[kb] = _sortable_key(s)
        return carry

    n_past = q0 // KB
    _two_per_trip(0, n_past, functools.partial(score_blk, False), 0)
    lax.fori_loop(n_past, nkb, functools.partial(score_blk, True), 0)

    def count(hit_fn):
        def body(kb, acc):
            return acc + _group_sum(jnp.where(hit_fn(kb, key_ref[kb]), 1, 0))

        acc = _two_per_trip(0, nkb, body, jnp.zeros((SUBLANES, QB), I32))
        return jnp.sum(acc, axis=0, keepdims=True)

    def count_ge(cand):
        return count(lambda kb, kk: kk >= cand) + jnp.where(cand <= neg_key, n_skip, 0)

    zero = jnp.zeros((1, QB), I32)
    cnt0 = count_ge(zero)
    state = (jnp.where(cnt0 >= topk, zero, jnp.full((1, QB), INT_MIN, I32)),
             jnp.where(cnt0 >= topk, cnt0, n_keys))

    def bisect(it, state):
        thr, cnt = state
        cand = thr | jnp.left_shift(jnp.int32(1), 30 - it)
        c = count_ge(cand)
        return jnp.where(c >= topk, cand, thr), jnp.where(c >= topk, c, cnt)

    thr, cnt = lax.fori_loop(0, 31, bisect, state)

    nbits = max(1, (n_keys - 1).bit_length())
    tie_ref[...] = jnp.full(tie_ref.shape, (1 << nbits) - 1, I32)

    @pl.when(jnp.max(jnp.where(cnt > topk, 1, 0)) > 0)
    def _():
        n_gt = count(lambda kb, kk: kk > thr) + jnp.where(thr < neg_key, n_skip, 0)
        need = topk - n_gt

        def ibisect(it, p):
            cand = p | jnp.left_shift(jnp.int32(1), nbits - 1 - it)
            below = count(lambda kb, kk: (kk == thr) & (kb * KB + s_row < cand))
            return jnp.where(below < need, cand, p)

        p = lax.fori_loop(0, nbits, ibisect, jnp.zeros((1, QB), I32))
        tie_ref[...] = jnp.broadcast_to(p, tie_ref.shape)

    tie = tie_ref[0:1, :]

    qlt = qlt_ref[0]

    FAR_TILE = bias_ref.shape[0] - 1
    n_far = jnp.maximum((q0 - FAR_TILE * QB) // KB + 1, 0)
    far_bias = bias_ref[FAR_TILE, 0:1, :]

    def logits_blk(near, kb, m8):
        k0 = pl.multiple_of(kb * KB, KB)
        kk = key_ref[kb]
        s_glob = k0 + s_row
        sel = (kk > thr) | ((kk == thr) & (s_glob <= tie))
        z = _dot(ckv_ref[pl.ds(k0, KB), :], qlt)
        if near:
            sel = sel & (s_glob <= t_row)
            z = z + bias_ref[(q0 - k0) // QB]
        tops = []
        for h in range(H):
            zh = jnp.where(sel, z[:, h * QB:(h + 1) * QB], NEG_BIG)
            z_ref[kb, :, h * QB:(h + 1) * QB] = zh
            tops.append(_group_max(zh))
        return jnp.maximum(m8, jnp.concatenate(tops, axis=1))

    neg8 = jnp.full((SUBLANES, H * QB), NEG_BIG, F32)
    m8_far = _two_per_trip(0, n_far, functools.partial(logits_blk, False), neg8)
    m8_near = _two_per_trip(n_far, nkb, functools.partial(logits_blk, True), neg8)
    m = jnp.maximum(jnp.max(m8_far, axis=0, keepdims=True) + far_bias,
                    jnp.max(m8_near, axis=0, keepdims=True))
    m_far = m - far_bias
    acc_ref[...] = jnp.zeros(acc_ref.shape, F32)

    def pv_blk(kb):
        p = jnp.exp((z_ref[kb] - jnp.where(kb < n_far, m_far, m)).astype(BF16))
        return _dot(ckvt_ref[0, kb], p)

    def pv_pair(j, carry):
        acc_ref[...] += pv_blk(2 * j) + pv_blk(2 * j + 1)
        return carry

    lax.fori_loop(0, nkb // 2, pv_pair, 0)

    @pl.when(nkb % 2 == 1)
    def _():
        acc_ref[...] += pv_blk(nkb - 1)

    Dc = acc_ref.shape[0] - SUBLANES
    inv_l = 1.0 / acc_ref[Dc:Dc + 1, :]
    o_lat_t = (acc_ref[0:Dc, :] * inv_l).astype(BF16)
    y_t = jnp.concatenate([_dot(wuvt_ref[h], o_lat_t[:, h * QB:(h + 1) * QB]) for h in range(H)], axis=0)
    o_ref[...] = y_t.T.astype(BF16)


def _dsa_attn(qlt, qit, w_row, ckvn, ckvt, idxk, bias_tiles, wuvt, B, T):
    H, HI, dI, Dc, dh = DSA_HEADS, IDX_HEADS, IDX_DIM, DSA_KV_RANK, DSA_DH
    QB = min(Q_BLOCK, T)
    KB = KEY_BLOCK
    nq = T // QB
    nk = T // KB
    topk = min(DSA_TOPK_MAX, T // 4)
    neg_key = int(np.array(NEG_BIG, np.float32).view(np.int32))
    neg_key = neg_key ^ ((neg_key >> 31) & 0x7FFFFFFF)
    return pl.pallas_call(
        functools.partial(_dsa_attn_kernel, topk, T, neg_key),
        grid=(B, nq),
        in_specs=[pl.BlockSpec((1, Dc, H * QB), lambda b, i: (b * nq + i, 0, 0)),
                  pl.BlockSpec((1, dI, HI * QB), lambda b, i: (b * nq + i, 0, 0)),
                  pl.BlockSpec((1, 1, HI * QB), lambda b, i: (b * nq + i, 0, 0)),
                  pl.BlockSpec((T, Dc), lambda b, i: (b, 0)),
                  pl.BlockSpec((1, nk, Dc + SUBLANES, KB), lambda b, i: (b, 0, 0, 0)),
                  pl.BlockSpec((T, dI), lambda b, i: (b, 0)),
                  pl.BlockSpec((4, KB, H * QB), lambda b, i: (0, 0, 0), pipeline_mode=pl.Buffered(1)),
                  pl.BlockSpec((H, dh, Dc), lambda b, i: (0, 0, 0))],
        out_specs=pl.BlockSpec((QB, DSA_W), lambda b, i: (b * nq + i, 0)),
        out_shape=jax.ShapeDtypeStruct((B * T, DSA_W), BF16),
        scratch_shapes=[pltpu.VMEM((nk, KB, QB), I32),
                        pltpu.VMEM((nk, KB, H * QB), F32),
                        pltpu.VMEM((Dc + SUBLANES, H * QB), F32),
                        pltpu.VMEM((SUBLANES, QB), I32)],
        compiler_params=_params("parallel", "arbitrary"),
        name="dsa_attn",
    )(qlt, qit, w_row, ckvn, ckvt, idxk, bias_tiles, wuvt)


def _rel_bias_tiles(rel_bias, QB, KB):
    max_exact = REL_BUCKETS // 2
    n_far = 3 * QB
    assert n_far - (KB - 1) > REL_MAX_DIST
    n = jnp.arange(n_far, dtype=jnp.int32)
    nf = jnp.maximum(n, max_exact).astype(F32)
    large = max_exact + (jnp.log(nf / max_exact) / math.log(REL_MAX_DIST / max_exact)
                         * (REL_BUCKETS - max_exact)).astype(jnp.int32)
    large = jnp.minimum(large, REL_BUCKETS - 1)
    bucket = jnp.where(n < max_exact, n, large)
    H = rel_bias.shape[1]
    tab = jnp.take(rel_bias, bucket, axis=0).T
    P = KB + QB
    tiles = []
    for delta in (0, QB, 2 * QB):
        d = np.concatenate([np.arange(QB + 1), np.arange(-(KB - 1), 0)])
        g = tab[:, np.clip(delta + d, 0, n_far - 1)]
        rows = jnp.tile(g, (1, KB))[:, :KB * (P - 1)].reshape(H, KB, P - 1)[:, :, :QB]
        tiles.append(rows.transpose(1, 0, 2).reshape(KB, H * QB))
    tiles.append(jnp.broadcast_to(jnp.repeat(rel_bias[REL_BUCKETS - 1], QB)[None, :], (KB, H * QB)))
    return jnp.stack(tiles).astype(F32)


def _merge_kernel(x_ref, yr_ref, yd_ref, yh_ref, lnpre_ref, wg_ref, wr_ref, wd_ref, wh_ref, wo_ref,
                  ln_ref, o_ref):
    D = x_ref.shape[1]
    x = x_ref[...]
    h = _rms(x, lnpre_ref[...]).astype(BF16)
    m = None
    for k, (y_ref, w_ref) in enumerate(((yr_ref, wr_ref), (yd_ref, wd_ref), (yh_ref, wh_ref))):
        gate = _sigmoid(_dot(h, wg_ref[:, k * D:(k + 1) * D]))
        term = gate * _dot(y_ref[...], w_ref[...])
        m = term if m is None else m + term
    u = _dot(m.astype(BF16), wo_ref[...])
    o_ref[...] = x + _rms(u, ln_ref[...])


def _merge(x2, y_ret, y_dsa, y_hg, lnpre, wg, wr, wd, wh, wo, ln):
    M = x2.shape[0]
    tm = min(PROJ_ROWS, M)
    D = D_MODEL
    row = lambda w: pl.BlockSpec((tm, w), lambda i: (i, 0))
    const = lambda r, c: pl.BlockSpec((r, c), lambda i: (0, 0))
    return pl.pallas_call(
        _merge_kernel,
        grid=(M // tm,),
        in_specs=[row(D), row(RET_W), row(DSA_W), row(HGRN_W), const(1, D), const(D, N_BRANCH * D),
                  const(RET_W, D), const(DSA_W, D), const(HGRN_W, D), const(D, D), const(1, D)],
        out_specs=row(D),
        out_shape=jax.ShapeDtypeStruct((M, D), F32),
        compiler_params=_params("parallel"),
        name="merge",
    )(x2, y_ret, y_dsa, y_hg, lnpre, wg, wr, wd, wh, wo, ln)


def _gelu_tanh(x):
    return 0.5 * x * (1.0 + jnp.tanh(math.sqrt(2.0 / math.pi) * (x + 0.044715 * (x * x * x))))


def _ffn_kernel(tiles_per_seq, x_ref, lnpre_ref, wup_ref, cw_ref, cb_ref, wdn_ref, lnpost_ref,
                o_ref, buf_ref, prev_ref):
    tm = x_ref.shape[0]
    HALO = SUBLANES
    fc = buf_ref.shape[1]
    n_pass = D_FF // fc
    first = (pl.program_id(0) % tiles_per_seq) == 0

    @pl.when(first)
    def _():
        prev_ref[...] = jnp.zeros_like(prev_ref)

    x = x_ref[...]
    h = _rms(x, lnpre_ref[...]).astype(BF16)

    def conv(part, c):
        col = part * D_FF + c * fc
        up = _dot(h, wup_ref[:, col:col + fc])
        slot = part * n_pass + c
        buf_ref[0:HALO, :] = prev_ref[slot]
        buf_ref[HALO:HALO + tm, :] = up
        prev_ref[slot] = up[tm - HALO:tm, :]
        w = cw_ref[:, col:col + fc]
        y = (up * w[2:3] + buf_ref[HALO - 1:HALO - 1 + tm, :] * w[1:2]
             + buf_ref[HALO - 2:HALO - 2 + tm, :] * w[0:1])
        return y + cb_ref[:, col:col + fc]

    acc = None
    for c in range(n_pass):
        a = conv(0, c)
        u = conv(1, c)
        act = (_gelu_tanh(a) * u).astype(BF16)
        d = _dot(act, wdn_ref[c * fc:(c + 1) * fc, :])
        acc = d if acc is None else acc + d
    o_ref[...] = x + _rms(acc, lnpost_ref[...])


def _ffn(x2, lnpre, wup, cw, cb, wdn, lnpost, T):
    M = x2.shape[0]
    D = D_MODEL
    tm = min(FFN_ROWS, T)
    fc = D_FF
    const = lambda r, c: pl.BlockSpec((r, c), lambda i: (0, 0))
    return pl.pallas_call(
        functools.partial(_ffn_kernel, T // tm),
        grid=(M // tm,),
        in_specs=[pl.BlockSpec((tm, D), lambda i: (i, 0)), const(1, D), const(D, 2 * D_FF),
                  const(CONV_WIDTH, 2 * D_FF), const(1, 2 * D_FF), const(D_FF, D), const(1, D)],
        out_specs=pl.BlockSpec((tm, D), lambda i: (i, 0)),
        out_shape=jax.ShapeDtypeStruct((M, D), F32),
        scratch_shapes=[pltpu.VMEM((tm + SUBLANES, fc), F32),
                        pltpu.VMEM((2 * (D_FF // fc), SUBLANES, fc), F32)],
        compiler_params=_params("arbitrary"),
        name="conv_ffn",
    )(x2, lnpre, wup, cw, cb, wdn, lnpost)


def _split_w_in(w):
    widths = (256, 256, 512, 512, 256, 128, 64, 8, 512, 512, 512, 512, N_BRANCH * D_MODEL)
    offs = np.concatenate([[0], np.cumsum(widths)])
    rq, rk, rv, rg, cq, ckv, ik, iw, hq, hf, hv, hg, gt = [w[..., offs[k]:offs[k + 1]] for k in range(len(widths))]
    zeros = lambda n: jnp.zeros(w.shape[:-1] + (n,), w.dtype)
    wb = jnp.concatenate([rq, rk, rv, cq, ckv, ik, zeros(2 * DSA_KV_RANK - DSA_KV_RANK - IDX_DIM), hq, hv], axis=-1)
    wf = jnp.concatenate([rg, hf, hg, iw, zeros(LANES - IDX_HEADS)], axis=-1)
    assert wb.shape[-1] == ZB_W and wf.shape[-1] == ZF_W
    return wb.astype(BF16), wf.astype(BF16), gt.astype(BF16)


def kernel(x, rel_bias, hgrn_lb, ln_mix_pre, ln_mix_post, ln_ffn_pre, ln_ffn_post, w_in, dsa_q_norm, dsa_kv_norm, dsa_w_uq, dsa_w_uk, dsa_w_uv, hgrn_norm, w_br_ret, w_br_dsa, w_br_hgrn, w_out, ffn_w_up, ffn_conv_w, ffn_conv_b, ffn_w_down):
    B, T, D = x.shape
    depth = w_in.shape[0]
    assert T % KEY_BLOCK == 0 and D == D_MODEL
    H, dh, Rq = DSA_HEADS, DSA_DH, DSA_Q_RANK
    bias_tiles = _rel_bias_tiles(rel_bias, min(Q_BLOCK, T), KEY_BLOCK)
    wq = dsa_w_uq[..., :H * dh].astype(BF16)
    wqi = dsa_w_uq[..., H * dh:].astype(BF16)
    wuk = dsa_w_uk.transpose(0, 1, 3, 2).astype(BF16)
    wuvt = dsa_w_uv.transpose(0, 1, 3, 2).astype(BF16)
    x2 = x.reshape(B * T, D)
    row = lambda v: v.reshape(1, -1)
    for l in range(depth):
        wb, wf, wg = _split_w_in(w_in[l])
        zb, zf = _inproj(x2, row(ln_mix_pre[l]), wb, wf)
        y_ret = _retention(zb, zf, B, T)
        y_hg = _hgrn(zb, zf, hgrn_lb, row(hgrn_norm[l]), l, B, T)
        qlt, qit, w_row, ckvn, ckvt, idxk = _dsa_prep(zb, zf, row(dsa_q_norm[l]), row(dsa_kv_norm[l]),
                                                      wq[l], wqi[l], wuk[l], B, T)
        y_dsa = _dsa_attn(qlt, qit, w_row, ckvn, ckvt, idxk, bias_tiles, wuvt[l], B, T)
        x2 = _merge(x2, y_ret, y_dsa, y_hg, row(ln_mix_pre[l]), wg, w_br_ret[l].astype(BF16),
                    w_br_dsa[l].astype(BF16), w_br_hgrn[l].astype(BF16), w_out[l].astype(BF16),
                    row(ln_mix_post[l]))
        x2 = _ffn(x2, row(ln_ffn_pre[l]), ffn_w_up[l].astype(BF16), ffn_conv_w[l], row(ffn_conv_b[l]),
                  ffn_w_down[l].astype(BF16), row(ln_ffn_post[l]), T)
    return x2.reshape(B, T, D)
```

```python
import functools
import math

import jax
import jax.numpy as jnp
import numpy as np
from jax import lax
from jax.experimental import pallas as pl
from jax.experimental.pallas import tpu as pltpu

F32 = jnp.float32
BF16 = jnp.bfloat16
I32 = jnp.int32

D_MODEL = 1024
RET_HEADS, RET_DK, RET_DV, RET_CHUNK = 4, 64, 128, 128
DSA_HEADS, DSA_DH, DSA_Q_RANK, DSA_KV_RANK = 8, 64, 256, 128
IDX_HEADS, IDX_DIM, DSA_TOPK_MAX = 8, 64, 256
Q_BLOCK = 256
HGRN_HEADS, HGRN_EXPAND, HGRN_DV = 4, 128, 128
F_FLOOR = 1e-6
REL_BUCKETS, REL_MAX_DIST = 32, 128
D_FF = 2816
CONV_WIDTH = 3
N_BRANCH = 3
EPS = 1e-6
NEG_BIG = -1e30

RET_W = RET_HEADS * RET_DV
DSA_W = DSA_HEADS * DSA_DH
HGRN_KW = HGRN_HEADS * HGRN_EXPAND
HGRN_W = HGRN_HEADS * HGRN_DV

ZB_RQ, ZB_RK, ZB_RV, ZB_CQ = 0, 256, 512, 1024
ZB_CKV = 1280
ZB_HQ, ZB_HV = 1536, 2048
ZB_W = 2560
ZF_RG, ZF_HF, ZF_HG, ZF_IW = 0, 512, 1024, 1536
ZF_W = 1664

VMEM_LIMIT_BYTES = 56 * 1024 * 1024
SUBLANES = 8
LANES = 128

KEY_BLOCK = 256
DSA_PREP_ROWS = 2048
PROJ_ROWS = 1024
FFN_ROWS = 512
HGRN_ROWS = 512
RET_ROWS = 256
HGRN_PAIR_CHUNK = 256
INT_MIN = -(2 ** 31)


def _params(*sem):
    return pltpu.CompilerParams(dimension_semantics=sem, vmem_limit_bytes=VMEM_LIMIT_BYTES)


def _dot(a, b):
    return jnp.dot(a, b, preferred_element_type=F32)


def _dot_nt(a, b):
    return lax.dot_general(a, b, (((1,), (1,)), ((), ())), preferred_element_type=F32)


def _dot_tn(a, b):
    return lax.dot_general(a, b, (((0,), (0,)), ((), ())), preferred_element_type=F32)


def _sigmoid(x):
    return 1.0 / (1.0 + jnp.exp(-x))


def _sigmoid_pair(x):
    t = jnp.exp(-jnp.abs(x))
    big = 1.0 / (1.0 + t)
    small = t * big
    pos = x >= 0.0
    return jnp.where(pos, big, small), jnp.where(pos, small, big)


def _rms(x, g):
    return x * lax.rsqrt(jnp.mean(x * x, axis=-1, keepdims=True) + EPS) * g


def _group_sum(x):
    return jnp.sum(x.reshape(x.shape[0] // SUBLANES, SUBLANES, x.shape[1]), axis=0)


def _group_max(x):
    return jnp.max(x.reshape(x.shape[0] // SUBLANES, SUBLANES, x.shape[1]), axis=0)


def _inproj_kernel(x_ref, g_ref, wb_ref, wf_ref, zb_ref, zf_ref):
    h = _rms(x_ref[...], g_ref[...]).astype(BF16)
    zb_ref[...] = _dot(h, wb_ref[...]).astype(BF16)
    zf_ref[...] = _dot(h, wf_ref[...])


def _inproj(x2, g, wb, wf):
    M = x2.shape[0]
    tm = min(PROJ_ROWS, M)
    const = lambda r, c: pl.BlockSpec((r, c), lambda i: (0, 0))
    return pl.pallas_call(
        _inproj_kernel,
        grid=(M // tm,),
        in_specs=[pl.BlockSpec((tm, D_MODEL), lambda i: (i, 0)), const(1, D_MODEL),
                  const(D_MODEL, ZB_W), const(D_MODEL, ZF_W)],
        out_specs=[pl.BlockSpec((tm, ZB_W), lambda i: (i, 0)), pl.BlockSpec((tm, ZF_W), lambda i: (i, 0))],
        out_shape=[jax.ShapeDtypeStruct((M, ZB_W), BF16), jax.ShapeDtypeStruct((M, ZF_W), F32)],
        compiler_params=_params("parallel"),
        name="inproj",
    )(x2, g, wb, wf)


def _ret_kernel(gam_ref, q_ref, k_ref, v_ref, g_ref, cos_ref, sin_ref, dm_ref, xi_ref, zeta_ref,
                o_ref, r_ref):
    rows = q_ref.shape[0]
    C = dm_ref.shape[1]
    H, dk, dv = RET_HEADS, RET_DK, RET_DV

    @pl.when(pl.program_id(1) == 0)
    def _():
        r_ref[...] = jnp.zeros_like(r_ref)

    cos = cos_ref[...]
    sin = sin_ref[...]
    lane = lax.broadcasted_iota(I32, (rows, H * dk), 1)
    first_half = (lane % dk) < (dk // 2)

    def rot(x):
        swapped = jnp.where(first_half, pltpu.roll(x, H * dk - dk // 2, 1), pltpu.roll(x, dk // 2, 1))
        return x * cos + swapped * sin

    q = rot(q_ref[...].astype(F32))
    k = rot(k_ref[...].astype(F32)) * dk ** -0.5
    for c in range(rows // C):
        rs = slice(c * C, (c + 1) * C)
        for h in range(H):
            qh = q[rs, h * dk:(h + 1) * dk].astype(BF16)
            kh = k[rs, h * dk:(h + 1) * dk]
            vh = v_ref[rs, h * dv:(h + 1) * dv]
            s = _dot_nt(qh, kh.astype(BF16)) * dm_ref[h]
            inner = _dot(s.astype(BF16), vh)
            rh = r_ref[h]
            cross = _dot(qh, rh.astype(BF16)) * xi_ref[h]
            r_ref[h] = gam_ref[h] * rh + _dot_tn((kh * zeta_ref[h]).astype(BF16), vh)
            o = inner + cross
            mu = jnp.mean(o, axis=-1, keepdims=True)
            oc = o - mu
            var = jnp.mean(oc * oc, axis=-1, keepdims=True)
            gh = g_ref[rs, h * dv:(h + 1) * dv]
            o_ref[rs, h * dv:(h + 1) * dv] = (gh * _sigmoid(gh) * (oc * lax.rsqrt(var + EPS))).astype(BF16)


def _retention(zb, zf, B, T):
    H, dk, dv = RET_HEADS, RET_DK, RET_DV
    C = min(RET_CHUNK, T)
    n = T // C
    pos = np.arange(T, dtype=np.float64)
    half = dk // 2
    freq = 1.0 / (10000.0 ** np.linspace(0.0, 1.0, half))
    ang = pos[:, None] * freq[None, :]
    cos = jnp.asarray(np.tile(np.cos(ang), (1, 2 * H)), F32)
    sin = jnp.asarray(np.tile(np.concatenate([-np.sin(ang), np.sin(ang)], axis=1), (1, H)), F32)
    log_gamma = np.log1p(-(2.0 ** (-5.0 - np.arange(H, dtype=np.float64))))
    i = np.arange(C, dtype=np.float64)
    rel = i[:, None] - i[None, :]
    dmask = jnp.asarray(np.where(rel >= 0, np.exp(np.maximum(rel, 0.0)[None] * log_gamma[:, None, None]), 0.0), F32)
    xi = jnp.asarray(np.exp((i + 1.0)[None, :] * log_gamma[:, None])[:, :, None], F32)
    zeta = jnp.asarray(np.exp((C - 1.0 - i)[None, :] * log_gamma[:, None])[:, :, None], F32)
    gamma_c = jnp.asarray(np.exp(C * log_gamma), F32)

    wq = H * dk
    wv = H * dv
    R = min(RET_ROWS, T)
    n = T // R
    full = lambda shape: pl.BlockSpec(shape, lambda b, c: (0,) * len(shape))
    return pl.pallas_call(
        _ret_kernel,
        grid=(B, n),
        in_specs=[pl.BlockSpec(memory_space=pltpu.SMEM),
                  pl.BlockSpec((R, wq), lambda b, c: (b * n + c, ZB_RQ // wq)),
                  pl.BlockSpec((R, wq), lambda b, c: (b * n + c, ZB_RK // wq)),
                  pl.BlockSpec((R, wv), lambda b, c: (b * n + c, ZB_RV // wv)),
                  pl.BlockSpec((R, wv), lambda b, c: (b * n + c, ZF_RG // wv)),
                  pl.BlockSpec((R, wq), lambda b, c: (c, 0)),
                  pl.BlockSpec((R, wq), lambda b, c: (c, 0)),
                  full((H, C, C)), full((H, C, 1)), full((H, C, 1))],
        out_specs=pl.BlockSpec((R, wv), lambda b, c: (b * n + c, 0)),
        out_shape=jax.ShapeDtypeStruct((B * T, wv), BF16),
        scratch_shapes=[pltpu.VMEM((H, dk, dv), F32)],
        compiler_params=_params("parallel", "arbitrary"),
        name="retention",
    )(gamma_c, zb, zb, zb, zf, cos, sin, dmask, xi, zeta)


def _hgrn_levels(C):
    ms, m = [], C // 2
    while m >= SUBLANES:
        ms.append(m)
        m //= 2
    return ms


def _cumsum_rows(tri, x):
    hi = x.astype(BF16)
    rest = x - hi.astype(F32)
    mid = rest.astype(BF16)
    lo = (rest - mid.astype(F32)).astype(BF16)
    return _dot(tri, hi) + (_dot(tri, mid) + _dot(tri, lo))


def _hgrn_kernel(layer, q_ref, f_ref, v_ref, g_ref, lbraw_ref, gain_ref, tri_ref, lmask_ref, o_ref, st_ref):
    TR = q_ref.shape[0]
    C = tri_ref.shape[0]
    H, dk, dv = HGRN_HEADS, HGRN_EXPAND, HGRN_DV
    SB = SUBLANES

    @pl.when(pl.program_id(1) == 0)
    def _():
        st_ref[...] = jnp.zeros_like(st_ref)

    raw = lbraw_ref[...]
    e = jnp.exp(raw - jnp.max(raw, axis=0, keepdims=True))
    soft = e / jnp.sum(e, axis=0, keepdims=True)
    cs = soft[0:1]
    for l in range(1, layer + 1):
        cs = cs + soft[l:l + 1]
    lb = jnp.clip(cs - soft[0:1], 0.0, 1.0)
    tri = tri_ref[...]
    row_in_blk = lax.broadcasted_iota(I32, (SB, dk), 0)
    lane_t = lax.broadcasted_iota(I32, (SB, C), 1)
    levels = _hgrn_levels(C)

    def chunk(c, carry):
        r0 = c * C
        sig_pos, sig_neg = _sigmoid_pair(f_ref[pl.ds(r0, C), :])
        f = lb + (1.0 - lb) * sig_pos
        log_f = jnp.log(jnp.maximum(f, F_FLOOR))
        kk = (1.0 - lb) * sig_neg
        b = _cumsum_rows(tri, log_f)
        qq = q_ref[pl.ds(r0, C), :].astype(F32)
        vb = v_ref[pl.ds(r0, C), :]
        eb = jnp.exp(b)
        b_last = b[C - 1:C, :]
        eb_last = eb[C - 1:C, :]
        q_dec = (qq * eb).astype(BF16)
        k_dec = (kk * jnp.exp(b_last - b)).astype(BF16)
        outs = []
        for h in range(H):
            sl = slice(h * dk, (h + 1) * dk)
            bh, qh, kh, vh = b[:, sl], qq[:, sl], kk[:, sl], vb[:, h * dv:(h + 1) * dv]
            diag = []
            for blk in range(C // SB):
                bs = bh[blk * SB:(blk + 1) * SB, :]
                ks = kh[blk * SB:(blk + 1) * SB, :]
                at = jnp.zeros((SB, C), F32)
                for tt in range(SB):
                    t = blk * SB + tt
                    diff = jnp.where(row_in_blk <= tt, bh[t:t + 1, :] - bs, NEG_BIG)
                    p = (qh[t:t + 1, :] * jnp.exp(diff)) * ks
                    at = jnp.where(lane_t == t, jnp.sum(p, axis=-1, keepdims=True), at)
                diag.append(at)
            a_t = jnp.concatenate(diag, axis=0)
            for lev, m in enumerate(levels):
                qs, ks = [], []
                for blk in range(C // m):
                    rows = slice(blk * m, (blk + 1) * m)
                    if blk % 2 == 1:
                        ref = bh[blk * m - 1:blk * m, :]
                        qs.append(qh[rows] * jnp.exp(bh[rows] - ref))
                        ks.append(jnp.zeros((m, dk), F32))
                    else:
                        ref = bh[(blk + 1) * m - 1:(blk + 1) * m, :]
                        ks.append(kh[rows] * jnp.exp(ref - bh[rows]))
                        qs.append(jnp.zeros((m, dk), F32))
                q_l = jnp.concatenate(qs, axis=0).astype(BF16)
                k_l = jnp.concatenate(ks, axis=0).astype(BF16)
                a_t = a_t + _dot_nt(k_l, q_l) * lmask_ref[lev]
            intra = _dot_tn(a_t.astype(BF16), vh)
            st = st_ref[h]
            outs.append(intra + _dot_nt(q_dec[:, sl], st.astype(BF16)))
            st_ref[h] = st * eb_last[:, sl] + _dot_tn(vh, k_dec[:, sl])
        o = _rms(jnp.concatenate(outs, axis=1), gain_ref[...])
        gg = g_ref[pl.ds(r0, C), :]
        o_ref[pl.ds(r0, C), :] = (gg * _sigmoid(gg) * o).astype(BF16)
        return carry

    for c in range(TR // C):
        chunk(c, 0)


def _hgrn(zb, zf, hgrn_lb, gain, layer, B, T):
    H, dk, dv = HGRN_HEADS, HGRN_EXPAND, HGRN_DV
    C = min(HGRN_PAIR_CHUNK, T)
    TR = min(HGRN_ROWS, T)
    n = T // TR
    w = H * dk
    L = hgrn_lb.shape[0]
    tri = jnp.tril(jnp.ones((C, C), BF16))
    idx = np.arange(C)
    lmask = np.stack([(((idx[None, :] // m) % 2 == 1) & (idx[:, None] // m == idx[None, :] // m - 1))
                      for m in _hgrn_levels(C)]).astype(np.float32)
    return pl.pallas_call(
        functools.partial(_hgrn_kernel, layer),
        grid=(B, n),
        in_specs=[pl.BlockSpec((TR, w), lambda b, c: (b * n + c, ZB_HQ // w)),
                  pl.BlockSpec((TR, w), lambda b, c: (b * n + c, ZF_HF // w)),
                  pl.BlockSpec((TR, w), lambda b, c: (b * n + c, ZB_HV // w)),
                  pl.BlockSpec((TR, w), lambda b, c: (b * n + c, ZF_HG // w)),
                  pl.BlockSpec((L, w), lambda b, c: (0, 0)),
                  pl.BlockSpec((1, w), lambda b, c: (0, 0)),
                  pl.BlockSpec((C, C), lambda b, c: (0, 0)),
                  pl.BlockSpec(lmask.shape, lambda b, c: (0, 0, 0))],
        out_specs=pl.BlockSpec((TR, w), lambda b, c: (b * n + c, 0)),
        out_shape=jax.ShapeDtypeStruct((B * T, w), BF16),
        scratch_shapes=[pltpu.VMEM((H, dv, dk), F32)],
        compiler_params=_params("parallel", "arbitrary"),
        name="hgrn2",
    )(zb, zf, zb, zf, hgrn_lb, gain, tri, jnp.asarray(lmask))


def _dsa_prep_kernel(cq_ref, kv_ref, iw_ref, qn_ref, kn_ref, wq_ref, wqi_ref, wukt_ref,
                     qlt_ref, qit_ref, w_ref, ckv_ref, ckvt_ref, ik_ref):
    tm = cq_ref.shape[0]
    H, dh, HI, dI, Dc = DSA_HEADS, DSA_DH, IDX_HEADS, IDX_DIM, DSA_KV_RANK
    QB = qlt_ref.shape[2] // H
    nb = tm // QB
    cq = _rms(cq_ref[...].astype(F32), qn_ref[...]).astype(BF16)
    ckv = _rms(kv_ref[:, :Dc].astype(F32), kn_ref[...])
    ckv_ref[...] = ckv.astype(BF16)
    KB = ckvt_ref.shape[3]
    for j in range(tm // KB):
        ckvt_ref[0, j, 0:Dc, :] = ckv[j * KB:(j + 1) * KB].T.astype(BF16)
        ckvt_ref[0, j, Dc:Dc + SUBLANES, :] = jnp.ones((SUBLANES, KB), BF16)
    ik_ref[...] = kv_ref[:, Dc:Dc + dI]
    w_t = iw_ref[...].T[0:HI, :] * (HI * dI) ** -0.5
    qi_all = _dot(cq, wqi_ref[...])
    q_all = _dot(cq, wq_ref[...])
    for j in range(nb):
        rows = slice(j * QB, (j + 1) * QB)
        qi_t = qi_all[rows].T
        q_t = q_all[rows].T.astype(BF16)
        for h in range(HI):
            qit_ref[j, :, h * QB:(h + 1) * QB] = qi_t[h * dI:(h + 1) * dI].astype(BF16)
            w_ref[j, :, h * QB:(h + 1) * QB] = w_t[h:h + 1, rows]
        for h in range(H):
            ql_t = _dot(wukt_ref[h], q_t[h * dh:(h + 1) * dh]) * dh ** -0.5
            qlt_ref[j, :, h * QB:(h + 1) * QB] = ql_t.astype(BF16)


def _dsa_prep(zb, zf, qn, kn, wq, wqi, wuk, B, T):
    M = B * T
    H, dh, HI, dI, Dc, Rq = DSA_HEADS, DSA_DH, IDX_HEADS, IDX_DIM, DSA_KV_RANK, DSA_Q_RANK
    QB = min(Q_BLOCK, T)
    KB = KEY_BLOCK
    tm = min(DSA_PREP_ROWS, T)
    nb = tm // QB
    nt = T // tm
    kpt = tm // KB
    return pl.pallas_call(
        _dsa_prep_kernel,
        grid=(M // tm,),
        in_specs=[pl.BlockSpec((tm, Rq), lambda i: (i, ZB_CQ // Rq)),
                  pl.BlockSpec((tm, 2 * Dc), lambda i: (i, ZB_CKV // (2 * Dc))),
                  pl.BlockSpec((tm, LANES), lambda i: (i, ZF_IW // LANES)),
                  pl.BlockSpec((1, Rq), lambda i: (0, 0)),
                  pl.BlockSpec((1, Dc), lambda i: (0, 0)),
                  pl.BlockSpec((Rq, H * dh), lambda i: (0, 0)),
                  pl.BlockSpec((Rq, HI * dI), lambda i: (0, 0)),
                  pl.BlockSpec((H, Dc, dh), lambda i: (0, 0, 0))],
        out_specs=[pl.BlockSpec((nb, Dc, H * QB), lambda i: (i, 0, 0)),
                   pl.BlockSpec((nb, dI, HI * QB), lambda i: (i, 0, 0)),
                   pl.BlockSpec((nb, 1, HI * QB), lambda i: (i, 0, 0)),
                   pl.BlockSpec((tm, Dc), lambda i: (i, 0)),
                   pl.BlockSpec((1, kpt, Dc + SUBLANES, KB), lambda i: (i // nt, i % nt, 0, 0)),
                   pl.BlockSpec((tm, dI), lambda i: (i, 0))],
        out_shape=[jax.ShapeDtypeStruct((M // QB, Dc, H * QB), BF16),
                   jax.ShapeDtypeStruct((M // QB, dI, HI * QB), BF16),
                   jax.ShapeDtypeStruct((M // QB, 1, HI * QB), F32),
                   jax.ShapeDtypeStruct((M, Dc), BF16),
                   jax.ShapeDtypeStruct((B, nt * kpt, Dc + SUBLANES, KB), BF16),
                   jax.ShapeDtypeStruct((M, dI), BF16)],
        compiler_params=_params("parallel"),
        name="dsa_prep",
    )(zb, zb, zf, qn, kn, wq, wqi, wuk)


def _sortable_key(s):
    s = jnp.where(s == 0.0, 0.0, s)
    bits = pltpu.bitcast(s, I32)
    return bits ^ ((bits >> 31) & 0x7FFFFFFF)


def _two_per_trip(lo, hi, body, init):
    def pair(j, carry):
        return body(lo + 2 * j + 1, body(lo + 2 * j, carry))

    carry = lax.fori_loop(0, (hi - lo) // 2, pair, init)
    return lax.cond((hi - lo) % 2 == 1, lambda c: body(hi - 1, c), lambda c: c, carry)


def _dsa_attn_kernel(topk, n_keys, neg_key, qlt_ref, qit_ref, w_ref, ckv_ref, ckvt_ref, ik_ref, bias_ref,
                     wuvt_ref, o_ref, key_ref, z_ref, acc_ref, tie_ref):
    H, HI = DSA_HEADS, IDX_HEADS
    QB = o_ref.shape[0]
    KB = key_ref.shape[1]
    q0 = pl.program_id(1) * QB
    nkb = (q0 + QB - 1) // KB + 1
    n_skip = n_keys - nkb * KB

    s_row = lax.broadcasted_iota(I32, (KB, QB), 0)
    t_row = q0 + lax.broadcasted_iota(I32, (1, QB), 1)

    qit = qit_ref[0]
    w_row = w_ref[0]

    def score_blk(kb, carry):
        k0 = pl.multiple_of(kb * KB, KB)
        p = _dot(ik_ref[pl.ds(k0, KB), :], qit)
        p = jnp.maximum(p, 0.0) * w_row
        s = p[:, 0:QB]
        for h in range(1, HI):
            s = s + p[:, h * QB:(h + 1) * QB]
        s = jnp.where(k0 + s_row <= t_row, s, NEG_BIG)
        key_ref[kb] = _sortable_key(s)
        return carry

    _two_per_trip(0, nkb, score_blk, 0)

    def count(hit_fn):
        def body(kb, acc):
            return acc + _group_sum(jnp.where(hit_fn(kb, key_ref[kb]), 1, 0))

        acc = _two_per_trip(0, nkb, body, jnp.zeros((SUBLANES, QB), I32))
        return jnp.sum(acc, axis=0, keepdims=True)

    def count_ge(cand):
        return count(lambda kb, kk: kk >= cand) + jnp.where(cand <= neg_key, n_skip, 0)

    zero = jnp.zeros((1, QB), I32)
    cnt0 = count_ge(zero)
    state = (jnp.where(cnt0 >= topk, zero, jnp.full((1, QB), INT_MIN, I32)),
             jnp.where(cnt0 >= topk, cnt0, n_keys))

    def bisect(it, state):
        thr, cnt = state
        cand = thr | jnp.left_shift(jnp.int32(1), 30 - it)
        c = count_ge(cand)
        return jnp.where(c >= topk, cand, thr), jnp.where(c >= topk, c, cnt)

    thr, cnt = lax.fori_loop(0, 31, bisect, state)

    nbits = max(1, (n_keys - 1).bit_length())
    tie_ref[...] = jnp.full(tie_ref.shape, (1 << nbits) - 1, I32)

    @pl.when(jnp.max(jnp.where(cnt > topk, 1, 0)) > 0)
    def _():
        n_gt = count(lambda kb, kk: kk > thr) + jnp.where(thr < neg_key, n_skip, 0)
        need = topk - n_gt

        def ibisect(it, p):
            cand = p | jnp.left_shift(jnp.int32(1), nbits - 1 - it)
            below = count(lambda kb, kk: (kk == thr) & (kb * KB + s_row < cand))
            return jnp.where(below < need, cand, p)

        p = lax.fori_loop(0, nbits, ibisect, jnp.zeros((1, QB), I32))
        tie_ref[...] = jnp.broadcast_to(p, tie_ref.shape)

    tie = tie_ref[0:1, :]

    qlt = qlt_ref[0]

    FAR_TILE = bias_ref.shape[0] - 1
    n_far = jnp.maximum((q0 - FAR_TILE * QB) // KB + 1, 0)
    far_bias = bias_ref[FAR_TILE, 0:1, :]

    def logits_blk(near, kb, m8):
        k0 = pl.multiple_of(kb * KB, KB)
        kk = key_ref[kb]
        s_glob = k0 + s_row
        sel = (kk > thr) | ((kk == thr) & (s_glob <= tie))
        z = _dot(ckv_ref[pl.ds(k0, KB), :], qlt)
        if near:
            sel = sel & (s_glob <= t_row)
            z = z + bias_ref[(q0 - k0) // QB]
        tops = []
        for h in range(H):
            zh = jnp.where(sel, z[:, h * QB:(h + 1) * QB], NEG_BIG)
            z_ref[kb, :, h * QB:(h + 1) * QB] = zh
            tops.append(_group_max(zh))
        return jnp.maximum(m8, jnp.concatenate(tops, axis=1))

    neg8 = jnp.full((SUBLANES, H * QB), NEG_BIG, F32)
    m8_far = _two_per_trip(0, n_far, functools.partial(logits_blk, False), neg8)
    m8_near = _two_per_trip(n_far, nkb, functools.partial(logits_blk, True), neg8)
    m = jnp.maximum(jnp.max(m8_far, axis=0, keepdims=True) + far_bias,
                    jnp.max(m8_near, axis=0, keepdims=True))
    m_far = m - far_bias
    acc_ref[...] = jnp.zeros(acc_ref.shape, F32)

    def pv_blk(kb):
        p = jnp.exp((z_ref[kb] - jnp.where(kb < n_far, m_far, m)).astype(BF16))
        return _dot(ckvt_ref[0, kb], p)

    def pv_pair(j, carry):
        acc_ref[...] += pv_blk(2 * j) + pv_blk(2 * j + 1)
        return carry

    lax.fori_loop(0, nkb // 2, pv_pair, 0)

    @pl.when(nkb % 2 == 1)
    def _():
        acc_ref[...] += pv_blk(nkb - 1)

    Dc = acc_ref.shape[0] - SUBLANES
    inv_l = 1.0 / acc_ref[Dc:Dc + 1, :]
    o_lat_t = (acc_ref[0:Dc, :] * inv_l).astype(BF16)
    y_t = jnp.concatenate([_dot(wuvt_ref[h], o_lat_t[:, h * QB:(h + 1) * QB]) for h in range(H)], axis=0)
    o_ref[...] = y_t.T.astype(BF16)


def _dsa_attn(qlt, qit, w_row, ckvn, ckvt, idxk, bias_tiles, wuvt, B, T):
    H, HI, dI, Dc, dh = DSA_HEADS, IDX_HEADS, IDX_DIM, DSA_KV_RANK, DSA_DH
    QB = min(Q_BLOCK, T)
    KB = KEY_BLOCK
    nq = T // QB
    nk = T // KB
    topk = min(DSA_TOPK_MAX, T // 4)
    neg_key = int(np.array(NEG_BIG, np.float32).view(np.int32))
    neg_key = neg_key ^ ((neg_key >> 31) & 0x7FFFFFFF)
    return pl.pallas_call(
        functools.partial(_dsa_attn_kernel, topk, T, neg_key),
        grid=(B, nq),
        in_specs=[pl.BlockSpec((1, Dc, H * QB), lambda b, i: (b * nq + i, 0, 0)),
                  pl.BlockSpec((1, dI, HI * QB), lambda b, i: (b * nq + i, 0, 0)),
                  pl.BlockSpec((1, 1, HI * QB), lambda b, i: (b * nq + i, 0, 0)),
                  pl.BlockSpec((T, Dc), lambda b, i: (b, 0)),
                  pl.BlockSpec((1, nk, Dc + SUBLANES, KB), lambda b, i: (b, 0, 0, 0)),
                  pl.BlockSpec((T, dI), lambda b, i: (b, 0)),
                  pl.BlockSpec((4, KB, H * QB), lambda b, i: (0, 0, 0), pipeline_mode=pl.Buffered(1)),
                  pl.BlockSpec((H, dh, Dc), lambda b, i: (0, 0, 0))],
        out_specs=pl.BlockSpec((QB, DSA_W), lambda b, i: (b * nq + i, 0)),
        out_shape=jax.ShapeDtypeStruct((B * T, DSA_W), BF16),
        scratch_shapes=[pltpu.VMEM((nk, KB, QB), I32),
                        pltpu.VMEM((nk, KB, H * QB), F32),
                        pltpu.VMEM((Dc + SUBLANES, H * QB), F32),
                        pltpu.VMEM((SUBLANES, QB), I32)],
        compiler_params=_params("parallel", "arbitrary"),
        name="dsa_attn",
    )(qlt, qit, w_row, ckvn, ckvt, idxk, bias_tiles, wuvt)


def _rel_bias_tiles(rel_bias, QB, KB):
    max_exact = REL_BUCKETS // 2
    n_far = 3 * QB
    assert n_far - (KB - 1) > REL_MAX_DIST
    n = jnp.arange(n_far, dtype=jnp.int32)
    nf = jnp.maximum(n, max_exact).astype(F32)
    large = max_exact + (jnp.log(nf / max_exact) / math.log(REL_MAX_DIST / max_exact)
                         * (REL_BUCKETS - max_exact)).astype(jnp.int32)
    large = jnp.minimum(large, REL_BUCKETS - 1)
    bucket = jnp.where(n < max_exact, n, large)
    H = rel_bias.shape[1]
    tab = jnp.take(rel_bias, bucket, axis=0).T
    P = KB + QB
    tiles = []
    for delta in (0, QB, 2 * QB):
        d = np.concatenate([np.arange(QB + 1), np.arange(-(KB - 1), 0)])
        g = tab[:, np.clip(delta + d, 0, n_far - 1)]
        rows = jnp.tile(g, (1, KB))[:, :KB * (P - 1)].reshape(H, KB, P - 1)[:, :, :QB]
        tiles.append(rows.transpose(1, 0, 2).reshape(KB, H * QB))
    tiles.append(jnp.broadcast_to(jnp.repeat(rel_bias[REL_BUCKETS - 1], QB)[None, :], (KB, H * QB)))
    return jnp.stack(tiles).astype(F32)


def _merge_kernel(x_ref, yr_ref, yd_ref, yh_ref, lnpre_ref, wg_ref, wr_ref, wd_ref, wh_ref, wo_ref,
                  ln_ref, o_ref):
    D = x_ref.shape[1]
    x = x_ref[...]
    h = _rms(x, lnpre_ref[...]).astype(BF16)
    m = None
    for k, (y_ref, w_ref) in enumerate(((yr_ref, wr_ref), (yd_ref, wd_ref), (yh_ref, wh_ref))):
        gate = _sigmoid(_dot(h, wg_ref[:, k * D:(k + 1) * D]))
        term = gate * _dot(y_ref[...], w_ref[...])
        m = term if m is None else m + term
    u = _dot(m.astype(BF16), wo_ref[...])
    o_ref[...] = x + _rms(u, ln_ref[...])


def _merge(x2, y_ret, y_dsa, y_hg, lnpre, wg, wr, wd, wh, wo, ln):
    M = x2.shape[0]
    tm = min(PROJ_ROWS, M)
    D = D_MODEL
    row = lambda w: pl.BlockSpec((tm, w), lambda i: (i, 0))
    const = lambda r, c: pl.BlockSpec((r, c), lambda i: (0, 0))
    return pl.pallas_call(
        _merge_kernel,
        grid=(M // tm,),
        in_specs=[row(D), row(RET_W), row(DSA_W), row(HGRN_W), const(1, D), const(D, N_BRANCH * D),
                  const(RET_W, D), const(DSA_W, D), const(HGRN_W, D), const(D, D), const(1, D)],
        out_specs=row(D),
        out_shape=jax.ShapeDtypeStruct((M, D), F32),
        compiler_params=_params("parallel"),
        name="merge",
    )(x2, y_ret, y_dsa, y_hg, lnpre, wg, wr, wd, wh, wo, ln)


def _gelu_tanh(x):
    return 0.5 * x * (1.0 + jnp.tanh(math.sqrt(2.0 / math.pi) * (x + 0.044715 * (x * x * x))))


def _ffn_kernel(tiles_per_seq, x_ref, lnpre_ref, wup_ref, cw_ref, cb_ref, wdn_ref, lnpost_ref,
                o_ref, buf_ref, prev_ref):
    tm = x_ref.shape[0]
    HALO = SUBLANES
    fc = buf_ref.shape[1]
    n_pass = D_FF // fc
    first = (pl.program_id(0) % tiles_per_seq) == 0

    @pl.when(first)
    def _():
        prev_ref[...] = jnp.zeros_like(prev_ref)

    x = x_ref[...]
    h = _rms(x, lnpre_ref[...]).astype(BF16)

    def conv(part, c):
        col = part * D_FF + c * fc
        up = _dot(h, wup_ref[:, col:col + fc])
        slot = part * n_pass + c
        buf_ref[0:HALO, :] = prev_ref[slot]
        buf_ref[HALO:HALO + tm, :] = up
        prev_ref[slot] = up[tm - HALO:tm, :]
        w = cw_ref[:, col:col + fc]
        y = (up * w[2:3] + buf_ref[HALO - 1:HALO - 1 + tm, :] * w[1:2]
             + buf_ref[HALO - 2:HALO - 2 + tm, :] * w[0:1])
        return y + cb_ref[:, col:col + fc]

    acc = None
    for c in range(n_pass):
        a = conv(0, c)
        u = conv(1, c)
        act = (_gelu_tanh(a) * u).astype(BF16)
        d = _dot(act, wdn_ref[c * fc:(c + 1) * fc, :])
        acc = d if acc is None else acc + d
    o_ref[...] = x + _rms(acc, lnpost_ref[...])


def _ffn(x2, lnpre, wup, cw, cb, wdn, lnpost, T):
    M = x2.shape[0]
    D = D_MODEL
    tm = min(FFN_ROWS, T)
    fc = D_FF
    const = lambda r, c: pl.BlockSpec((r, c), lambda i: (0, 0))
    return pl.pallas_call(
        functools.partial(_ffn_kernel, T // tm),
        grid=(M // tm,),
        in_specs=[pl.BlockSpec((tm, D), lambda i: (i, 0)), const(1, D), const(D, 2 * D_FF),
                  const(CONV_WIDTH, 2 * D_FF), const(1, 2 * D_FF), const(D_FF, D), const(1, D)],
        out_specs=pl.BlockSpec((tm, D), lambda i: (i, 0)),
        out_shape=jax.ShapeDtypeStruct((M, D), F32),
        scratch_shapes=[pltpu.VMEM((tm + SUBLANES, fc), F32),
                        pltpu.VMEM((2 * (D_FF // fc), SUBLANES, fc), F32)],
        compiler_params=_params("arbitrary"),
        name="conv_ffn",
    )(x2, lnpre, wup, cw, cb, wdn, lnpost)


def _split_w_in(w):
    widths = (256, 256, 512, 512, 256, 128, 64, 8, 512, 512, 512, 512, N_BRANCH * D_MODEL)
    offs = np.concatenate([[0], np.cumsum(widths)])
    rq, rk, rv, rg, cq, ckv, ik, iw, hq, hf, hv, hg, gt = [w[..., offs[k]:offs[k + 1]] for k in range(len(widths))]
    zeros = lambda n: jnp.zeros(w.shape[:-1] + (n,), w.dtype)
    wb = jnp.concatenate([rq, rk, rv, cq, ckv, ik, zeros(2 * DSA_KV_RANK - DSA_KV_RANK - IDX_DIM), hq, hv], axis=-1)
    wf = jnp.concatenate([rg, hf, hg, iw, zeros(LANES - IDX_HEADS)], axis=-1)
    assert wb.shape[-1] == ZB_W and wf.shape[-1] == ZF_W
    return wb.astype(BF16), wf.astype(BF16), gt.astype(BF16)


def kernel(x, rel_bias, hgrn_lb, ln_mix_pre, ln_mix_post, ln_ffn_pre, ln_ffn_post, w_in, dsa_q_norm, dsa_kv_norm, dsa_w_uq, dsa_w_uk, dsa_w_uv, hgrn_norm, w_br_ret, w_br_dsa, w_br_hgrn, w_out, ffn_w_up, ffn_conv_w, ffn_conv_b, ffn_w_down):
    B, T, D = x.shape
    depth = w_in.shape[0]
    assert T % KEY_BLOCK == 0 and D == D_MODEL
    H, dh, Rq = DSA_HEADS, DSA_DH, DSA_Q_RANK
    bias_tiles = _rel_bias_tiles(rel_bias, min(Q_BLOCK, T), KEY_BLOCK)
    wq = dsa_w_uq[..., :H * dh].astype(BF16)
    wqi = dsa_w_uq[..., H * dh:].astype(BF16)
    wuk = dsa_w_uk.transpose(0, 1, 3, 2).astype(BF16)
    wuvt = dsa_w_uv.transpose(0, 1, 3, 2).astype(BF16)
    x2 = x.reshape(B * T, D)
    row = lambda v: v.reshape(1, -1)
    for l in range(depth):
        wb, wf, wg = _split_w_in(w_in[l])
        zb, zf = _inproj(x2, row(ln_mix_pre[l]), wb, wf)
        y_ret = _retention(zb, zf, B, T)
        y_hg = _hgrn(zb, zf, hgrn_lb, row(hgrn_norm[l]), l, B, T)
        qlt, qit, w_row, ckvn, ckvt, idxk = _dsa_prep(zb, zf, row(dsa_q_norm[l]), row(dsa_kv_norm[l]),
                                                      wq[l], wqi[l], wuk[l], B, T)
        y_dsa = _dsa_attn(qlt, qit, w_row, ckvn, ckvt, idxk, bias_tiles, wuvt[l], B, T)
        x2 = _merge(x2, y_ret, y_dsa, y_hg, row(ln_mix_pre[l]), wg, w_br_ret[l].astype(BF16),
                    w_br_dsa[l].astype(BF16), w_br_hgrn[l].astype(BF16), w_out[l].astype(BF16),
                    row(ln_mix_post[l]))
        x2 = _ffn(x2, row(ln_ffn_pre[l]), ffn_w_up[l].astype(BF16), ffn_conv_w[l], row(ffn_conv_b[l]),
                  ffn_w_down[l].astype(BF16), row(ln_ffn_post[l]), T)
    return x2.reshape(B, T, D)
```

```python
import functools
import math

import jax
import jax.numpy as jnp
import numpy as np
from jax import lax
from jax.experimental import pallas as pl
from jax.experimental.pallas import tpu as pltpu

F32 = jnp.float32
BF16 = jnp.bfloat16
I32 = jnp.int32

D_MODEL = 1024
RET_HEADS, RET_DK, RET_DV, RET_CHUNK = 4, 64, 128, 128
DSA_HEADS, DSA_DH, DSA_Q_RANK, DSA_KV_RANK = 8, 64, 256, 128
IDX_HEADS, IDX_DIM, DSA_TOPK_MAX = 8, 64, 256
Q_BLOCK = 256
HGRN_HEADS, HGRN_EXPAND, HGRN_DV = 4, 128, 128
F_FLOOR = 1e-6
REL_BUCKETS, REL_MAX_DIST = 32, 128
D_FF = 2816
CONV_WIDTH = 3
N_BRANCH = 3
EPS = 1e-6
NEG_BIG = -1e30

RET_W = RET_HEADS * RET_DV
DSA_W = DSA_HEADS * DSA_DH
HGRN_KW = HGRN_HEADS * HGRN_EXPAND
HGRN_W = HGRN_HEADS * HGRN_DV

ZB_RQ, ZB_RK, ZB_RV, ZB_CQ = 0, 256, 512, 1024
ZB_CKV = 1280
ZB_HQ, ZB_HV = 1536, 2048
ZB_W = 2560
ZF_RG, ZF_HF, ZF_HG, ZF_IW = 0, 512, 1024, 1536
ZF_W = 1664

VMEM_LIMIT_BYTES = 56 * 1024 * 1024
SUBLANES = 8
LANES = 128

KEY_BLOCK = 256
DSA_PREP_ROWS = 2048
PROJ_ROWS = 1024
FFN_ROWS = 512
HGRN_ROWS = 512
RET_ROWS = 256
HGRN_PAIR_CHUNK = 256
INT_MIN = -(2 ** 31)


def _params(*sem):
    return pltpu.CompilerParams(dimension_semantics=sem, vmem_limit_bytes=VMEM_LIMIT_BYTES)


def _dot(a, b):
    return jnp.dot(a, b, preferred_element_type=F32)


def _dot_nt(a, b):
    return lax.dot_general(a, b, (((1,), (1,)), ((), ())), preferred_element_type=F32)


def _dot_tn(a, b):
    return lax.dot_general(a, b, (((0,), (0,)), ((), ())), preferred_element_type=F32)


def _sigmoid(x):
    return 0.5 * jnp.tanh(0.5 * x) + 0.5


def _sigmoid_pair(x):
    t = jnp.exp(-jnp.abs(x))
    big = 1.0 / (1.0 + t)
    small = t * big
    pos = x >= 0.0
    return jnp.where(pos, big, small), jnp.where(pos, small, big)


def _rms(x, g):
    return x * lax.rsqrt(jnp.mean(x * x, axis=-1, keepdims=True) + EPS) * g


def _group_sum(x):
    return jnp.sum(x.reshape(x.shape[0] // SUBLANES, SUBLANES, x.shape[1]), axis=0)


def _group_max(x):
    return jnp.max(x.reshape(x.shape[0] // SUBLANES, SUBLANES, x.shape[1]), axis=0)


def _inproj_kernel(x_ref, g_ref, wb_ref, wf_ref, zb_ref, zf_ref):
    h = _rms(x_ref[...], g_ref[...]).astype(BF16)
    zb_ref[...] = _dot(h, wb_ref[...]).astype(BF16)
    zf_ref[...] = _dot(h, wf_ref[...])


def _inproj(x2, g, wb, wf):
    M = x2.shape[0]
    tm = min(PROJ_ROWS, M)
    const = lambda r, c: pl.BlockSpec((r, c), lambda i: (0, 0))
    return pl.pallas_call(
        _inproj_kernel,
        grid=(M // tm,),
        in_specs=[pl.BlockSpec((tm, D_MODEL), lambda i: (i, 0)), const(1, D_MODEL),
                  const(D_MODEL, ZB_W), const(D_MODEL, ZF_W)],
        out_specs=[pl.BlockSpec((tm, ZB_W), lambda i: (i, 0)), pl.BlockSpec((tm, ZF_W), lambda i: (i, 0))],
        out_shape=[jax.ShapeDtypeStruct((M, ZB_W), BF16), jax.ShapeDtypeStruct((M, ZF_W), F32)],
        compiler_params=_params("parallel"),
        name="inproj",
    )(x2, g, wb, wf)


def _ret_kernel(gam_ref, q_ref, k_ref, v_ref, g_ref, cos_ref, sin_ref, dm_ref, xi_ref, zeta_ref,
                o_ref, r_ref):
    rows = q_ref.shape[0]
    C = dm_ref.shape[1]
    H, dk, dv = RET_HEADS, RET_DK, RET_DV

    @pl.when(pl.program_id(1) == 0)
    def _():
        r_ref[...] = jnp.zeros_like(r_ref)

    cos = cos_ref[...]
    sin = sin_ref[...]
    lane = lax.broadcasted_iota(I32, (rows, H * dk), 1)
    first_half = (lane % dk) < (dk // 2)

    def rot(x):
        swapped = jnp.where(first_half, pltpu.roll(x, H * dk - dk // 2, 1), pltpu.roll(x, dk // 2, 1))
        return x * cos + swapped * sin

    q = rot(q_ref[...].astype(F32))
    k = rot(k_ref[...].astype(F32)) * dk ** -0.5
    for c in range(rows // C):
        rs = slice(c * C, (c + 1) * C)
        for h in range(H):
            qh = q[rs, h * dk:(h + 1) * dk].astype(BF16)
            kh = k[rs, h * dk:(h + 1) * dk]
            vh = v_ref[rs, h * dv:(h + 1) * dv]
            s = _dot_nt(qh, kh.astype(BF16)) * dm_ref[h]
            inner = _dot(s.astype(BF16), vh)
            rh = r_ref[h]
            cross = _dot(qh, rh.astype(BF16)) * xi_ref[h]
            r_ref[h] = gam_ref[h] * rh + _dot_tn((kh * zeta_ref[h]).astype(BF16), vh)
            o = inner + cross
            mu = jnp.mean(o, axis=-1, keepdims=True)
            oc = o - mu
            var = jnp.mean(oc * oc, axis=-1, keepdims=True)
            gh = g_ref[rs, h * dv:(h + 1) * dv]
            o_ref[rs, h * dv:(h + 1) * dv] = (gh * _sigmoid(gh) * (oc * lax.rsqrt(var + EPS))).astype(BF16)


def _retention(zb, zf, B, T):
    H, dk, dv = RET_HEADS, RET_DK, RET_DV
    C = min(RET_CHUNK, T)
    n = T // C
    pos = np.arange(T, dtype=np.float64)
    half = dk // 2
    freq = 1.0 / (10000.0 ** np.linspace(0.0, 1.0, half))
    ang = pos[:, None] * freq[None, :]
    cos = jnp.asarray(np.tile(np.cos(ang), (1, 2 * H)), F32)
    sin = jnp.asarray(np.tile(np.concatenate([-np.sin(ang), np.sin(ang)], axis=1), (1, H)), F32)
    log_gamma = np.log1p(-(2.0 ** (-5.0 - np.arange(H, dtype=np.float64))))
    i = np.arange(C, dtype=np.float64)
    rel = i[:, None] - i[None, :]
    dmask = jnp.asarray(np.where(rel >= 0, np.exp(np.maximum(rel, 0.0)[None] * log_gamma[:, None, None]), 0.0), F32)
    xi = jnp.asarray(np.exp((i + 1.0)[None, :] * log_gamma[:, None])[:, :, None], F32)
    zeta = jnp.asarray(np.exp((C - 1.0 - i)[None, :] * log_gamma[:, None])[:, :, None], F32)
    gamma_c = jnp.asarray(np.exp(C * log_gamma), F32)

    wq = H * dk
    wv = H * dv
    R = min(RET_ROWS, T)
    n = T // R
    full = lambda shape: pl.BlockSpec(shape, lambda b, c: (0,) * len(shape))
    return pl.pallas_call(
        _ret_kernel,
        grid=(B, n),
        in_specs=[pl.BlockSpec(memory_space=pltpu.SMEM),
                  pl.BlockSpec((R, wq), lambda b, c: (b * n + c, ZB_RQ // wq)),
                  pl.BlockSpec((R, wq), lambda b, c: (b * n + c, ZB_RK // wq)),
                  pl.BlockSpec((R, wv), lambda b, c: (b * n + c, ZB_RV // wv)),
                  pl.BlockSpec((R, wv), lambda b, c: (b * n + c, ZF_RG // wv)),
                  pl.BlockSpec((R, wq), lambda b, c: (c, 0)),
                  pl.BlockSpec((R, wq), lambda b, c: (c, 0)),
                  full((H, C, C)), full((H, C, 1)), full((H, C, 1))],
        out_specs=pl.BlockSpec((R, wv), lambda b, c: (b * n + c, 0)),
        out_shape=jax.ShapeDtypeStruct((B * T, wv), BF16),
        scratch_shapes=[pltpu.VMEM((H, dk, dv), F32)],
        compiler_params=_params("parallel", "arbitrary"),
        name="retention",
    )(gamma_c, zb, zb, zb, zf, cos, sin, dmask, xi, zeta)


def _hgrn_levels(C):
    ms, m = [], C // 2
    while m >= SUBLANES:
        ms.append(m)
        m //= 2
    return ms


def _cumsum_rows(tri, x):
    hi = x.astype(BF16)
    rest = x - hi.astype(F32)
    mid = rest.astype(BF16)
    lo = (rest - mid.astype(F32)).astype(BF16)
    return _dot(tri, hi) + (_dot(tri, mid) + _dot(tri, lo))


def _hgrn_kernel(layer, q_ref, f_ref, v_ref, g_ref, lbraw_ref, gain_ref, tri_ref, lmask_ref, o_ref, st_ref):
    TR = q_ref.shape[0]
    C = tri_ref.shape[0]
    H, dk, dv = HGRN_HEADS, HGRN_EXPAND, HGRN_DV
    SB = SUBLANES

    @pl.when(pl.program_id(1) == 0)
    def _():
        st_ref[...] = jnp.zeros_like(st_ref)

    raw = lbraw_ref[...]
    e = jnp.exp(raw - jnp.max(raw, axis=0, keepdims=True))
    soft = e / jnp.sum(e, axis=0, keepdims=True)
    cs = soft[0:1]
    for l in range(1, layer + 1):
        cs = cs + soft[l:l + 1]
    lb = jnp.clip(cs - soft[0:1], 0.0, 1.0)
    tri = tri_ref[...]
    row_in_blk = lax.broadcasted_iota(I32, (SB, dk), 0)
    lane_t = lax.broadcasted_iota(I32, (SB, C), 1)
    levels = _hgrn_levels(C)

    def chunk(c, carry):
        r0 = c * C
        sig_pos, sig_neg = _sigmoid_pair(f_ref[pl.ds(r0, C), :])
        f = lb + (1.0 - lb) * sig_pos
        log_f = jnp.log(jnp.maximum(f, F_FLOOR))
        kk = (1.0 - lb) * sig_neg
        b = _cumsum_rows(tri, log_f)
        qq = q_ref[pl.ds(r0, C), :].astype(F32)
        vb = v_ref[pl.ds(r0, C), :]
        eb = jnp.exp(b)
        b_last = b[C - 1:C, :]
        eb_last = eb[C - 1:C, :]
        q_dec = (qq * eb).astype(BF16)
        k_dec = (kk * jnp.exp(b_last - b)).astype(BF16)
        outs = []
        for h in range(H):
            sl = slice(h * dk, (h + 1) * dk)
            bh, qh, kh, vh = b[:, sl], qq[:, sl], kk[:, sl], vb[:, h * dv:(h + 1) * dv]
            diag = []
            for blk in range(C // SB):
                bs = bh[blk * SB:(blk + 1) * SB, :]
                ks = kh[blk * SB:(blk + 1) * SB, :]
                at = jnp.zeros((SB, C), F32)
                for tt in range(SB):
                    t = blk * SB + tt
                    diff = jnp.where(row_in_blk <= tt, bh[t:t + 1, :] - bs, NEG_BIG)
                    p = (qh[t:t + 1, :] * jnp.exp(diff)) * ks
                    at = jnp.where(lane_t == t, jnp.sum(p, axis=-1, keepdims=True), at)
                diag.append(at)
            a_t = jnp.concatenate(diag, axis=0)
            for lev, m in enumerate(levels):
                qs, ks = [], []
                for blk in range(C // m):
                    rows = slice(blk * m, (blk + 1) * m)
                    if blk % 2 == 1:
                        ref = bh[blk * m - 1:blk * m, :]
                        qs.append(qh[rows] * jnp.exp(bh[rows] - ref))
                        ks.append(jnp.zeros((m, dk), F32))
                    else:
                        ref = bh[(blk + 1) * m - 1:(blk + 1) * m, :]
                        ks.append(kh[rows] * jnp.exp(ref - bh[rows]))
                        qs.append(jnp.zeros((m, dk), F32))
                q_l = jnp.concatenate(qs, axis=0).astype(BF16)
                k_l = jnp.concatenate(ks, axis=0).astype(BF16)
                a_t = a_t + _dot_nt(k_l, q_l) * lmask_ref[lev]
            intra = _dot_tn(a_t.astype(BF16), vh)
            st = st_ref[h]
            outs.append(intra + _dot_nt(q_dec[:, sl], st.astype(BF16)))
            st_ref[h] = st * eb_last[:, sl] + _dot_tn(vh, k_dec[:, sl])
        o = _rms(jnp.concatenate(outs, axis=1), gain_ref[...])
        gg = g_ref[pl.ds(r0, C), :]
        o_ref[pl.ds(r0, C), :] = (gg * _sigmoid(gg) * o).astype(BF16)
        return carry

    for c in range(TR // C):
        chunk(c, 0)


def _hgrn(zb, zf, hgrn_lb, gain, layer, B, T):
    H, dk, dv = HGRN_HEADS, HGRN_EXPAND, HGRN_DV
    C = min(HGRN_PAIR_CHUNK, T)
    TR = min(HGRN_ROWS, T)
    n = T // TR
    w = H * dk
    L = hgrn_lb.shape[0]
    tri = jnp.tril(jnp.ones((C, C), BF16))
    idx = np.arange(C)
    lmask = np.stack([(((idx[None, :] // m) % 2 == 1) & (idx[:, None] // m == idx[None, :] // m - 1))
                      for m in _hgrn_levels(C)]).astype(np.float32)
    return pl.pallas_call(
        functools.partial(_hgrn_kernel, layer),
        grid=(B, n),
        in_specs=[pl.BlockSpec((TR, w), lambda b, c: (b * n + c, ZB_HQ // w)),
                  pl.BlockSpec((TR, w), lambda b, c: (b * n + c, ZF_HF // w)),
                  pl.BlockSpec((TR, w), lambda b, c: (b * n + c, ZB_HV // w)),
                  pl.BlockSpec((TR, w), lambda b, c: (b * n + c, ZF_HG // w)),
                  pl.BlockSpec((L, w), lambda b, c: (0, 0)),
                  pl.BlockSpec((1, w), lambda b, c: (0, 0)),
                  pl.BlockSpec((C, C), lambda b, c: (0, 0)),
                  pl.BlockSpec(lmask.shape, lambda b, c: (0, 0, 0))],
        out_specs=pl.BlockSpec((TR, w), lambda b, c: (b * n + c, 0)),
        out_shape=jax.ShapeDtypeStruct((B * T, w), BF16),
        scratch_shapes=[pltpu.VMEM((H, dv, dk), F32)],
        compiler_params=_params("parallel", "arbitrary"),
        name="hgrn2",
    )(zb, zf, zb, zf, hgrn_lb, gain, tri, jnp.asarray(lmask))


def _dsa_prep_kernel(cq_ref, kv_ref, iw_ref, qn_ref, kn_ref, wq_ref, wqi_ref, wukt_ref,
                     qlt_ref, qit_ref, w_ref, ckv_ref, ckvt_ref, ik_ref):
    tm = cq_ref.shape[0]
    H, dh, HI, dI, Dc = DSA_HEADS, DSA_DH, IDX_HEADS, IDX_DIM, DSA_KV_RANK
    QB = qlt_ref.shape[2] // H
    nb = tm // QB
    cq = _rms(cq_ref[...].astype(F32), qn_ref[...]).astype(BF16)
    ckv = _rms(kv_ref[:, :Dc].astype(F32), kn_ref[...])
    ckv_ref[...] = ckv.astype(BF16)
    KB = ckvt_ref.shape[3]
    for j in range(tm // KB):
        ckvt_ref[0, j, 0:Dc, :] = ckv[j * KB:(j + 1) * KB].T.astype(BF16)
        ckvt_ref[0, j, Dc:Dc + SUBLANES, :] = jnp.ones((SUBLANES, KB), BF16)
    ik_ref[...] = kv_ref[:, Dc:Dc + dI]
    w_t = iw_ref[...].T[0:HI, :] * (HI * dI) ** -0.5
    qi_all = _dot(cq, wqi_ref[...])
    q_all = _dot(cq, wq_ref[...])
    for j in range(nb):
        rows = slice(j * QB, (j + 1) * QB)
        qi_t = qi_all[rows].T
        q_t = q_all[rows].T.astype(BF16)
        for h in range(HI):
            qit_ref[j, :, h * QB:(h + 1) * QB] = qi_t[h * dI:(h + 1) * dI].astype(BF16)
            w_ref[j, :, h * QB:(h + 1) * QB] = w_t[h:h + 1, rows]
        for h in range(H):
            ql_t = _dot(wukt_ref[h], q_t[h * dh:(h + 1) * dh]) * dh ** -0.5
            qlt_ref[j, :, h * QB:(h + 1) * QB] = ql_t.astype(BF16)


def _dsa_prep(zb, zf, qn, kn, wq, wqi, wuk, B, T):
    M = B * T
    H, dh, HI, dI, Dc, Rq = DSA_HEADS, DSA_DH, IDX_HEADS, IDX_DIM, DSA_KV_RANK, DSA_Q_RANK
    QB = min(Q_BLOCK, T)
    KB = KEY_BLOCK
    tm = min(DSA_PREP_ROWS, T)
    nb = tm // QB
    nt = T // tm
    kpt = tm // KB
    return pl.pallas_call(
        _dsa_prep_kernel,
        grid=(M // tm,),
        in_specs=[pl.BlockSpec((tm, Rq), lambda i: (i, ZB_CQ // Rq)),
                  pl.BlockSpec((tm, 2 * Dc), lambda i: (i, ZB_CKV // (2 * Dc))),
                  pl.BlockSpec((tm, LANES), lambda i: (i, ZF_IW // LANES)),
                  pl.BlockSpec((1, Rq), lambda i: (0, 0)),
                  pl.BlockSpec((1, Dc), lambda i: (0, 0)),
                  pl.BlockSpec((Rq, H * dh), lambda i: (0, 0)),
                  pl.BlockSpec((Rq, HI * dI), lambda i: (0, 0)),
                  pl.BlockSpec((H, Dc, dh), lambda i: (0, 0, 0))],
        out_specs=[pl.BlockSpec((nb, Dc, H * QB), lambda i: (i, 0, 0)),
                   pl.BlockSpec((nb, dI, HI * QB), lambda i: (i, 0, 0)),
                   pl.BlockSpec((nb, 1, HI * QB), lambda i: (i, 0, 0)),
                   pl.BlockSpec((tm, Dc), lambda i: (i, 0)),
                   pl.BlockSpec((1, kpt, Dc + SUBLANES, KB), lambda i: (i // nt, i % nt, 0, 0)),
                   pl.BlockSpec((tm, dI), lambda i: (i, 0))],
        out_shape=[jax.ShapeDtypeStruct((M // QB, Dc, H * QB), BF16),
                   jax.ShapeDtypeStruct((M // QB, dI, HI * QB), BF16),
                   jax.ShapeDtypeStruct((M // QB, 1, HI * QB), F32),
                   jax.ShapeDtypeStruct((M, Dc), BF16),
                   jax.ShapeDtypeStruct((B, nt * kpt, Dc + SUBLANES, KB), BF16),
                   jax.ShapeDtypeStruct((M, dI), BF16)],
        compiler_params=_params("parallel"),
        name="dsa_prep",
    )(zb, zb, zf, qn, kn, wq, wqi, wuk)


def _sortable_key(s):
    s = jnp.where(s == 0.0, 0.0, s)
    bits = pltpu.bitcast(s, I32)
    return bits ^ ((bits >> 31) & 0x7FFFFFFF)


def _two_per_trip(lo, hi, body, init):
    def pair(j, carry):
        return body(lo + 2 * j + 1, body(lo + 2 * j, carry))

    carry = lax.fori_loop(0, (hi - lo) // 2, pair, init)
    return lax.cond((hi - lo) % 2 == 1, lambda c: body(hi - 1, c), lambda c: c, carry)


def _dsa_attn_kernel(topk, n_keys, neg_key, qlt_ref, qit_ref, w_ref, ckv_ref, ckvt_ref, ik_ref, bias_ref,
                     wuvt_ref, o_ref, key_ref, z_ref, acc_ref, tie_ref):
    H, HI = DSA_HEADS, IDX_HEADS
    QB = o_ref.shape[0]
    KB = key_ref.shape[1]
    q0 = pl.program_id(1) * QB
    nkb = (q0 + QB - 1) // KB + 1
    n_skip = n_keys - nkb * KB

    s_row = lax.broadcasted_iota(I32, (KB, QB), 0)
    t_row = q0 + lax.broadcasted_iota(I32, (1, QB), 1)

    qit = qit_ref[0]
    w_row = w_ref[0]

    def score_blk(kb, carry):
        k0 = pl.multiple_of(kb * KB, KB)
        p = _dot(ik_ref[pl.ds(k0, KB), :], qit)
        p = jnp.maximum(p, 0.0) * w_row
        s = p[:, 0:QB]
        for h in range(1, HI):
            s = s + p[:, h * QB:(h + 1) * QB]
        s = jnp.where(k0 + s_row <= t_row, s, NEG_BIG)
        key_ref[kb] = _sortable_key(s)
        return carry

    _two_per_trip(0, nkb, score_blk, 0)

    def count(hit_fn):
        def body(kb, acc):
            return acc + _group_sum(jnp.where(hit_fn(kb, key_ref[kb]), 1, 0))

        acc = _two_per_trip(0, nkb, body, jnp.zeros((SUBLANES, QB), I32))
        return jnp.sum(acc, axis=0, keepdims=True)

    def count_ge(cand):
        return count(lambda kb, kk: kk >= cand) + jnp.where(cand <= neg_key, n_skip, 0)

    zero = jnp.zeros((1, QB), I32)
    cnt0 = count_ge(zero)
    state = (jnp.where(cnt0 >= topk, zero, jnp.full((1, QB), INT_MIN, I32)),
             jnp.where(cnt0 >= topk, cnt0, n_keys))

    def bisect(it, state):
        thr, cnt = state
        cand = thr | jnp.left_shift(jnp.int32(1), 30 - it)
        c = count_ge(cand)
        return jnp.where(c >= topk, cand, thr), jnp.where(c >= topk, c, cnt)

    thr, cnt = lax.fori_loop(0, 31, bisect, state)

    nbits = max(1, (n_keys - 1).bit_length())
    tie_ref[...] = jnp.full(tie_ref.shape, (1 << nbits) - 1, I32)

    @pl.when(jnp.max(jnp.where(cnt > topk, 1, 0)) > 0)
    def _():
        n_gt = count(lambda kb, kk: kk > thr) + jnp.where(thr < neg_key, n_skip, 0)
        need = topk - n_gt

        def ibisect(it, p):
            cand = p | jnp.left_shift(jnp.int32(1), nbits - 1 - it)
            below = count(lambda kb, kk: (kk == thr) & (kb * KB + s_row < cand))
            return jnp.where(below < need, cand, p)

        p = lax.fori_loop(0, nbits, ibisect, jnp.zeros((1, QB), I32))
        tie_ref[...] = jnp.broadcast_to(p, tie_ref.shape)

    tie = tie_ref[0:1, :]

    qlt = qlt_ref[0]

    FAR_TILE = bias_ref.shape[0] - 1
    n_far = jnp.maximum((q0 - FAR_TILE * QB) // KB + 1, 0)
    far_bias = bias_ref[FAR_TILE, 0:1, :]

    def logits_blk(near, kb, m8):
        k0 = pl.multiple_of(kb * KB, KB)
        kk = key_ref[kb]
        s_glob = k0 + s_row
        sel = (kk > thr) | ((kk == thr) & (s_glob <= tie))
        z = _dot(ckv_ref[pl.ds(k0, KB), :], qlt)
        if near:
            sel = sel & (s_glob <= t_row)
            z = z + bias_ref[(q0 - k0) // QB]
        tops = []
        for h in range(H):
            zh = jnp.where(sel, z[:, h * QB:(h + 1) * QB], NEG_BIG)
            z_ref[kb, :, h * QB:(h + 1) * QB] = zh
            tops.append(_group_max(zh))
        return jnp.maximum(m8, jnp.concatenate(tops, axis=1))

    neg8 = jnp.full((SUBLANES, H * QB), NEG_BIG, F32)
    m8_far = _two_per_trip(0, n_far, functools.partial(logits_blk, False), neg8)
    m8_near = _two_per_trip(n_far, nkb, functools.partial(logits_blk, True), neg8)
    m = jnp.maximum(jnp.max(m8_far, axis=0, keepdims=True) + far_bias,
                    jnp.max(m8_near, axis=0, keepdims=True))
    m_far = m - far_bias
    acc_ref[...] = jnp.zeros(acc_ref.shape, F32)

    def pv_blk(kb):
        p = jnp.exp((z_ref[kb] - jnp.where(kb < n_far, m_far, m)).astype(BF16))
        return _dot(ckvt_ref[0, kb], p)

    def pv_pair(j, carry):
        acc_ref[...] += pv_blk(2 * j) + pv_blk(2 * j + 1)
        return carry

    lax.fori_loop(0, nkb // 2, pv_pair, 0)

    @pl.when(nkb % 2 == 1)
    def _():
        acc_ref[...] += pv_blk(nkb - 1)

    Dc = acc_ref.shape[0] - SUBLANES
    inv_l = 1.0 / acc_ref[Dc:Dc + 1, :]
    o_lat_t = (acc_ref[0:Dc, :] * inv_l).astype(BF16)
    y_t = jnp.concatenate([_dot(wuvt_ref[h], o_lat_t[:, h * QB:(h + 1) * QB]) for h in range(H)], axis=0)
    o_ref[...] = y_t.T.astype(BF16)


def _dsa_attn(qlt, qit, w_row, ckvn, ckvt, idxk, bias_tiles, wuvt, B, T):
    H, HI, dI, Dc, dh = DSA_HEADS, IDX_HEADS, IDX_DIM, DSA_KV_RANK, DSA_DH
    QB = min(Q_BLOCK, T)
    KB = KEY_BLOCK
    nq = T // QB
    nk = T // KB
    topk = min(DSA_TOPK_MAX, T // 4)
    neg_key = int(np.array(NEG_BIG, np.float32).view(np.int32))
    neg_key = neg_key ^ ((neg_key >> 31) & 0x7FFFFFFF)
    return pl.pallas_call(
        functools.partial(_dsa_attn_kernel, topk, T, neg_key),
        grid=(B, nq),
        in_specs=[pl.BlockSpec((1, Dc, H * QB), lambda b, i: (b * nq + i, 0, 0)),
                  pl.BlockSpec((1, dI, HI * QB), lambda b, i: (b * nq + i, 0, 0)),
                  pl.BlockSpec((1, 1, HI * QB), lambda b, i: (b * nq + i, 0, 0)),
                  pl.BlockSpec((T, Dc), lambda b, i: (b, 0)),
                  pl.BlockSpec((1, nk, Dc + SUBLANES, KB), lambda b, i: (b, 0, 0, 0)),
                  pl.BlockSpec((T, dI), lambda b, i: (b, 0)),
                  pl.BlockSpec((4, KB, H * QB), lambda b, i: (0, 0, 0), pipeline_mode=pl.Buffered(1)),
                  pl.BlockSpec((H, dh, Dc), lambda b, i: (0, 0, 0))],
        out_specs=pl.BlockSpec((QB, DSA_W), lambda b, i: (b * nq + i, 0)),
        out_shape=jax.ShapeDtypeStruct((B * T, DSA_W), BF16),
        scratch_shapes=[pltpu.VMEM((nk, KB, QB), I32),
                        pltpu.VMEM((nk, KB, H * QB), F32),
                        pltpu.VMEM((Dc + SUBLANES, H * QB), F32),
                        pltpu.VMEM((SUBLANES, QB), I32)],
        compiler_params=_params("parallel", "arbitrary"),
        name="dsa_attn",
    )(qlt, qit, w_row, ckvn, ckvt, idxk, bias_tiles, wuvt)


def _rel_bias_tiles(rel_bias, QB, KB):
    max_exact = REL_BUCKETS // 2
    n_far = 3 * QB
    assert n_far - (KB - 1) > REL_MAX_DIST
    n = jnp.arange(n_far, dtype=jnp.int32)
    nf = jnp.maximum(n, max_exact).astype(F32)
    large = max_exact + (jnp.log(nf / max_exact) / math.log(REL_MAX_DIST / max_exact)
                         * (REL_BUCKETS - max_exact)).astype(jnp.int32)
    large = jnp.minimum(large, REL_BUCKETS - 1)
    bucket = jnp.where(n < max_exact, n, large)
    H = rel_bias.shape[1]
    tab = jnp.take(rel_bias, bucket, axis=0).T
    P = KB + QB
    tiles = []
    for delta in (0, QB, 2 * QB):
        d = np.concatenate([np.arange(QB + 1), np.arange(-(KB - 1), 0)])
        g = tab[:, np.clip(delta + d, 0, n_far - 1)]
        rows = jnp.tile(g, (1, KB))[:, :KB * (P - 1)].reshape(H, KB, P - 1)[:, :, :QB]
        tiles.append(rows.transpose(1, 0, 2).reshape(KB, H * QB))
    tiles.append(jnp.broadcast_to(jnp.repeat(rel_bias[REL_BUCKETS - 1], QB)[None, :], (KB, H * QB)))
    return jnp.stack(tiles).astype(F32)


def _merge_kernel(x_ref, yr_ref, yd_ref, yh_ref, lnpre_ref, wg_ref, wr_ref, wd_ref, wh_ref, wo_ref,
                  ln_ref, o_ref):
    D = x_ref.shape[1]
    x = x_ref[...]
    h = _rms(x, lnpre_ref[...]).astype(BF16)
    m = None
    for k, (y_ref, w_ref) in enumerate(((yr_ref, wr_ref), (yd_ref, wd_ref), (yh_ref, wh_ref))):
        gate = _sigmoid(_dot(h, wg_ref[:, k * D:(k + 1) * D]))
        term = gate * _dot(y_ref[...], w_ref[...])
        m = term if m is None else m + term
    u = _dot(m.astype(BF16), wo_ref[...])
    o_ref[...] = x + _rms(u, ln_ref[...])


def _merge(x2, y_ret, y_dsa, y_hg, lnpre, wg, wr, wd, wh, wo, ln):
    M = x2.shape[0]
    tm = min(PROJ_ROWS, M)
    D = D_MODEL
    row = lambda w: pl.BlockSpec((tm, w), lambda i: (i, 0))
    const = lambda r, c: pl.BlockSpec((r, c), lambda i: (0, 0))
    return pl.pallas_call(
        _merge_kernel,
        grid=(M // tm,),
        in_specs=[row(D), row(RET_W), row(DSA_W), row(HGRN_W), const(1, D), const(D, N_BRANCH * D),
                  const(RET_W, D), const(DSA_W, D), const(HGRN_W, D), const(D, D), const(1, D)],
        out_specs=row(D),
        out_shape=jax.ShapeDtypeStruct((M, D), F32),
        compiler_params=_params("parallel"),
        name="merge",
    )(x2, y_ret, y_dsa, y_hg, lnpre, wg, wr, wd, wh, wo, ln)


def _gelu_tanh(x):
    return 0.5 * x * (1.0 + jnp.tanh(math.sqrt(2.0 / math.pi) * (x + 0.044715 * (x * x * x))))


def _ffn_kernel(tiles_per_seq, x_ref, lnpre_ref, wup_ref, cw_ref, cb_ref, wdn_ref, lnpost_ref,
                o_ref, buf_ref, prev_ref):
    tm = x_ref.shape[0]
    HALO = SUBLANES
    fc = buf_ref.shape[1]
    n_pass = D_FF // fc
    first = (pl.program_id(0) % tiles_per_seq) == 0

    @pl.when(first)
    def _():
        prev_ref[...] = jnp.zeros_like(prev_ref)

    x = x_ref[...]
    h = _rms(x, lnpre_ref[...]).astype(BF16)

    def conv(part, c):
        col = part * D_FF + c * fc
        up = _dot(h, wup_ref[:, col:col + fc])
        slot = part * n_pass + c
        buf_ref[0:HALO, :] = prev_ref[slot]
        buf_ref[HALO:HALO + tm, :] = up
        prev_ref[slot] = up[tm - HALO:tm, :]
        w = cw_ref[:, col:col + fc]
        y = (up * w[2:3] + buf_ref[HALO - 1:HALO - 1 + tm, :] * w[1:2]
             + buf_ref[HALO - 2:HALO - 2 + tm, :] * w[0:1])
        return y + cb_ref[:, col:col + fc]

    acc = None
    for c in range(n_pass):
        a = conv(0, c)
        u = conv(1, c)
        act = (_gelu_tanh(a) * u).astype(BF16)
        d = _dot(act, wdn_ref[c * fc:(c + 1) * fc, :])
        acc = d if acc is None else acc + d
    o_ref[...] = x + _rms(acc, lnpost_ref[...])


def _ffn(x2, lnpre, wup, cw, cb, wdn, lnpost, T):
    M = x2.shape[0]
    D = D_MODEL
    tm = min(FFN_ROWS, T)
    fc = D_FF
    const = lambda r, c: pl.BlockSpec((r, c), lambda i: (0, 0))
    return pl.pallas_call(
        functools.partial(_ffn_kernel, T // tm),
        grid=(M // tm,),
        in_specs=[pl.BlockSpec((tm, D), lambda i: (i, 0)), const(1, D), const(D, 2 * D_FF),
                  const(CONV_WIDTH, 2 * D_FF), const(1, 2 * D_FF), const(D_FF, D), const(1, D)],
        out_specs=pl.BlockSpec((tm, D), lambda i: (i, 0)),
        out_shape=jax.ShapeDtypeStruct((M, D), F32),
        scratch_shapes=[pltpu.VMEM((tm + SUBLANES, fc), F32),
                        pltpu.VMEM((2 * (D_FF // fc), SUBLANES, fc), F32)],
        compiler_params=_params("arbitrary"),
        name="conv_ffn",
    )(x2, lnpre, wup, cw, cb, wdn, lnpost)


def _split_w_in(w):
    widths = (256, 256, 512, 512, 256, 128, 64, 8, 512, 512, 512, 512, N_BRANCH * D_MODEL)
    offs = np.concatenate([[0], np.cumsum(widths)])
    rq, rk, rv, rg, cq, ckv, ik, iw, hq, hf, hv, hg, gt = [w[..., offs[k]:offs[k + 1]] for k in range(len(widths))]
    zeros = lambda n: jnp.zeros(w.shape[:-1] + (n,), w.dtype)
    wb = jnp.concatenate([rq, rk, rv, cq, ckv, ik, zeros(2 * DSA_KV_RANK - DSA_KV_RANK - IDX_DIM), hq, hv], axis=-1)
    wf = jnp.concatenate([rg, hf, hg, iw, zeros(LANES - IDX_HEADS)], axis=-1)
    assert wb.shape[-1] == ZB_W and wf.shape[-1] == ZF_W
    return wb.astype(BF16), wf.astype(BF16), gt.astype(BF16)


def kernel(x, rel_bias, hgrn_lb, ln_mix_pre, ln_mix_post, ln_ffn_pre, ln_ffn_post, w_in, dsa_q_norm, dsa_kv_norm, dsa_w_uq, dsa_w_uk, dsa_w_uv, hgrn_norm, w_br_ret, w_br_dsa, w_br_hgrn, w_out, ffn_w_up, ffn_conv_w, ffn_conv_b, ffn_w_down):
    B, T, D = x.shape
    depth = w_in.shape[0]
    assert T % KEY_BLOCK == 0 and D == D_MODEL
    H, dh, Rq = DSA_HEADS, DSA_DH, DSA_Q_RANK
    bias_tiles = _rel_bias_tiles(rel_bias, min(Q_BLOCK, T), KEY_BLOCK)
    wq = dsa_w_uq[..., :H * dh].astype(BF16)
    wqi = dsa_w_uq[..., H * dh:].astype(BF16)
    wuk = dsa_w_uk.transpose(0, 1, 3, 2).astype(BF16)
    wuvt = dsa_w_uv.transpose(0, 1, 3, 2).astype(BF16)
    x2 = x.reshape(B * T, D)
    row = lambda v: v.reshape(1, -1)
    for l in range(depth):
        wb, wf, wg = _split_w_in(w_in[l])
        zb, zf = _inproj(x2, row(ln_mix_pre[l]), wb, wf)
        y_ret = _retention(zb, zf, B, T)
        y_hg = _hgrn(zb, zf, hgrn_lb, row(hgrn_norm[l]), l, B, T)
        qlt, qit, w_row, ckvn, ckvt, idxk = _dsa_prep(zb, zf, row(dsa_q_norm[l]), row(dsa_kv_norm[l]),
                                                      wq[l], wqi[l], wuk[l], B, T)
        y_dsa = _dsa_attn(qlt, qit, w_row, ckvn, ckvt, idxk, bias_tiles, wuvt[l], B, T)
        x2 = _merge(x2, y_ret, y_dsa, y_hg, row(ln_mix_pre[l]), wg, w_br_ret[l].astype(BF16),
                    w_br_dsa[l].astype(BF16), w_br_hgrn[l].astype(BF16), w_out[l].astype(BF16),
                    row(ln_mix_post[l]))
        x2 = _ffn(x2, row(ln_ffn_pre[l]), ffn_w_up[l].astype(BF16), ffn_conv_w[l], row(ffn_conv_b[l]),
                  ffn_w_down[l].astype(BF16), row(ln_ffn_post[l]), T)
    return x2.reshape(B, T, D)
```

```python
import functools
import math

import jax
import jax.numpy as jnp
import numpy as np
from jax import lax
from jax.experimental import pallas as pl
from jax.experimental.pallas import tpu as pltpu

F32 = jnp.float32
BF16 = jnp.bfloat16
I32 = jnp.int32

D_MODEL = 1024
RET_HEADS, RET_DK, RET_DV, RET_CHUNK = 4, 64, 128, 128
DSA_HEADS, DSA_DH, DSA_Q_RANK, DSA_KV_RANK = 8, 64, 256, 128
IDX_HEADS, IDX_DIM, DSA_TOPK_MAX = 8, 64, 256
Q_BLOCK = 256
HGRN_HEADS, HGRN_EXPAND, HGRN_DV = 4, 128, 128
F_FLOOR = 1e-6
REL_BUCKETS, REL_MAX_DIST = 32, 128
D_FF = 2816
CONV_WIDTH = 3
N_BRANCH = 3
EPS = 1e-6
NEG_BIG = -1e30

RET_W = RET_HEADS * RET_DV
DSA_W = DSA_HEADS * DSA_DH
HGRN_KW = HGRN_HEADS * HGRN_EXPAND
HGRN_W = HGRN_HEADS * HGRN_DV

ZB_RQ, ZB_RK, ZB_RV, ZB_CQ = 0, 256, 512, 1024
ZB_CKV = 1280
ZB_HQ, ZB_HV = 1536, 2048
ZB_W = 2560
ZF_RG, ZF_HF, ZF_HG, ZF_IW = 0, 512, 1024, 1536
ZF_W = 1664

VMEM_LIMIT_BYTES = 56 * 1024 * 1024
SUBLANES = 8
LANES = 128

KEY_BLOCK = 256
DSA_PREP_ROWS = 2048
PROJ_ROWS = 1024
FFN_ROWS = 512
HGRN_ROWS = 512
RET_ROWS = 256
HGRN_PAIR_CHUNK = 256
INT_MIN = -(2 ** 31)


def _params(*sem):
    return pltpu.CompilerParams(dimension_semantics=sem, vmem_limit_bytes=VMEM_LIMIT_BYTES)


def _dot(a, b):
    return jnp.dot(a, b, preferred_element_type=F32)


def _dot_nt(a, b):
    return lax.dot_general(a, b, (((1,), (1,)), ((), ())), preferred_element_type=F32)


def _dot_tn(a, b):
    return lax.dot_general(a, b, (((0,), (0,)), ((), ())), preferred_element_type=F32)


def _sigmoid(x):
    return 0.5 * jnp.tanh(0.5 * x) + 0.5


def _sigmoid_pair(x):
    t = jnp.exp(-jnp.abs(x))
    big = 1.0 / (1.0 + t)
    small = t * big
    pos = x >= 0.0
    return jnp.where(pos, big, small), jnp.where(pos, small, big)


def _rms(x, g):
    return x * lax.rsqrt(jnp.mean(x * x, axis=-1, keepdims=True) + EPS) * g


def _group_sum(x):
    return jnp.sum(x.reshape(x.shape[0] // SUBLANES, SUBLANES, x.shape[1]), axis=0)


def _group_max(x):
    return jnp.max(x.reshape(x.shape[0] // SUBLANES, SUBLANES, x.shape[1]), axis=0)


def _inproj_kernel(x_ref, g_ref, wb_ref, wf_ref, zb_ref, zf_ref):
    h = _rms(x_ref[...], g_ref[...]).astype(BF16)
    zb_ref[...] = _dot(h, wb_ref[...]).astype(BF16)
    zf_ref[...] = _dot(h, wf_ref[...])


def _inproj(x2, g, wb, wf):
    M = x2.shape[0]
    tm = min(PROJ_ROWS, M)
    const = lambda r, c: pl.BlockSpec((r, c), lambda i: (0, 0))
    return pl.pallas_call(
        _inproj_kernel,
        grid=(M // tm,),
        in_specs=[pl.BlockSpec((tm, D_MODEL), lambda i: (i, 0)), const(1, D_MODEL),
                  const(D_MODEL, ZB_W), const(D_MODEL, ZF_W)],
        out_specs=[pl.BlockSpec((tm, ZB_W), lambda i: (i, 0)), pl.BlockSpec((tm, ZF_W), lambda i: (i, 0))],
        out_shape=[jax.ShapeDtypeStruct((M, ZB_W), BF16), jax.ShapeDtypeStruct((M, ZF_W), F32)],
        compiler_params=_params("parallel"),
        name="inproj",
    )(x2, g, wb, wf)


def _ret_kernel(gam_ref, q_ref, k_ref, v_ref, g_ref, cos_ref, sin_ref, dm_ref, xi_ref, zeta_ref,
                o_ref, r_ref):
    rows = q_ref.shape[0]
    C = dm_ref.shape[1]
    H, dk, dv = RET_HEADS, RET_DK, RET_DV

    @pl.when(pl.program_id(1) == 0)
    def _():
        r_ref[...] = jnp.zeros_like(r_ref)

    cos = cos_ref[...]
    sin = sin_ref[...]
    lane = lax.broadcasted_iota(I32, (rows, H * dk), 1)
    first_half = (lane % dk) < (dk // 2)

    def rot(x):
        swapped = jnp.where(first_half, pltpu.roll(x, H * dk - dk // 2, 1), pltpu.roll(x, dk // 2, 1))
        return x * cos + swapped * sin

    q = rot(q_ref[...].astype(F32))
    k = rot(k_ref[...].astype(F32)) * dk ** -0.5
    for c in range(rows // C):
        rs = slice(c * C, (c + 1) * C)
        for h in range(H):
            qh = q[rs, h * dk:(h + 1) * dk].astype(BF16)
            kh = k[rs, h * dk:(h + 1) * dk]
            vh = v_ref[rs, h * dv:(h + 1) * dv]
            s = _dot_nt(qh, kh.astype(BF16)) * dm_ref[h]
            inner = _dot(s.astype(BF16), vh)
            rh = r_ref[h]
            cross = _dot(qh, rh.astype(BF16)) * xi_ref[h]
            r_ref[h] = gam_ref[h] * rh + _dot_tn((kh * zeta_ref[h]).astype(BF16), vh)
            o = inner + cross
            mu = jnp.mean(o, axis=-1, keepdims=True)
            oc = o - mu
            var = jnp.mean(oc * oc, axis=-1, keepdims=True)
            gh = g_ref[rs, h * dv:(h + 1) * dv]
            o_ref[rs, h * dv:(h + 1) * dv] = (gh * _sigmoid(gh) * (oc * lax.rsqrt(var + EPS))).astype(BF16)


def _retention(zb, zf, B, T):
    H, dk, dv = RET_HEADS, RET_DK, RET_DV
    C = min(RET_CHUNK, T)
    n = T // C
    pos = np.arange(T, dtype=np.float64)
    half = dk // 2
    freq = 1.0 / (10000.0 ** np.linspace(0.0, 1.0, half))
    ang = pos[:, None] * freq[None, :]
    cos = jnp.asarray(np.tile(np.cos(ang), (1, 2 * H)), F32)
    sin = jnp.asarray(np.tile(np.concatenate([-np.sin(ang), np.sin(ang)], axis=1), (1, H)), F32)
    log_gamma = np.log1p(-(2.0 ** (-5.0 - np.arange(H, dtype=np.float64))))
    i = np.arange(C, dtype=np.float64)
    rel = i[:, None] - i[None, :]
    dmask = jnp.asarray(np.where(rel >= 0, np.exp(np.maximum(rel, 0.0)[None] * log_gamma[:, None, None]), 0.0), F32)
    xi = jnp.asarray(np.exp((i + 1.0)[None, :] * log_gamma[:, None])[:, :, None], F32)
    zeta = jnp.asarray(np.exp((C - 1.0 - i)[None, :] * log_gamma[:, None])[:, :, None], F32)
    gamma_c = jnp.asarray(np.exp(C * log_gamma), F32)

    wq = H * dk
    wv = H * dv
    R = min(RET_ROWS, T)
    n = T // R
    full = lambda shape: pl.BlockSpec(shape, lambda b, c: (0,) * len(shape))
    return pl.pallas_call(
        _ret_kernel,
        grid=(B, n),
        in_specs=[pl.BlockSpec(memory_space=pltpu.SMEM),
                  pl.BlockSpec((R, wq), lambda b, c: (b * n + c, ZB_RQ // wq)),
                  pl.BlockSpec((R, wq), lambda b, c: (b * n + c, ZB_RK // wq)),
                  pl.BlockSpec((R, wv), lambda b, c: (b * n + c, ZB_RV // wv)),
                  pl.BlockSpec((R, wv), lambda b, c: (b * n + c, ZF_RG // wv)),
                  pl.BlockSpec((R, wq), lambda b, c: (c, 0)),
                  pl.BlockSpec((R, wq), lambda b, c: (c, 0)),
                  full((H, C, C)), full((H, C, 1)), full((H, C, 1))],
        out_specs=pl.BlockSpec((R, wv), lambda b, c: (b * n + c, 0)),
        out_shape=jax.ShapeDtypeStruct((B * T, wv), BF16),
        scratch_shapes=[pltpu.VMEM((H, dk, dv), F32)],
        compiler_params=_params("parallel", "arbitrary"),
        name="retention",
    )(gamma_c, zb, zb, zb, zf, cos, sin, dmask, xi, zeta)


def _hgrn_levels(C):
    ms, m = [], C // 2
    while m >= SUBLANES:
        ms.append(m)
        m //= 2
    return ms


def _cumsum_rows(tri, x):
    hi = x.astype(BF16)
    rest = x - hi.astype(F32)
    mid = rest.astype(BF16)
    lo = (rest - mid.astype(F32)).astype(BF16)
    return _dot(tri, hi) + (_dot(tri, mid) + _dot(tri, lo))


def _hgrn_kernel(layer, q_ref, f_ref, v_ref, g_ref, lbraw_ref, gain_ref, tri_ref, lmask_ref, o_ref, st_ref):
    TR = q_ref.shape[0]
    C = tri_ref.shape[0]
    H, dk, dv = HGRN_HEADS, HGRN_EXPAND, HGRN_DV
    SB = SUBLANES

    @pl.when(pl.program_id(1) == 0)
    def _():
        st_ref[...] = jnp.zeros_like(st_ref)

    raw = lbraw_ref[...]
    e = jnp.exp(raw - jnp.max(raw, axis=0, keepdims=True))
    soft = e / jnp.sum(e, axis=0, keepdims=True)
    cs = soft[0:1]
    for l in range(1, layer + 1):
        cs = cs + soft[l:l + 1]
    lb = jnp.clip(cs - soft[0:1], 0.0, 1.0)
    tri = tri_ref[...]
    row_in_blk = lax.broadcasted_iota(I32, (SB, dk), 0)
    lane_t = lax.broadcasted_iota(I32, (SB, C), 1)
    levels = _hgrn_levels(C)

    def chunk(c, carry):
        r0 = c * C
        sig_pos, sig_neg = _sigmoid_pair(f_ref[pl.ds(r0, C), :])
        f = lb + (1.0 - lb) * sig_pos
        log_f = jnp.log(jnp.maximum(f, F_FLOOR))
        kk = (1.0 - lb) * sig_neg
        b = _cumsum_rows(tri, log_f)
        qq = q_ref[pl.ds(r0, C), :].astype(F32)
        vb = v_ref[pl.ds(r0, C), :]
        eb = jnp.exp(b)
        b_last = b[C - 1:C, :]
        eb_last = eb[C - 1:C, :]
        q_dec = (qq * eb).astype(BF16)
        k_dec = (kk * jnp.exp(b_last - b)).astype(BF16)
        outs = []
        for h in range(H):
            sl = slice(h * dk, (h + 1) * dk)
            bh, qh, kh, vh = b[:, sl], qq[:, sl], kk[:, sl], vb[:, h * dv:(h + 1) * dv]
            diag = []
            for blk in range(C // SB):
                bs = bh[blk * SB:(blk + 1) * SB, :]
                ks = kh[blk * SB:(blk + 1) * SB, :]
                at = jnp.zeros((SB, C), F32)
                for tt in range(SB):
                    t = blk * SB + tt
                    diff = jnp.where(row_in_blk <= tt, bh[t:t + 1, :] - bs, NEG_BIG)
                    p = (qh[t:t + 1, :] * jnp.exp(diff)) * ks
                    at = jnp.where(lane_t == t, jnp.sum(p, axis=-1, keepdims=True), at)
                diag.append(at)
            a_t = jnp.concatenate(diag, axis=0)
            for lev, m in enumerate(levels):
                qs, ks = [], []
                for blk in range(C // m):
                    rows = slice(blk * m, (blk + 1) * m)
                    if blk % 2 == 1:
                        ref = bh[blk * m - 1:blk * m, :]
                        qs.append(qh[rows] * jnp.exp(bh[rows] - ref))
                        ks.append(jnp.zeros((m, dk), F32))
                    else:
                        ref = bh[(blk + 1) * m - 1:(blk + 1) * m, :]
                        ks.append(kh[rows] * jnp.exp(ref - bh[rows]))
                        qs.append(jnp.zeros((m, dk), F32))
                q_l = jnp.concatenate(qs, axis=0).astype(BF16)
                k_l = jnp.concatenate(ks, axis=0).astype(BF16)
                a_t = a_t + _dot_nt(k_l, q_l) * lmask_ref[lev]
            intra = _dot_tn(a_t.astype(BF16), vh)
            st = st_ref[h]
            outs.append(intra + _dot_nt(q_dec[:, sl], st.astype(BF16)))
            st_ref[h] = st * eb_last[:, sl] + _dot_tn(vh, k_dec[:, sl])
        o = _rms(jnp.concatenate(outs, axis=1), gain_ref[...])
        gg = g_ref[pl.ds(r0, C), :]
        o_ref[pl.ds(r0, C), :] = (gg * _sigmoid(gg) * o).astype(BF16)
        return carry

    for c in range(TR // C):
        chunk(c, 0)


def _hgrn(zb, zf, hgrn_lb, gain, layer, B, T):
    H, dk, dv = HGRN_HEADS, HGRN_EXPAND, HGRN_DV
    C = min(HGRN_PAIR_CHUNK, T)
    TR = min(HGRN_ROWS, T)
    n = T // TR
    w = H * dk
    L = hgrn_lb.shape[0]
    tri = jnp.tril(jnp.ones((C, C), BF16))
    idx = np.arange(C)
    lmask = np.stack([(((idx[None, :] // m) % 2 == 1) & (idx[:, None] // m == idx[None, :] // m - 1))
                      for m in _hgrn_levels(C)]).astype(np.float32)
    return pl.pallas_call(
        functools.partial(_hgrn_kernel, layer),
        grid=(B, n),
        in_specs=[pl.BlockSpec((TR, w), lambda b, c: (b * n + c, ZB_HQ // w)),
                  pl.BlockSpec((TR, w), lambda b, c: (b * n + c, ZF_HF // w)),
                  pl.BlockSpec((TR, w), lambda b, c: (b * n + c, ZB_HV // w)),
                  pl.BlockSpec((TR, w), lambda b, c: (b * n + c, ZF_HG // w)),
                  pl.BlockSpec((L, w), lambda b, c: (0, 0)),
                  pl.BlockSpec((1, w), lambda b, c: (0, 0)),
                  pl.BlockSpec((C, C), lambda b, c: (0, 0)),
                  pl.BlockSpec(lmask.shape, lambda b, c: (0, 0, 0))],
        out_specs=pl.BlockSpec((TR, w), lambda b, c: (b * n + c, 0)),
        out_shape=jax.ShapeDtypeStruct((B * T, w), BF16),
        scratch_shapes=[pltpu.VMEM((H, dv, dk), F32)],
        compiler_params=_params("parallel", "arbitrary"),
        name="hgrn2",
    )(zb, zf, zb, zf, hgrn_lb, gain, tri, jnp.asarray(lmask))


def _dsa_prep_kernel(cq_ref, kv_ref, iw_ref, qn_ref, kn_ref, wq_ref, wqi_ref, wukt_ref,
                     qlt_ref, qit_ref, w_ref, ckv_ref, ckvt_ref, ik_ref):
    tm = cq_ref.shape[0]
    H, dh, HI, dI, Dc = DSA_HEADS, DSA_DH, IDX_HEADS, IDX_DIM, DSA_KV_RANK
    QB = qlt_ref.shape[2] // H
    nb = tm // QB
    cq = _rms(cq_ref[...].astype(F32), qn_ref[...]).astype(BF16)
    ckv = _rms(kv_ref[:, :Dc].astype(F32), kn_ref[...])
    ckv_ref[...] = ckv.astype(BF16)
    KB = ckvt_ref.shape[3]
    for j in range(tm // KB):
        ckvt_ref[0, j, 0:Dc, :] = ckv[j * KB:(j + 1) * KB].T.astype(BF16)
        ckvt_ref[0, j, Dc:Dc + SUBLANES, :] = jnp.ones((SUBLANES, KB), BF16)
    ik_ref[...] = kv_ref[:, Dc:Dc + dI]
    w_t = iw_ref[...].T[0:HI, :] * (HI * dI) ** -0.5
    qi_all = _dot(cq, wqi_ref[...])
    q_all = _dot(cq, wq_ref[...])
    for j in range(nb):
        rows = slice(j * QB, (j + 1) * QB)
        qi_t = qi_all[rows].T
        q_t = q_all[rows].T.astype(BF16)
        for h in range(HI):
            qit_ref[j, :, h * QB:(h + 1) * QB] = qi_t[h * dI:(h + 1) * dI].astype(BF16)
            w_ref[j, :, h * QB:(h + 1) * QB] = w_t[h:h + 1, rows]
        for h in range(H):
            ql_t = _dot(wukt_ref[h], q_t[h * dh:(h + 1) * dh]) * dh ** -0.5
            qlt_ref[j, :, h * QB:(h + 1) * QB] = ql_t.astype(BF16)


def _dsa_prep(zb, zf, qn, kn, wq, wqi, wuk, B, T):
    M = B * T
    H, dh, HI, dI, Dc, Rq = DSA_HEADS, DSA_DH, IDX_HEADS, IDX_DIM, DSA_KV_RANK, DSA_Q_RANK
    QB = min(Q_BLOCK, T)
    KB = KEY_BLOCK
    tm = min(DSA_PREP_ROWS, T)
    nb = tm // QB
    nt = T // tm
    kpt = tm // KB
    return pl.pallas_call(
        _dsa_prep_kernel,
        grid=(M // tm,),
        in_specs=[pl.BlockSpec((tm, Rq), lambda i: (i, ZB_CQ // Rq)),
                  pl.BlockSpec((tm, 2 * Dc), lambda i: (i, ZB_CKV // (2 * Dc))),
                  pl.BlockSpec((tm, LANES), lambda i: (i, ZF_IW // LANES)),
                  pl.BlockSpec((1, Rq), lambda i: (0, 0)),
                  pl.BlockSpec((1, Dc), lambda i: (0, 0)),
                  pl.BlockSpec((Rq, H * dh), lambda i: (0, 0)),
                  pl.BlockSpec((Rq, HI * dI), lambda i: (0, 0)),
                  pl.BlockSpec((H, Dc, dh), lambda i: (0, 0, 0))],
        out_specs=[pl.BlockSpec((nb, Dc, H * QB), lambda i: (i, 0, 0)),
                   pl.BlockSpec((nb, dI, HI * QB), lambda i: (i, 0, 0)),
                   pl.BlockSpec((nb, 1, HI * QB), lambda i: (i, 0, 0)),
                   pl.BlockSpec((tm, Dc), lambda i: (i, 0)),
                   pl.BlockSpec((1, kpt, Dc + SUBLANES, KB), lambda i: (i // nt, i % nt, 0, 0)),
                   pl.BlockSpec((tm, dI), lambda i: (i, 0))],
        out_shape=[jax.ShapeDtypeStruct((M // QB, Dc, H * QB), BF16),
                   jax.ShapeDtypeStruct((M // QB, dI, HI * QB), BF16),
                   jax.ShapeDtypeStruct((M // QB, 1, HI * QB), F32),
                   jax.ShapeDtypeStruct((M, Dc), BF16),
                   jax.ShapeDtypeStruct((B, nt * kpt, Dc + SUBLANES, KB), BF16),
                   jax.ShapeDtypeStruct((M, dI), BF16)],
        compiler_params=_params("parallel"),
        name="dsa_prep",
    )(zb, zb, zf, qn, kn, wq, wqi, wuk)


def _sortable_key(s):
    s = jnp.where(s == 0.0, 0.0, s)
    bits = pltpu.bitcast(s, I32)
    return bits ^ ((bits >> 31) & 0x7FFFFFFF)


def _two_per_trip(lo, hi, body, init):
    def pair(j, carry):
        return body(lo + 2 * j + 1, body(lo + 2 * j, carry))

    carry = lax.fori_loop(0, (hi - lo) // 2, pair, init)
    return lax.cond((hi - lo) % 2 == 1, lambda c: body(hi - 1, c), lambda c: c, carry)


def _dsa_attn_kernel(topk, n_keys, neg_key, qlt_ref, qit_ref, w_ref, ckv_ref, ckvt_ref, ik_ref, bias_ref,
                     wuvt_ref, o_ref, key_ref, z_ref, acc_ref, tie_ref):
    H, HI = DSA_HEADS, IDX_HEADS
    QB = o_ref.shape[0]
    KB = key_ref.shape[1]
    q0 = pl.program_id(1) * QB
    nkb = (q0 + QB - 1) // KB + 1
    n_skip = n_keys - nkb * KB

    s_row = lax.broadcasted_iota(I32, (KB, QB), 0)
    t_row = q0 + lax.broadcasted_iota(I32, (1, QB), 1)

    qit = qit_ref[0]
    w_row = w_ref[0]

    def score_blk(kb, carry):
        k0 = pl.multiple_of(kb * KB, KB)
        p = _dot(ik_ref[pl.ds(k0, KB), :], qit)
        p = jnp.maximum(p, 0.0) * w_row
        s = p[:, 0:QB]
        for h in range(1, HI):
            s = s + p[:, h * QB:(h + 1) * QB]
        s = jnp.where(k0 + s_row <= t_row, s, NEG_BIG)
        key_ref[kb] = _sortable_key(s)
        return carry

    _two_per_trip(0, nkb, score_blk, 0)

    def count(hit_fn):
        def body(kb, acc):
            return acc + _group_sum(jnp.where(hit_fn(kb, key_ref[kb]), 1, 0))

        acc = _two_per_trip(0, nkb, body, jnp.zeros((SUBLANES, QB), I32))
        return jnp.sum(acc, axis=0, keepdims=True)

    def count_ge(cand):
        return count(lambda kb, kk: kk >= cand) + jnp.where(cand <= neg_key, n_skip, 0)

    zero = jnp.zeros((1, QB), I32)
    cnt0 = count_ge(zero)
    state = (jnp.where(cnt0 >= topk, zero, jnp.full((1, QB), INT_MIN, I32)),
             jnp.where(cnt0 >= topk, cnt0, n_keys))

    def bisect(it, state):
        thr, cnt = state
        cand = thr | jnp.left_shift(jnp.int32(1), 30 - it)
        c = count_ge(cand)
        return jnp.where(c >= topk, cand, thr), jnp.where(c >= topk, c, cnt)

    thr, cnt = lax.fori_loop(0, 31, bisect, state)

    nbits = max(1, (n_keys - 1).bit_length())
    tie_ref[...] = jnp.full(tie_ref.shape, (1 << nbits) - 1, I32)

    @pl.when(jnp.max(jnp.where(cnt > topk, 1, 0)) > 0)
    def _():
        n_gt = count(lambda kb, kk: kk > thr) + jnp.where(thr < neg_key, n_skip, 0)
        need = topk - n_gt

        def ibisect(it, p):
            cand = p | jnp.left_shift(jnp.int32(1), nbits - 1 - it)
            below = count(lambda kb, kk: (kk == thr) & (kb * KB + s_row < cand))
            return jnp.where(below < need, cand, p)

        p = lax.fori_loop(0, nbits, ibisect, jnp.zeros((1, QB), I32))
        tie_ref[...] = jnp.broadcast_to(p, tie_ref.shape)

    tie = tie_ref[0:1, :]

    qlt = qlt_ref[0]

    FAR_TILE = bias_ref.shape[0] - 1
    n_far = jnp.maximum((q0 - FAR_TILE * QB) // KB + 1, 0)
    far_bias = bias_ref[FAR_TILE, 0:1, :]

    def logits_blk(near, kb, m8):
        k0 = pl.multiple_of(kb * KB, KB)
        kk = key_ref[kb]
        s_glob = k0 + s_row
        sel = (kk > thr) | ((kk == thr) & (s_glob <= tie))
        z = _dot(ckv_ref[pl.ds(k0, KB), :], qlt)
        if near:
            sel = sel & (s_glob <= t_row)
            z = z + bias_ref[(q0 - k0) // QB]
        tops = []
        for h in range(H):
            zh = jnp.where(sel, z[:, h * QB:(h + 1) * QB], NEG_BIG)
            z_ref[kb, :, h * QB:(h + 1) * QB] = zh
            tops.append(_group_max(zh))
        return jnp.maximum(m8, jnp.concatenate(tops, axis=1))

    neg8 = jnp.full((SUBLANES, H * QB), NEG_BIG, F32)
    m8_far = _two_per_trip(0, n_far, functools.partial(logits_blk, False), neg8)
    m8_near = _two_per_trip(n_far, nkb, functools.partial(logits_blk, True), neg8)
    m = jnp.maximum(jnp.max(m8_far, axis=0, keepdims=True) + far_bias,
                    jnp.max(m8_near, axis=0, keepdims=True))
    m_far = m - far_bias
    acc_ref[...] = jnp.zeros(acc_ref.shape, F32)

    def pv_blk(kb):
        p = jnp.exp((z_ref[kb] - jnp.where(kb < n_far, m_far, m)).astype(BF16))
        return _dot(ckvt_ref[0, kb], p)

    def pv_pair(j, carry):
        acc_ref[...] += pv_blk(2 * j) + pv_blk(2 * j + 1)
        return carry

    lax.fori_loop(0, nkb // 2, pv_pair, 0)

    @pl.when(nkb % 2 == 1)
    def _():
        acc_ref[...] += pv_blk(nkb - 1)

    Dc = acc_ref.shape[0] - SUBLANES
    inv_l = 1.0 / acc_ref[Dc:Dc + 1, :]
    o_lat_t = (acc_ref[0:Dc, :] * inv_l).astype(BF16)
    y_t = jnp.concatenate([_dot(wuvt_ref[h], o_lat_t[:, h * QB:(h + 1) * QB]) for h in range(H)], axis=0)
    o_ref[...] = y_t.T.astype(BF16)


def _dsa_attn(qlt, qit, w_row, ckvn, ckvt, idxk, bias_tiles, wuvt, B, T):
    H, HI, dI, Dc, dh = DSA_HEADS, IDX_HEADS, IDX_DIM, DSA_KV_RANK, DSA_DH
    QB = min(Q_BLOCK, T)
    KB = KEY_BLOCK
    nq = T // QB
    nk = T // KB
    topk = min(DSA_TOPK_MAX, T // 4)
    neg_key = int(np.array(NEG_BIG, np.float32).view(np.int32))
    neg_key = neg_key ^ ((neg_key >> 31) & 0x7FFFFFFF)
    return pl.pallas_call(
        functools.partial(_dsa_attn_kernel, topk, T, neg_key),
        grid=(B, nq),
        in_specs=[pl.BlockSpec((1, Dc, H * QB), lambda b, i: (b * nq + i, 0, 0)),
                  pl.BlockSpec((1, dI, HI * QB), lambda b, i: (b * nq + i, 0, 0)),
                  pl.BlockSpec((1, 1, HI * QB), lambda b, i: (b * nq + i, 0, 0)),
                  pl.BlockSpec((T, Dc), lambda b, i: (b, 0)),
                  pl.BlockSpec((1, nk, Dc + SUBLANES, KB), lambda b, i: (b, 0, 0, 0)),
                  pl.BlockSpec((T, dI), lambda b, i: (b, 0)),
                  pl.BlockSpec(bias_tiles.shape, lambda b, i: (0, 0, 0), pipeline_mode=pl.Buffered(1)),
                  pl.BlockSpec((H, dh, Dc), lambda b, i: (0, 0, 0))],
        out_specs=pl.BlockSpec((QB, DSA_W), lambda b, i: (b * nq + i, 0)),
        out_shape=jax.ShapeDtypeStruct((B * T, DSA_W), BF16),
        scratch_shapes=[pltpu.VMEM((nk, KB, QB), I32),
                        pltpu.VMEM((nk, KB, H * QB), F32),
                        pltpu.VMEM((Dc + SUBLANES, H * QB), F32),
                        pltpu.VMEM((SUBLANES, QB), I32)],
        compiler_params=_params("parallel", "arbitrary"),
        name="dsa_attn",
    )(qlt, qit, w_row, ckvn, ckvt, idxk, bias_tiles, wuvt)


def _rel_bias_tiles(rel_bias, QB, KB):
    max_exact = REL_BUCKETS // 2
    n_near = 1
    while n_near * QB - (KB - 1) <= REL_MAX_DIST:
        n_near += 1
    n_far = (n_near + 1) * QB
    n = jnp.arange(n_far, dtype=jnp.int32)
    nf = jnp.maximum(n, max_exact).astype(F32)
    large = max_exact + (jnp.log(nf / max_exact) / math.log(REL_MAX_DIST / max_exact)
                         * (REL_BUCKETS - max_exact)).astype(jnp.int32)
    large = jnp.minimum(large, REL_BUCKETS - 1)
    bucket = jnp.where(n < max_exact, n, large)
    H = rel_bias.shape[1]
    tab = jnp.take(rel_bias, bucket, axis=0).T
    P = KB + QB
    tiles = []
    for delta in range(0, n_near * QB, QB):
        d = np.concatenate([np.arange(QB + 1), np.arange(-(KB - 1), 0)])
        g = tab[:, np.clip(delta + d, 0, n_far - 1)]
        rows = jnp.tile(g, (1, KB))[:, :KB * (P - 1)].reshape(H, KB, P - 1)[:, :, :QB]
        tiles.append(rows.transpose(1, 0, 2).reshape(KB, H * QB))
    tiles.append(jnp.broadcast_to(jnp.repeat(rel_bias[REL_BUCKETS - 1], QB)[None, :], (KB, H * QB)))
    return jnp.stack(tiles).astype(F32)


def _merge_kernel(x_ref, yr_ref, yd_ref, yh_ref, lnpre_ref, wg_ref, wr_ref, wd_ref, wh_ref, wo_ref,
                  ln_ref, o_ref):
    D = x_ref.shape[1]
    x = x_ref[...]
    h = _rms(x, lnpre_ref[...]).astype(BF16)
    W = D // 2
    u = None
    for c in range(D // W):
        cols = slice(c * W, (c + 1) * W)
        m = None
        for k, (y_ref, w_ref) in enumerate(((yr_ref, wr_ref), (yd_ref, wd_ref), (yh_ref, wh_ref))):
            gate = _sigmoid(_dot(h, wg_ref[:, k * D + c * W:k * D + (c + 1) * W]))
            term = gate * _dot(y_ref[...], w_ref[:, cols])
            m = term if m is None else m + term
        part = _dot(m.astype(BF16), wo_ref[cols, :])
        u = part if u is None else u + part
    o_ref[...] = x + _rms(u, ln_ref[...])


def _merge(x2, y_ret, y_dsa, y_hg, lnpre, wg, wr, wd, wh, wo, ln):
    M = x2.shape[0]
    tm = min(PROJ_ROWS, M)
    D = D_MODEL
    row = lambda w: pl.BlockSpec((tm, w), lambda i: (i, 0))
    const = lambda r, c: pl.BlockSpec((r, c), lambda i: (0, 0))
    return pl.pallas_call(
        _merge_kernel,
        grid=(M // tm,),
        in_specs=[row(D), row(RET_W), row(DSA_W), row(HGRN_W), const(1, D), const(D, N_BRANCH * D),
                  const(RET_W, D), const(DSA_W, D), const(HGRN_W, D), const(D, D), const(1, D)],
        out_specs=row(D),
        out_shape=jax.ShapeDtypeStruct((M, D), F32),
        compiler_params=_params("parallel"),
        name="merge",
    )(x2, y_ret, y_dsa, y_hg, lnpre, wg, wr, wd, wh, wo, ln)


def _gelu_tanh(x):
    return 0.5 * x * (1.0 + jnp.tanh(math.sqrt(2.0 / math.pi) * (x + 0.044715 * (x * x * x))))


def _ffn_kernel(tiles_per_seq, x_ref, lnpre_ref, wup_ref, cw_ref, cb_ref, wdn_ref, lnpost_ref,
                o_ref, buf_ref, prev_ref):
    tm = x_ref.shape[0]
    HALO = SUBLANES
    fc = buf_ref.shape[1]
    n_pass = D_FF // fc
    first = (pl.program_id(0) % tiles_per_seq) == 0

    @pl.when(first)
    def _():
        prev_ref[...] = jnp.zeros_like(prev_ref)

    x = x_ref[...]
    h = _rms(x, lnpre_ref[...]).astype(BF16)

    def conv(part, c):
        col = part * D_FF + c * fc
        up = _dot(h, wup_ref[:, col:col + fc])
        slot = part * n_pass + c
        buf_ref[0:HALO, :] = prev_ref[slot]
        buf_ref[HALO:HALO + tm, :] = up
        prev_ref[slot] = up[tm - HALO:tm, :]
        w = cw_ref[:, col:col + fc]
        y = (up * w[2:3] + buf_ref[HALO - 1:HALO - 1 + tm, :] * w[1:2]
             + buf_ref[HALO - 2:HALO - 2 + tm, :] * w[0:1])
        return y + cb_ref[:, col:col + fc]

    acc = None
    for c in range(n_pass):
        a = conv(0, c)
        u = conv(1, c)
        act = (_gelu_tanh(a) * u).astype(BF16)
        d = _dot(act, wdn_ref[c * fc:(c + 1) * fc, :])
        acc = d if acc is None else acc + d
    o_ref[...] = x + _rms(acc, lnpost_ref[...])


def _ffn(x2, lnpre, wup, cw, cb, wdn, lnpost, T):
    M = x2.shape[0]
    D = D_MODEL
    tm = min(FFN_ROWS, T)
    fc = D_FF
    const = lambda r, c: pl.BlockSpec((r, c), lambda i: (0, 0))
    return pl.pallas_call(
        functools.partial(_ffn_kernel, T // tm),
        grid=(M // tm,),
        in_specs=[pl.BlockSpec((tm, D), lambda i: (i, 0)), const(1, D), const(D, 2 * D_FF),
                  const(CONV_WIDTH, 2 * D_FF), const(1, 2 * D_FF), const(D_FF, D), const(1, D)],
        out_specs=pl.BlockSpec((tm, D), lambda i: (i, 0)),
        out_shape=jax.ShapeDtypeStruct((M, D), F32),
        scratch_shapes=[pltpu.VMEM((tm + SUBLANES, fc), F32),
                        pltpu.VMEM((2 * (D_FF // fc), SUBLANES, fc), F32)],
        compiler_params=_params("arbitrary"),
        name="conv_ffn",
    )(x2, lnpre, wup, cw, cb, wdn, lnpost)


def _split_w_in(w):
    widths = (256, 256, 512, 512, 256, 128, 64, 8, 512, 512, 512, 512, N_BRANCH * D_MODEL)
    offs = np.concatenate([[0], np.cumsum(widths)])
    rq, rk, rv, rg, cq, ckv, ik, iw, hq, hf, hv, hg, gt = [w[..., offs[k]:offs[k + 1]] for k in range(len(widths))]
    zeros = lambda n: jnp.zeros(w.shape[:-1] + (n,), w.dtype)
    wb = jnp.concatenate([rq, rk, rv, cq, ckv, ik, zeros(2 * DSA_KV_RANK - DSA_KV_RANK - IDX_DIM), hq, hv], axis=-1)
    wf = jnp.concatenate([rg, hf, hg, iw, zeros(LANES - IDX_HEADS)], axis=-1)
    assert wb.shape[-1] == ZB_W and wf.shape[-1] == ZF_W
    return wb.astype(BF16), wf.astype(BF16), gt.astype(BF16)


def kernel(x, rel_bias, hgrn_lb, ln_mix_pre, ln_mix_post, ln_ffn_pre, ln_ffn_post, w_in, dsa_q_norm, dsa_kv_norm, dsa_w_uq, dsa_w_uk, dsa_w_uv, hgrn_norm, w_br_ret, w_br_dsa, w_br_hgrn, w_out, ffn_w_up, ffn_conv_w, ffn_conv_b, ffn_w_down):
    B, T, D = x.shape
    depth = w_in.shape[0]
    assert T % KEY_BLOCK == 0 and D == D_MODEL
    H, dh, Rq = DSA_HEADS, DSA_DH, DSA_Q_RANK
    bias_tiles = _rel_bias_tiles(rel_bias, min(Q_BLOCK, T), KEY_BLOCK)
    wq = dsa_w_uq[..., :H * dh].astype(BF16)
    wqi = dsa_w_uq[..., H * dh:].astype(BF16)
    wuk = dsa_w_uk.transpose(0, 1, 3, 2).astype(BF16)
    wuvt = dsa_w_uv.transpose(0, 1, 3, 2).astype(BF16)
    x2 = x.reshape(B * T, D)
    row = lambda v: v.reshape(1, -1)
    for l in range(depth):
        wb, wf, wg = _split_w_in(w_in[l])
        zb, zf = _inproj(x2, row(ln_mix_pre[l]), wb, wf)
        y_ret = _retention(zb, zf, B, T)
        y_hg = _hgrn(zb, zf, hgrn_lb, row(hgrn_norm[l]), l, B, T)
        qlt, qit, w_row, ckvn, ckvt, idxk = _dsa_prep(zb, zf, row(dsa_q_norm[l]), row(dsa_kv_norm[l]),
                                                      wq[l], wqi[l], wuk[l], B, T)
        y_dsa = _dsa_attn(qlt, qit, w_row, ckvn, ckvt, idxk, bias_tiles, wuvt[l], B, T)
        x2 = _merge(x2, y_ret, y_dsa, y_hg, row(ln_mix_pre[l]), wg, w_br_ret[l].astype(BF16),
                    w_br_dsa[l].astype(BF16), w_br_hgrn[l].astype(BF16), w_out[l].astype(BF16),
                    row(ln_mix_post[l]))
        x2 = _ffn(x2, row(ln_ffn_pre[l]), ffn_w_up[l].astype(BF16), ffn_conv_w[l], row(ffn_conv_b[l]),
                  ffn_w_down[l].astype(BF16), row(ln_ffn_post[l]), T)
    return x2.reshape(B, T, D)
```
